```python
import jax
import jax.numpy as jnp
from jax import lax
import numpy as np

D_MODEL = 1024
BATCH = 4
SEQ = 4096
DEPTH = 2

N_META = 16
D_MIX = D_MODEL
POOL_WIDTH = D_MIX // 2
POOL_WINDOWS = (2, 4, 8, 16)
N_POOL_GROUPS = len(POOL_WINDOWS)
POOL_GROUP_DIM = POOL_WIDTH // N_POOL_GROUPS
MLA_HEADS = 8
QK_NOPE_DIM = 64
QK_ROPE_DIM = 32
QK_HEAD_DIM = QK_NOPE_DIM + QK_ROPE_DIM
V_HEAD_DIM = (D_MIX - POOL_WIDTH) // MLA_HEADS
Q_LORA_RANK = 384
KV_LORA_RANK = 256
IN_WIDTH = POOL_WIDTH + Q_LORA_RANK + KV_LORA_RANK + QK_ROPE_DIM
D_FF = -(-8 * D_MODEL // (3 * 256)) * 256
ROPE_THETA = 10000.0
RMS_EPS = 1e-6
Q_BLOCK = 128

kernel_name = "hymba_pool_mla_hybrid"


def rms_norm(x, g):
    xf = x.astype(jnp.float32)
    y = xf * lax.rsqrt(jnp.mean(xf * xf, axis=-1, keepdims=True) + RMS_EPS)
    return (y * g.astype(jnp.float32)).astype(x.dtype)


def rope_tables(length):
    inv = 1.0 / (ROPE_THETA ** (jnp.arange(0, QK_ROPE_DIM, 2, dtype=jnp.float32) / QK_ROPE_DIM))
    ang = jnp.arange(length, dtype=jnp.float32)[:, None] * inv[None, :]
    return jnp.cos(ang), jnp.sin(ang)


def apply_rope(x, cos, sin):
    x1, x2 = jnp.split(x, 2, axis=-1)
    c = cos.astype(x.dtype)
    s = sin.astype(x.dtype)
    return jnp.concatenate([x1 * c - x2 * s, x1 * s + x2 * c], axis=-1)


def multiscale_pool(u, w_pool, pool_scale):
    B, L, _ = u.shape
    uf = u.astype(jnp.float32)
    cs = jnp.concatenate([jnp.zeros((B, 1, POOL_WIDTH), jnp.float32), lax.cumsum(uf, axis=1)], axis=1)
    t = jnp.arange(L)
    outs = []
    for g, w in enumerate(POOL_WINDOWS):
        sl = slice(g * POOL_GROUP_DIM, (g + 1) * POOL_GROUP_DIM)
        csg = cs[:, :, sl]
        start = jnp.maximum(t + 1 - w, 0)
        count = (t + 1 - start).astype(jnp.float32)
        mean = (csg[:, 1:] - csg[:, start]) / count[None, :, None]
        outs.append(mean - uf[:, :, sl])
    p = jnp.stack(outs, axis=2).astype(u.dtype)
    y = jnp.einsum('blgc,gcd->blgd', p, w_pool).reshape(B, L, POOL_WIDTH)
    return y * pool_scale.astype(y.dtype)


def causal_mla_attention(q_nope, q_rope, k_nope, k_rope, v):
    B, L, H, _ = q_nope.shape
    n_blocks = -(-L // Q_BLOCK)
    Lp = n_blocks * Q_BLOCK
    pad = Lp - L

    def padl(a):
        return jnp.pad(a, [(0, 0), (0, pad)] + [(0, 0)] * (a.ndim - 2))

    q_nope, q_rope, k_nope, k_rope, v = (padl(a) for a in (q_nope, q_rope, k_nope, k_rope, v))

    def blocks(a):
        return jnp.moveaxis(a.reshape(B, n_blocks, Q_BLOCK, *a.shape[2:]), 1, 0)

    qn_b, qr_b = blocks(q_nope), blocks(q_rope)
    key_pos = jnp.arange(Lp)
    scale = QK_HEAD_DIM ** -0.5

    def one_block(args):
        i, qn, qr = args
        s = (jnp.einsum('bqhd,bkhd->bhqk', qn, k_nope)
             + jnp.einsum('bqhr,bkr->bhqk', qr, k_rope)).astype(jnp.float32) * scale
        q_pos = i * Q_BLOCK + jnp.arange(Q_BLOCK)
        mask = key_pos[None, :] <= q_pos[:, None]
        s = jnp.where(mask[None, None], s, -jnp.inf)
        p = jax.nn.softmax(s, axis=-1).astype(v.dtype)
        return jnp.einsum('bhqk,bkhd->bqhd', p, v)

    out = lax.map(one_block, (jnp.arange(n_blocks), qn_b, qr_b))
    return jnp.moveaxis(out, 0, 1).reshape(B, Lp, H, V_HEAD_DIM)[:, :L]


def mla_group(c_q, c_kv, k_rope_raw, cos, sin, q_a_norm_g, w_q_b, kv_a_norm_g, w_kv_b, q_norm_g, k_norm_g):
    B, L, _ = c_q.shape
    q = (rms_norm(c_q, q_a_norm_g) @ w_q_b).reshape(B, L, MLA_HEADS, QK_HEAD_DIM)
    kv = (rms_norm(c_kv, kv_a_norm_g) @ w_kv_b).reshape(B, L, MLA_HEADS, QK_NOPE_DIM + V_HEAD_DIM)
    q_nope, q_rope = q[..., :QK_NOPE_DIM], q[..., QK_NOPE_DIM:]
    k_nope, v = kv[..., :QK_NOPE_DIM], kv[..., QK_NOPE_DIM:]
    q_nope = rms_norm(q_nope, q_norm_g[:QK_NOPE_DIM])
    q_rope = rms_norm(q_rope, q_norm_g[QK_NOPE_DIM:])
    k_nope = rms_norm(k_nope, k_norm_g[:QK_NOPE_DIM])
    k_rope = rms_norm(k_rope_raw, k_norm_g[QK_NOPE_DIM:])
    q_rope = apply_rope(q_rope, cos[:, None, :], sin[:, None, :])
    k_rope = apply_rope(k_rope, cos, sin)
    o = causal_mla_attention(q_nope, q_rope, k_nope, k_rope, v)
    return o.reshape(B, L, MLA_HEADS * V_HEAD_DIM)


def setup_inputs(seed: int = 0) -> dict:
    key = jax.random.key(seed)
    ks = jax.random.split(key, 18)
    f32 = jnp.float32
    nl = DEPTH

    def nrm(k, shape, scale):
        return jax.random.normal(k, shape, f32) * scale

    def gain(k, shape):
        return 1.0 + 0.05 * jax.random.normal(k, shape, f32)

    return {
        'x': nrm(ks[0], (BATCH, SEQ, D_MODEL), 1.0),
        'meta_tokens': nrm(ks[1], (N_META, D_MODEL), 1.0),
        'attn_norm_g': gain(ks[2], (nl, D_MODEL)),
        'w_in': nrm(ks[3], (nl, D_MODEL, IN_WIDTH), D_MODEL ** -0.5),
        'w_pool': nrm(ks[4], (nl, N_POOL_GROUPS, POOL_GROUP_DIM, POOL_GROUP_DIM), POOL_GROUP_DIM ** -0.5),
        'pool_scale': gain(ks[5], (nl, POOL_WIDTH)),
        'q_a_norm_g': gain(ks[6], (nl, Q_LORA_RANK)),
        'w_q_b': nrm(ks[7], (nl, Q_LORA_RANK, MLA_HEADS * QK_HEAD_DIM), Q_LORA_RANK ** -0.5),
        'kv_a_norm_g': gain(ks[8], (nl, KV_LORA_RANK)),
        'w_kv_b': nrm(ks[9], (nl, KV_LORA_RANK, MLA_HEADS * (QK_NOPE_DIM + V_HEAD_DIM)), KV_LORA_RANK ** -0.5),
        'q_norm_g': gain(ks[10], (nl, QK_HEAD_DIM)),
        'k_norm_g': gain(ks[11], (nl, QK_HEAD_DIM)),
        'w_out': nrm(ks[12], (nl, D_MIX, D_MODEL), D_MIX ** -0.5),
        'ffn_norm_g': gain(ks[13], (nl, D_MODEL)),
        'w_gate': nrm(ks[14], (nl, D_MODEL, D_FF), D_MODEL ** -0.5),
        'w_up': nrm(ks[15], (nl, D_MODEL, D_FF), D_MODEL ** -0.5),
        'w_down': nrm(ks[16], (nl, D_FF, D_MODEL), D_FF ** -0.5),
    }


def reference(x, meta_tokens, attn_norm_g, w_in, w_pool, pool_scale, q_a_norm_g, w_q_b,
              kv_a_norm_g, w_kv_b, q_norm_g, k_norm_g, w_out, ffn_norm_g, w_gate, w_up, w_down):
    B = x.shape[0]
    meta = jnp.broadcast_to(meta_tokens.astype(x.dtype)[None], (B, N_META, D_MODEL))
    h = jnp.concatenate([meta, x], axis=1)
    L = h.shape[1]
    cos, sin = rope_tables(L)
    s1 = POOL_WIDTH
    s2 = s1 + Q_LORA_RANK
    s3 = s2 + KV_LORA_RANK
    for l in range(DEPTH):
        z = rms_norm(h, attn_norm_g[l]) @ w_in[l]
        u_pool, c_q, c_kv, k_rope = jnp.split(z, [s1, s2, s3], axis=-1)
        y_pool = multiscale_pool(u_pool, w_pool[l], pool_scale[l])
        y_mla = mla_group(c_q, c_kv, k_rope, cos, sin, q_a_norm_g[l], w_q_b[l],
                          kv_a_norm_g[l], w_kv_b[l], q_norm_g[l], k_norm_g[l])
        h = h + jnp.concatenate([y_pool, y_mla], axis=-1) @ w_out[l]
        g = rms_norm(h, ffn_norm_g[l])
        h = h + (jax.nn.silu(g @ w_gate[l]) * (g @ w_up[l])) @ w_down[l]
    return h[:, N_META:]
```

```python
import functools

import jax
import jax.numpy as jnp
from jax import lax
from jax.experimental import pallas as pl
from jax.experimental.pallas import tpu as pltpu

D_MODEL = 1024
N_META = 16
POOL_WIDTH = 512
POOL_WINDOWS = (2, 4, 8, 16)
POOL_GROUP_DIM = 128
MLA_HEADS = 8
QK_NOPE_DIM = 64
QK_ROPE_DIM = 32
QK_HEAD_DIM = QK_NOPE_DIM + QK_ROPE_DIM
V_HEAD_DIM = 64
Q_LORA_RANK = 384
KV_LORA_RANK = 256
D_FF = 2816
ROPE_THETA = 10000.0
RMS_EPS = 1e-6

LANES = 128
MAX_POOL_WINDOW = max(POOL_WINDOWS)

SEQ_TILE = 384
FFN_ROW_TILE = 512
FF_CHUNK = 256
HEADS_PER_STEP = 2
IN_WIDTH_PADDED = 1280
VMEM_LIMIT_BYTES = 56 * 1024 * 1024

NEG_BIG = -1e30


def _rms_scale(x, width):
    return lax.rsqrt(jnp.sum(x * x, axis=-1, keepdims=True) * (1.0 / width) + RMS_EPS)


def _rope(xn, cos_t, sin_lo, sin_hi):
    half = QK_ROPE_DIM // 2
    return (xn * cos_t
            + pltpu.roll(xn, LANES - half, 1) * sin_lo
            + pltpu.roll(xn, half, 1) * sin_hi)


def _mix_in_kernel(h_ref, attn_g_ref, w_in_ref, w_pool_ref, pool_scale_ref,
                   qa_g_ref, wq_ref, kva_g_ref, wk_ref, wv_ref,
                   gq_ref, gkn_ref, gkr_ref, cos_ref, sin_lo_ref, sin_hi_ref,
                   ypool_ref, q_ref, k_ref, v_ref, ue_ref):
    i = pl.program_id(1)
    tm = h_ref.shape[1]
    halo = MAX_POOL_WINDOW

    x = h_ref[0]
    xn = (x * _rms_scale(x, D_MODEL) * attn_g_ref[...]).astype(jnp.bfloat16)
    z = jnp.dot(xn, w_in_ref[...], preferred_element_type=jnp.float32)

    u = z[:, :POOL_WIDTH]

    @pl.when(i == 0)
    def _():
        ue_ref[0:halo, :] = jnp.zeros((halo, POOL_WIDTH), jnp.float32)

    ue_ref[halo:, :] = u
    pos = i * tm + lax.broadcasted_iota(jnp.int32, (tm, 1), 0)
    for g, w in enumerate(POOL_WINDOWS):
        sl = slice(g * POOL_GROUP_DIM, (g + 1) * POOL_GROUP_DIM)
        ug = u[:, sl]
        s = ug
        for d in range(1, w):
            s = s + ue_ref[halo - d:halo - d + tm, sl]
        cnt = jnp.minimum(pos + 1, w).astype(jnp.float32)
        p = (s / cnt - ug).astype(jnp.bfloat16)
        yg = jnp.dot(p, w_pool_ref[g], preferred_element_type=jnp.float32)
        ypool_ref[0, :, sl] = (yg * pool_scale_ref[:, sl]).astype(jnp.bfloat16)
    ue_ref[0:halo, :] = ue_ref[tm:tm + halo, :]

    lane = lax.broadcasted_iota(jnp.int32, (1, LANES), 1)
    is_nope = lane < QK_NOPE_DIM
    cos_t = cos_ref[...]
    sin_lo = sin_lo_ref[...]
    sin_hi = sin_hi_ref[...]
    scale = QK_HEAD_DIM ** -0.5

    c_q = z[:, POOL_WIDTH:POOL_WIDTH + Q_LORA_RANK]
    cqn = (c_q * _rms_scale(c_q, Q_LORA_RANK) * qa_g_ref[...]).astype(jnp.bfloat16)
    qf = jnp.dot(cqn, wq_ref[...], preferred_element_type=jnp.float32)

    kv0 = POOL_WIDTH + Q_LORA_RANK
    c_kv = z[:, kv0:kv0 + KV_LORA_RANK]
    ckvn = (c_kv * _rms_scale(c_kv, KV_LORA_RANK) * kva_g_ref[...]).astype(jnp.bfloat16)
    kf = jnp.dot(ckvn, wk_ref[...], preferred_element_type=jnp.float32)
    v_ref[0] = jnp.dot(ckvn, wv_ref[...],
                       preferred_element_type=jnp.float32).astype(jnp.bfloat16)

    kr = z[:, kv0 + KV_LORA_RANK:]
    krn = kr * _rms_scale(kr, QK_ROPE_DIM) * gkr_ref[...]
    kr_rot = _rope(krn, cos_t, sin_lo, sin_hi)

    for hd in range(MLA_HEADS):
        sl = slice(hd * LANES, (hd + 1) * LANES)
        qh = qf[:, sl]
        sq = qh * qh
        ss_n = jnp.sum(jnp.where(is_nope, sq, 0.0), axis=-1, keepdims=True)
        ss_r = jnp.sum(jnp.where(is_nope, 0.0, sq), axis=-1, keepdims=True)
        r_n = lax.rsqrt(ss_n * (1.0 / QK_NOPE_DIM) + RMS_EPS)
        r_r = lax.rsqrt(ss_r * (1.0 / QK_ROPE_DIM) + RMS_EPS)
        qn = qh * jnp.where(is_nope, r_n, r_r) * gq_ref[...]
        q_ref[0, hd] = (_rope(qn, cos_t, sin_lo, sin_hi) * scale).astype(jnp.bfloat16)

        kh = kf[:, sl]
        kn = kh * _rms_scale(kh, QK_NOPE_DIM) * gkn_ref[...]
        k_ref[0, hd] = (kn + kr_rot).astype(jnp.bfloat16)


def _attn_kernel(q_ref, k_ref, v_ref, o_ref):
    i = pl.program_id(2)
    tq = q_ref.shape[2]
    tk = tq
    row = lax.broadcasted_iota(jnp.int32, (tq, tk), 0)
    col = lax.broadcasted_iota(jnp.int32, (tq, tk), 1)
    causal = col <= row

    outs = []
    for hh in range(HEADS_PER_STEP):
        q = q_ref[0, hh]

        def step(j, carry, masked, q=q, hh=hh):
            m, l, acc = carry
            start = pl.multiple_of(j * tk, tk)
            kj = k_ref[0, hh, pl.ds(start, tk), :]
            vj = v_ref[0, pl.ds(start, tk), :]
            s = lax.dot_general(q, kj, (((1,), (1,)), ((), ())),
                                preferred_element_type=jnp.float32)
            if masked:
                s = jnp.where(causal, s, NEG_BIG)
            m_new = jnp.maximum(m, jnp.max(s, axis=-1, keepdims=True))
            alpha = jnp.exp(m - m_new)
            p = jnp.exp(s - m_new)
            l = alpha * l + jnp.sum(p, axis=-1, keepdims=True)
            acc = alpha * acc + jnp.dot(p.astype(jnp.bfloat16), vj,
                                        preferred_element_type=jnp.float32)
            return m_new, l, acc

        init = (jnp.full((tq, 1), NEG_BIG, jnp.float32),
                jnp.zeros((tq, 1), jnp.float32),
                jnp.zeros((tq, LANES), jnp.float32))
        carry = lax.fori_loop(0, i, functools.partial(step, masked=False), init)
        _, l, acc = step(i, carry, masked=True)
        outs.append(acc / l)

    lane = lax.broadcasted_iota(jnp.int32, (1, LANES), 1)
    o_ref[0] = jnp.where(lane < V_HEAD_DIM, outs[0], outs[1]).astype(jnp.bfloat16)


def _out_ffn_kernel(h_ref, ypool_ref, ymla_ref, wo_pool_ref, wo_mla_ref, ffn_g_ref,
                    wg_ref, wu_ref, wd_ref, o_ref, acc_ref):
    h1 = (h_ref[...]
          + jnp.dot(ypool_ref[...], wo_pool_ref[...], preferred_element_type=jnp.float32)
          + jnp.dot(ymla_ref[...], wo_mla_ref[...], preferred_element_type=jnp.float32))
    g = (h1 * _rms_scale(h1, D_MODEL) * ffn_g_ref[...]).astype(jnp.bfloat16)
    acc_ref[...] = h1

    def chunk(c, carry):
        gate = jnp.dot(g, wg_ref[c], preferred_element_type=jnp.float32)
        up = jnp.dot(g, wu_ref[c], preferred_element_type=jnp.float32)
        act = (gate * (1.0 / (1.0 + jnp.exp(-gate))) * up).astype(jnp.bfloat16)
        acc_ref[...] += jnp.dot(act, wd_ref[c], preferred_element_type=jnp.float32)
        return carry

    lax.fori_loop(0, wg_ref.shape[0], chunk, 0)
    o_ref[...] = acc_ref[...]


def _const_spec(shape):
    nd = len(shape)
    return pl.BlockSpec(shape, lambda *_: (0,) * nd)


def _resident_spec(shape):
    nd = len(shape)
    return pl.BlockSpec(shape, lambda *_: (0,) * nd, pipeline_mode=pl.Buffered(1))


def _mix_in(h, attn_g, w_in_p, w_pool, pool_scale, qa_g, wq_p, kva_g, wk_p, wv,
            gq, gkn, gkr, cos_t, sin_lo, sin_hi):
    batch, lp, _ = h.shape
    tm = SEQ_TILE
    n_tiles = lp // tm
    row_spec = lambda width: pl.BlockSpec((1, tm, width), lambda b, i: (b, i, 0))
    head_spec = pl.BlockSpec((1, MLA_HEADS, tm, LANES), lambda b, i: (b, 0, i, 0))
    table_spec = pl.BlockSpec((tm, LANES), lambda b, i: (i, 0))
    return pl.pallas_call(
        _mix_in_kernel,
        grid=(batch, n_tiles),
        in_specs=[
            row_spec(D_MODEL),
            _const_spec(attn_g.shape), _const_spec(w_in_p.shape), _const_spec(w_pool.shape),
            _const_spec(pool_scale.shape), _const_spec(qa_g.shape), _const_spec(wq_p.shape),
            _const_spec(kva_g.shape), _const_spec(wk_p.shape), _const_spec(wv.shape),
            _const_spec(gq.shape), _const_spec(gkn.shape), _const_spec(gkr.shape),
            table_spec, table_spec, table_spec,
        ],
        out_specs=[row_spec(POOL_WIDTH), head_spec, head_spec,
                   row_spec(MLA_HEADS * V_HEAD_DIM)],
        out_shape=[
            jax.ShapeDtypeStruct((batch, lp, POOL_WIDTH), jnp.bfloat16),
            jax.ShapeDtypeStruct((batch, MLA_HEADS, lp, LANES), jnp.bfloat16),
            jax.ShapeDtypeStruct((batch, MLA_HEADS, lp, LANES), jnp.bfloat16),
            jax.ShapeDtypeStruct((batch, lp, MLA_HEADS * V_HEAD_DIM), jnp.bfloat16),
        ],
        scratch_shapes=[pltpu.VMEM((tm + MAX_POOL_WINDOW, POOL_WIDTH), jnp.float32)],
        compiler_params=pltpu.CompilerParams(
            dimension_semantics=("arbitrary", "arbitrary"),
            vmem_limit_bytes=VMEM_LIMIT_BYTES),
        name="mix_in",
    )(h, attn_g, w_in_p, w_pool, pool_scale, qa_g, wq_p, kva_g, wk_p, wv,
      gq, gkn, gkr, cos_t, sin_lo, sin_hi)


def _attention(q, k, v):
    batch, heads, lp, _ = q.shape
    tq = SEQ_TILE
    hps = HEADS_PER_STEP
    return pl.pallas_call(
        _attn_kernel,
        grid=(batch, heads // hps, lp // tq),
        in_specs=[
            pl.BlockSpec((1, hps, tq, LANES), lambda b, hp, i: (b, hp, i, 0)),
            pl.BlockSpec((1, hps, lp, LANES), lambda b, hp, i: (b, hp, 0, 0)),
            pl.BlockSpec((1, lp, LANES), lambda b, hp, i: (b, 0, hp)),
        ],
        out_specs=pl.BlockSpec((1, tq, LANES), lambda b, hp, i: (b, i, hp)),
        out_shape=jax.ShapeDtypeStruct((batch, lp, heads * V_HEAD_DIM), jnp.bfloat16),
        compiler_params=pltpu.CompilerParams(
            dimension_semantics=("arbitrary", "arbitrary", "arbitrary"),
            vmem_limit_bytes=VMEM_LIMIT_BYTES),
        name="mla_attn",
    )(q, k, v)


def _out_ffn(h2d, ypool2d, ymla2d, wo_pool, wo_mla, ffn_g, wg, wu, wd):
    rows, d = h2d.shape
    tm = FFN_ROW_TILE
    row_spec = lambda width: pl.BlockSpec((tm, width), lambda i: (i, 0))
    return pl.pallas_call(
        _out_ffn_kernel,
        grid=(rows // tm,),
        in_specs=[
            row_spec(d), row_spec(POOL_WIDTH), row_spec(MLA_HEADS * V_HEAD_DIM),
            _resident_spec(wo_pool.shape), _resident_spec(wo_mla.shape),
            _const_spec(ffn_g.shape),
            _resident_spec(wg.shape), _resident_spec(wu.shape), _resident_spec(wd.shape),
        ],
        out_specs=row_spec(d),
        out_shape=jax.ShapeDtypeStruct((rows, d), jnp.float32),
        scratch_shapes=[pltpu.VMEM((tm, d), jnp.float32)],
        compiler_params=pltpu.CompilerParams(
            dimension_semantics=("arbitrary",),
            vmem_limit_bytes=VMEM_LIMIT_BYTES),
        name="out_ffn",
    )(h2d, ypool2d, ymla2d, wo_pool, wo_mla, ffn_g, wg, wu, wd)


def _rope_tables(lp):
    half = QK_ROPE_DIM // 2
    inv = 1.0 / (ROPE_THETA ** (jnp.arange(0, QK_ROPE_DIM, 2, dtype=jnp.float32) / QK_ROPE_DIM))
    ang = jnp.arange(lp, dtype=jnp.float32)[:, None] * inv[None, :]
    cos, sin = jnp.cos(ang), jnp.sin(ang)
    zeros = lambda n: jnp.zeros((lp, n), jnp.float32)
    tail = LANES - QK_HEAD_DIM
    cos_t = jnp.concatenate([jnp.ones((lp, QK_NOPE_DIM), jnp.float32), cos, cos, zeros(tail)], axis=1)
    sin_lo = jnp.concatenate([zeros(QK_NOPE_DIM), -sin, zeros(half), zeros(tail)], axis=1)
    sin_hi = jnp.concatenate([zeros(QK_NOPE_DIM), zeros(half), sin, zeros(tail)], axis=1)
    return cos_t, sin_lo, sin_hi


def _pad_lanes(a, width):
    return jnp.pad(a, [(0, 0)] * (a.ndim - 1) + [(0, width - a.shape[-1])])


def kernel(x, meta_tokens, attn_norm_g, w_in, w_pool, pool_scale, q_a_norm_g, w_q_b,
           kv_a_norm_g, w_kv_b, q_norm_g, k_norm_g, w_out, ffn_norm_g, w_gate, w_up, w_down):
    batch, seq, d = x.shape
    depth = w_in.shape[0]
    length = N_META + seq
    lp = -(-length // SEQ_TILE) * SEQ_TILE
    assert (batch * lp) % FFN_ROW_TILE == 0 and D_FF % FF_CHUNK == 0
    bf16 = jnp.bfloat16
    n_chunks = D_FF // FF_CHUNK

    meta = jnp.broadcast_to(meta_tokens.astype(x.dtype)[None], (batch, N_META, d))
    h = jnp.concatenate([meta, x, jnp.zeros((batch, lp - length, d), x.dtype)], axis=1)
    cos_t, sin_lo, sin_hi = _rope_tables(lp)

    s1 = POOL_WIDTH
    s3 = s1 + Q_LORA_RANK + KV_LORA_RANK
    for l in range(depth):
        rope_block = jnp.pad(w_in[l][:, s3:], ((0, 0), (QK_NOPE_DIM, LANES - QK_HEAD_DIM)))
        w_in_p = jnp.concatenate([w_in[l][:, :s3], rope_block], axis=1).astype(bf16)
        wq_p = _pad_lanes(w_q_b[l].reshape(Q_LORA_RANK, MLA_HEADS, QK_HEAD_DIM), LANES)
        wq_p = wq_p.reshape(Q_LORA_RANK, MLA_HEADS * LANES).astype(bf16)
        wkv = w_kv_b[l].reshape(KV_LORA_RANK, MLA_HEADS, QK_NOPE_DIM + V_HEAD_DIM)
        wk_p = _pad_lanes(wkv[:, :, :QK_NOPE_DIM], LANES)
        wk_p = wk_p.reshape(KV_LORA_RANK, MLA_HEADS * LANES).astype(bf16)
        wv = wkv[:, :, QK_NOPE_DIM:].reshape(KV_LORA_RANK, MLA_HEADS * V_HEAD_DIM).astype(bf16)
        gq = _pad_lanes(q_norm_g[l][None, :], LANES)
        gkn = _pad_lanes(k_norm_g[l][None, :QK_NOPE_DIM], LANES)
        gkr = jnp.pad(k_norm_g[l][None, QK_NOPE_DIM:], ((0, 0), (QK_NOPE_DIM, LANES - QK_HEAD_DIM)))

        ypool, q, k, v = _mix_in(
            h, attn_norm_g[l][None, :], w_in_p, w_pool[l].astype(bf16), pool_scale[l][None, :],
            q_a_norm_g[l][None, :], wq_p, kv_a_norm_g[l][None, :], wk_p, wv,
            gq, gkn, gkr, cos_t, sin_lo, sin_hi)
        ymla = _attention(q, k, v)

        wo = w_out[l].astype(bf16)
        wg = w_gate[l].reshape(d, n_chunks, FF_CHUNK).transpose(1, 0, 2).astype(bf16)
        wu = w_up[l].reshape(d, n_chunks, FF_CHUNK).transpose(1, 0, 2).astype(bf16)
        wd = w_down[l].reshape(n_chunks, FF_CHUNK, d).astype(bf16)
        h = _out_ffn(h.reshape(batch * lp, d), ypool.reshape(batch * lp, POOL_WIDTH),
                     ymla.reshape(batch * lp, MLA_HEADS * V_HEAD_DIM),
                     wo[:POOL_WIDTH], wo[POOL_WIDTH:], ffn_norm_g[l][None, :],
                     wg, wu, wd).reshape(batch, lp, d)
    return h[:, N_META:length]
```

```python
import functools

import jax
import jax.numpy as jnp
from jax import lax
from jax.experimental import pallas as pl
from jax.experimental.pallas import tpu as pltpu

D_MODEL = 1024
N_META = 16
POOL_WIDTH = 512
POOL_WINDOWS = (2, 4, 8, 16)
POOL_GROUP_DIM = 128
MLA_HEADS = 8
QK_NOPE_DIM = 64
QK_ROPE_DIM = 32
QK_HEAD_DIM = QK_NOPE_DIM + QK_ROPE_DIM
V_HEAD_DIM = 64
Q_LORA_RANK = 384
KV_LORA_RANK = 256
D_FF = 2816
ROPE_THETA = 10000.0
RMS_EPS = 1e-6

LANES = 128
MAX_POOL_WINDOW = max(POOL_WINDOWS)

SEQ_TILE = 384
FFN_ROW_TILE = 512
FF_CHUNK = 256
HEADS_PER_STEP = 4
IN_WIDTH_PADDED = 1280
VMEM_LIMIT_BYTES = 56 * 1024 * 1024

NEG_BIG = -1e30
LOG2_E = 1.4426950408889634


def _rms_scale(x, width):
    return lax.rsqrt(jnp.sum(x * x, axis=-1, keepdims=True) * (1.0 / width) + RMS_EPS)


def _rope(xn, cos_t, sin_lo, sin_hi):
    half = QK_ROPE_DIM // 2
    return (xn * cos_t
            + pltpu.roll(xn, LANES - half, 1) * sin_lo
            + pltpu.roll(xn, half, 1) * sin_hi)


def _mix_in_kernel(h_ref, attn_g_ref, w_in_ref, w_pool_ref, pool_scale_ref,
                   qa_g_ref, wq_ref, kva_g_ref, wk_ref, wv_ref,
                   gq_ref, gkn_ref, gkr_ref, cos_ref, sin_lo_ref, sin_hi_ref,
                   ypool_ref, q_ref, k_ref, v_ref, ue_ref):
    i = pl.program_id(1)
    tm = h_ref.shape[1]
    halo = MAX_POOL_WINDOW

    x = h_ref[0]
    xn = (x * _rms_scale(x, D_MODEL) * attn_g_ref[...]).astype(jnp.bfloat16)
    z = jnp.dot(xn, w_in_ref[...], preferred_element_type=jnp.float32)

    u = z[:, :POOL_WIDTH]

    @pl.when(i == 0)
    def _():
        ue_ref[0:halo, :] = jnp.zeros((halo, POOL_WIDTH), jnp.float32)

    ue_ref[halo:, :] = u
    pos = i * tm + lax.broadcasted_iota(jnp.int32, (tm, 1), 0)
    for g, w in enumerate(POOL_WINDOWS):
        sl = slice(g * POOL_GROUP_DIM, (g + 1) * POOL_GROUP_DIM)
        ug = u[:, sl]
        s = ug
        for d in range(1, w):
            s = s + ue_ref[halo - d:halo - d + tm, sl]
        cnt = jnp.minimum(pos + 1, w).astype(jnp.float32)
        p = (s / cnt - ug).astype(jnp.bfloat16)
        yg = jnp.dot(p, w_pool_ref[g], preferred_element_type=jnp.float32)
        ypool_ref[0, :, sl] = (yg * pool_scale_ref[:, sl]).astype(jnp.bfloat16)
    ue_ref[0:halo, :] = ue_ref[tm:tm + halo, :]

    lane = lax.broadcasted_iota(jnp.int32, (1, LANES), 1)
    is_nope = lane < QK_NOPE_DIM
    cos_t = cos_ref[...]
    sin_lo = sin_lo_ref[...]
    sin_hi = sin_hi_ref[...]
    scale = QK_HEAD_DIM ** -0.5 * LOG2_E

    c_q = z[:, POOL_WIDTH:POOL_WIDTH + Q_LORA_RANK]
    cqn = (c_q * _rms_scale(c_q, Q_LORA_RANK) * qa_g_ref[...]).astype(jnp.bfloat16)
    qf = jnp.dot(cqn, wq_ref[...], preferred_element_type=jnp.float32)

    kv0 = POOL_WIDTH + Q_LORA_RANK
    c_kv = z[:, kv0:kv0 + KV_LORA_RANK]
    ckvn = (c_kv * _rms_scale(c_kv, KV_LORA_RANK) * kva_g_ref[...]).astype(jnp.bfloat16)
    kf = jnp.dot(ckvn, wk_ref[...], preferred_element_type=jnp.float32)
    v_ref[0] = jnp.dot(ckvn, wv_ref[...],
                       preferred_element_type=jnp.float32).astype(jnp.bfloat16)

    kr = z[:, kv0 + KV_LORA_RANK:]
    krn = kr * _rms_scale(kr, QK_ROPE_DIM) * gkr_ref[...]
    kr_rot = _rope(krn, cos_t, sin_lo, sin_hi)

    for hd in range(MLA_HEADS):
        sl = slice(hd * LANES, (hd + 1) * LANES)
        qh = qf[:, sl]
        sq = qh * qh
        ss_n = jnp.sum(jnp.where(is_nope, sq, 0.0), axis=-1, keepdims=True)
        ss_r = jnp.sum(jnp.where(is_nope, 0.0, sq), axis=-1, keepdims=True)
        r_n = lax.rsqrt(ss_n * (1.0 / QK_NOPE_DIM) + RMS_EPS)
        r_r = lax.rsqrt(ss_r * (1.0 / QK_ROPE_DIM) + RMS_EPS)
        qn = qh * jnp.where(is_nope, r_n, r_r) * gq_ref[...]
        q_ref[0, hd] = (_rope(qn, cos_t, sin_lo, sin_hi) * scale).astype(jnp.bfloat16)

        kh = kf[:, sl]
        kn = kh * _rms_scale(kh, QK_NOPE_DIM) * gkn_ref[...]
        k_ref[0, hd] = (kn + kr_rot).astype(jnp.bfloat16)


def _attn_kernel(q_ref, k_ref, v_ref, o_ref):
    i = pl.program_id(2)
    tq = q_ref.shape[2]
    tk = tq
    row = lax.broadcasted_iota(jnp.int32, (tq, tk), 0)
    col = lax.broadcasted_iota(jnp.int32, (tq, tk), 1)
    causal = col <= row

    qs = [q_ref[0, hh] for hh in range(HEADS_PER_STEP)]

    def step(j, carry, masked):
        start = pl.multiple_of(j * tk, tk)
        new = []
        scores = []
        for hh in range(HEADS_PER_STEP):
            kj = k_ref[0, hh, pl.ds(start, tk), :]
            scores.append(lax.dot_general(qs[hh], kj, (((1,), (1,)), ((), ())),
                                          preferred_element_type=jnp.float32))
        for hh in range(HEADS_PER_STEP):
            m, l, acc = carry[hh]
            vj = v_ref[0, pl.ds(start, tk), (hh // 2) * LANES:(hh // 2 + 1) * LANES]
            s = scores[hh]
            if masked:
                s = jnp.where(causal, s, NEG_BIG)
            m_new = jnp.maximum(m, jnp.max(s, axis=-1, keepdims=True))
            alpha = jnp.exp2(m - m_new)
            p = jnp.exp2(s - m_new)
            l = alpha * l + jnp.sum(p, axis=-1, keepdims=True)
            acc = alpha * acc + jnp.dot(p.astype(jnp.bfloat16), vj,
                                        preferred_element_type=jnp.float32)
            new.append((m_new, l, acc))
        return tuple(new)

    init = tuple((jnp.full((tq, 1), NEG_BIG, jnp.float32),
                  jnp.zeros((tq, 1), jnp.float32),
                  jnp.zeros((tq, LANES), jnp.float32)) for _ in range(HEADS_PER_STEP))
    carry = lax.fori_loop(0, i, functools.partial(step, masked=False), init)
    carry = step(i, carry, masked=True)
    outs = [acc / l for _, l, acc in carry]

    lane = lax.broadcasted_iota(jnp.int32, (1, LANES), 1)
    for pair in range(HEADS_PER_STEP // 2):
        o_ref[0, :, pair * LANES:(pair + 1) * LANES] = jnp.where(
            lane < V_HEAD_DIM, outs[2 * pair], outs[2 * pair + 1]).astype(jnp.bfloat16)


def _out_ffn_kernel(h_ref, ypool_ref, ymla_ref, wo_pool_ref, wo_mla_ref, ffn_g_ref,
                    wg_ref, wu_ref, wd_ref, o_ref, acc_ref):
    h1 = (h_ref[...]
          + jnp.dot(ypool_ref[...], wo_pool_ref[...], preferred_element_type=jnp.float32)
          + jnp.dot(ymla_ref[...], wo_mla_ref[...], preferred_element_type=jnp.float32))
    g = (h1 * _rms_scale(h1, D_MODEL) * ffn_g_ref[...]).astype(jnp.bfloat16)
    acc_ref[...] = h1

    def chunk(c, carry):
        gate = jnp.dot(g, wg_ref[c], preferred_element_type=jnp.float32)
        up = jnp.dot(g, wu_ref[c], preferred_element_type=jnp.float32)
        act = (gate * (1.0 / (1.0 + jnp.exp(-gate))) * up).astype(jnp.bfloat16)
        acc_ref[...] += jnp.dot(act, wd_ref[c], preferred_element_type=jnp.float32)
        return carry

    lax.fori_loop(0, wg_ref.shape[0], chunk, 0)
    o_ref[...] = acc_ref[...]


def _const_spec(shape):
    nd = len(shape)
    return pl.BlockSpec(shape, lambda *_: (0,) * nd)


def _resident_spec(shape):
    nd = len(shape)
    return pl.BlockSpec(shape, lambda *_: (0,) * nd, pipeline_mode=pl.Buffered(1))


def _mix_in(h, attn_g, w_in_p, w_pool, pool_scale, qa_g, wq_p, kva_g, wk_p, wv,
            gq, gkn, gkr, cos_t, sin_lo, sin_hi):
    batch, lp, _ = h.shape
    tm = SEQ_TILE
    n_tiles = lp // tm
    row_spec = lambda width: pl.BlockSpec((1, tm, width), lambda b, i: (b, i, 0))
    head_spec = pl.BlockSpec((1, MLA_HEADS, tm, LANES), lambda b, i: (b, 0, i, 0))
    table_spec = pl.BlockSpec((tm, LANES), lambda b, i: (i, 0))
    return pl.pallas_call(
        _mix_in_kernel,
        grid=(batch, n_tiles),
        in_specs=[
            row_spec(D_MODEL),
            _const_spec(attn_g.shape), _const_spec(w_in_p.shape), _const_spec(w_pool.shape),
            _const_spec(pool_scale.shape), _const_spec(qa_g.shape), _const_spec(wq_p.shape),
            _const_spec(kva_g.shape), _const_spec(wk_p.shape), _const_spec(wv.shape),
            _const_spec(gq.shape), _const_spec(gkn.shape), _const_spec(gkr.shape),
            table_spec, table_spec, table_spec,
        ],
        out_specs=[row_spec(POOL_WIDTH), head_spec, head_spec,
                   row_spec(MLA_HEADS * V_HEAD_DIM)],
        out_shape=[
            jax.ShapeDtypeStruct((batch, lp, POOL_WIDTH), jnp.bfloat16),
            jax.ShapeDtypeStruct((batch, MLA_HEADS, lp, LANES), jnp.bfloat16),
            jax.ShapeDtypeStruct((batch, MLA_HEADS, lp, LANES), jnp.bfloat16),
            jax.ShapeDtypeStruct((batch, lp, MLA_HEADS * V_HEAD_DIM), jnp.bfloat16),
        ],
        scratch_shapes=[pltpu.VMEM((tm + MAX_POOL_WINDOW, POOL_WIDTH), jnp.float32)],
        compiler_params=pltpu.CompilerParams(
            dimension_semantics=("arbitrary", "arbitrary"),
            vmem_limit_bytes=VMEM_LIMIT_BYTES),
        name="mix_in",
    )(h, attn_g, w_in_p, w_pool, pool_scale, qa_g, wq_p, kva_g, wk_p, wv,
      gq, gkn, gkr, cos_t, sin_lo, sin_hi)


def _attention(q, k, v):
    batch, heads, lp, _ = q.shape
    tq = SEQ_TILE
    hps = HEADS_PER_STEP
    return pl.pallas_call(
        _attn_kernel,
        grid=(batch, heads // hps, lp // tq),
        in_specs=[
            pl.BlockSpec((1, hps, tq, LANES), lambda b, hp, i: (b, hp, i, 0)),
            pl.BlockSpec((1, hps, lp, LANES), lambda b, hp, i: (b, hp, 0, 0)),
            pl.BlockSpec((1, lp, hps * V_HEAD_DIM), lambda b, hp, i: (b, 0, hp)),
        ],
        out_specs=pl.BlockSpec((1, tq, hps * V_HEAD_DIM), lambda b, hp, i: (b, i, hp)),
        out_shape=jax.ShapeDtypeStruct((batch, lp, heads * V_HEAD_DIM), jnp.bfloat16),
        compiler_params=pltpu.CompilerParams(
            dimension_semantics=("arbitrary", "arbitrary", "arbitrary"),
            vmem_limit_bytes=VMEM_LIMIT_BYTES),
        name="mla_attn",
    )(q, k, v)


def _out_ffn(h2d, ypool2d, ymla2d, wo_pool, wo_mla, ffn_g, wg, wu, wd):
    rows, d = h2d.shape
    tm = FFN_ROW_TILE
    row_spec = lambda width: pl.BlockSpec((tm, width), lambda i: (i, 0))
    return pl.pallas_call(
        _out_ffn_kernel,
        grid=(rows // tm,),
        in_specs=[
            row_spec(d), row_spec(POOL_WIDTH), row_spec(MLA_HEADS * V_HEAD_DIM),
            _resident_spec(wo_pool.shape), _resident_spec(wo_mla.shape),
            _const_spec(ffn_g.shape),
            _resident_spec(wg.shape), _resident_spec(wu.shape), _resident_spec(wd.shape),
        ],
        out_specs=row_spec(d),
        out_shape=jax.ShapeDtypeStruct((rows, d), jnp.float32),
        scratch_shapes=[pltpu.VMEM((tm, d), jnp.float32)],
        compiler_params=pltpu.CompilerParams(
            dimension_semantics=("arbitrary",),
            vmem_limit_bytes=VMEM_LIMIT_BYTES),
        name="out_ffn",
    )(h2d, ypool2d, ymla2d, wo_pool, wo_mla, ffn_g, wg, wu, wd)


def _rope_tables(lp):
    half = QK_ROPE_DIM // 2
    inv = 1.0 / (ROPE_THETA ** (jnp.arange(0, QK_ROPE_DIM, 2, dtype=jnp.float32) / QK_ROPE_DIM))
    ang = jnp.arange(lp, dtype=jnp.float32)[:, None] * inv[None, :]
    cos, sin = jnp.cos(ang), jnp.sin(ang)
    zeros = lambda n: jnp.zeros((lp, n), jnp.float32)
    tail = LANES - QK_HEAD_DIM
    cos_t = jnp.concatenate([jnp.ones((lp, QK_NOPE_DIM), jnp.float32), cos, cos, zeros(tail)], axis=1)
    sin_lo = jnp.concatenate([zeros(QK_NOPE_DIM), -sin, zeros(half), zeros(tail)], axis=1)
    sin_hi = jnp.concatenate([zeros(QK_NOPE_DIM), zeros(half), sin, zeros(tail)], axis=1)
    return cos_t, sin_lo, sin_hi


def _pad_lanes(a, width):
    return jnp.pad(a, [(0, 0)] * (a.ndim - 1) + [(0, width - a.shape[-1])])


def kernel(x, meta_tokens, attn_norm_g, w_in, w_pool, pool_scale, q_a_norm_g, w_q_b,
           kv_a_norm_g, w_kv_b, q_norm_g, k_norm_g, w_out, ffn_norm_g, w_gate, w_up, w_down):
    batch, seq, d = x.shape
    depth = w_in.shape[0]
    length = N_META + seq
    lp = -(-length // SEQ_TILE) * SEQ_TILE
    assert (batch * lp) % FFN_ROW_TILE == 0 and D_FF % FF_CHUNK == 0
    bf16 = jnp.bfloat16
    n_chunks = D_FF // FF_CHUNK

    meta = jnp.broadcast_to(meta_tokens.astype(x.dtype)[None], (batch, N_META, d))
    h = jnp.concatenate([meta, x, jnp.zeros((batch, lp - length, d), x.dtype)], axis=1)
    cos_t, sin_lo, sin_hi = _rope_tables(lp)

    s1 = POOL_WIDTH
    s3 = s1 + Q_LORA_RANK + KV_LORA_RANK
    for l in range(depth):
        rope_block = jnp.pad(w_in[l][:, s3:], ((0, 0), (QK_NOPE_DIM, LANES - QK_HEAD_DIM)))
        w_in_p = jnp.concatenate([w_in[l][:, :s3], rope_block], axis=1).astype(bf16)
        wq_p = _pad_lanes(w_q_b[l].reshape(Q_LORA_RANK, MLA_HEADS, QK_HEAD_DIM), LANES)
        wq_p = wq_p.reshape(Q_LORA_RANK, MLA_HEADS * LANES).astype(bf16)
        wkv = w_kv_b[l].reshape(KV_LORA_RANK, MLA_HEADS, QK_NOPE_DIM + V_HEAD_DIM)
        wk_p = _pad_lanes(wkv[:, :, :QK_NOPE_DIM], LANES)
        wk_p = wk_p.reshape(KV_LORA_RANK, MLA_HEADS * LANES).astype(bf16)
        wv = wkv[:, :, QK_NOPE_DIM:].reshape(KV_LORA_RANK, MLA_HEADS * V_HEAD_DIM).astype(bf16)
        gq = _pad_lanes(q_norm_g[l][None, :], LANES)
        gkn = _pad_lanes(k_norm_g[l][None, :QK_NOPE_DIM], LANES)
        gkr = jnp.pad(k_norm_g[l][None, QK_NOPE_DIM:], ((0, 0), (QK_NOPE_DIM, LANES - QK_HEAD_DIM)))

        ypool, q, k, v = _mix_in(
            h, attn_norm_g[l][None, :], w_in_p, w_pool[l].astype(bf16), pool_scale[l][None, :],
            q_a_norm_g[l][None, :], wq_p, kv_a_norm_g[l][None, :], wk_p, wv,
            gq, gkn, gkr, cos_t, sin_lo, sin_hi)
        ymla = _attention(q, k, v)

        wo = w_out[l].astype(bf16)
        wg = w_gate[l].reshape(d, n_chunks, FF_CHUNK).transpose(1, 0, 2).astype(bf16)
        wu = w_up[l].reshape(d, n_chunks, FF_CHUNK).transpose(1, 0, 2).astype(bf16)
        wd = w_down[l].reshape(n_chunks, FF_CHUNK, d).astype(bf16)
        h = _out_ffn(h.reshape(batch * lp, d), ypool.reshape(batch * lp, POOL_WIDTH),
                     ymla.reshape(batch * lp, MLA_HEADS * V_HEAD_DIM),
                     wo[:POOL_WIDTH], wo[POOL_WIDTH:], ffn_norm_g[l][None, :],
                     wg, wu, wd).reshape(batch, lp, d)
    return h[:, N_META:length]
```

```python
import functools

import jax
import jax.numpy as jnp
from jax import lax
from jax.experimental import pallas as pl
from jax.experimental.pallas import tpu as pltpu

D_MODEL = 1024
N_META = 16
POOL_WIDTH = 512
POOL_WINDOWS = (2, 4, 8, 16)
POOL_GROUP_DIM = 128
MLA_HEADS = 8
QK_NOPE_DIM = 64
QK_ROPE_DIM = 32
QK_HEAD_DIM = QK_NOPE_DIM + QK_ROPE_DIM
V_HEAD_DIM = 64
Q_LORA_RANK = 384
KV_LORA_RANK = 256
D_FF = 2816
ROPE_THETA = 10000.0
RMS_EPS = 1e-6

LANES = 128
SUBLANES = 8
assert POOL_WINDOWS == tuple(2 ** (k + 1) for k in range(len(POOL_WINDOWS)))
POOL_HALO = SUBLANES * len(POOL_WINDOWS)
assert POOL_HALO >= max(POOL_WINDOWS)

SEQ_TILE = 384
FFN_ROW_TILE = 512
FF_CHUNK = 256
HEADS_PER_STEP = 8
IN_WIDTH_PADDED = 1280
VMEM_LIMIT_BYTES = 56 * 1024 * 1024

NEG_BIG = -1e30
LOG2_E = 1.4426950408889634


def _rms_scale(x, width):
    return lax.rsqrt(jnp.sum(x * x, axis=-1, keepdims=True) * (1.0 / width) + RMS_EPS)


def _rope(xn, cos_t, sin_lo, sin_hi):
    half = QK_ROPE_DIM // 2
    return (xn * cos_t
            + pltpu.roll(xn, LANES - half, 1) * sin_lo
            + pltpu.roll(xn, half, 1) * sin_hi)


def _mix_in_kernel(h_ref, attn_g_ref, w_in_ref, w_pool_ref, pool_scale_ref,
                   qa_g_ref, wq_ref, wqs_ref, kva_g_ref, wkt_ref, wv_ref,
                   gq_ref, gqs_ref, gkn_ref, gkr_ref, cos_ref, sin_ref,
                   ypool_ref, q_ref, k_ref, v_ref, ue_ref, s2_ref, s4_ref, s8_ref):
    i = pl.program_id(1)
    tm = h_ref.shape[1]
    halo = POOL_HALO

    x = h_ref[0]
    xn = (x * _rms_scale(x, D_MODEL) * attn_g_ref[...]).astype(jnp.bfloat16)
    z = jnp.dot(xn, w_in_ref[...], preferred_element_type=jnp.float32)

    u = z[:, :POOL_WIDTH]

    @pl.when(i == 0)
    def _():
        ue_ref[0:halo, :] = jnp.zeros((halo, POOL_WIDTH), jnp.float32)

    gd = POOL_GROUP_DIM
    rows = tm + halo
    ue_ref[halo:, :] = u
    r1, r2, r3, r4 = (k * SUBLANES for k in (1, 2, 3, 4))
    s2_ref[r1:, :] = ue_ref[r1:, :] + ue_ref[r1 - 1:rows - 1, :]
    s4_ref[r2:, :] = s2_ref[r2:, gd:] + s2_ref[r2 - 2:rows - 2, gd:]
    s8_ref[r3:, :] = s4_ref[r3:, gd:] + s4_ref[r3 - 4:rows - 4, gd:]
    s16 = s8_ref[r4:, gd:] + s8_ref[r4 - 8:rows - 8, gd:]
    window_sums = (s2_ref[halo:, 0:gd], s4_ref[halo:, 0:gd], s8_ref[halo:, 0:gd], s16)

    pos = i * tm + lax.broadcasted_iota(jnp.int32, (tm, 1), 0)
    for g, w in enumerate(POOL_WINDOWS):
        sl = slice(g * gd, (g + 1) * gd)
        cnt = jnp.minimum(pos + 1, w).astype(jnp.float32)
        p = (window_sums[g] / cnt - u[:, sl]).astype(jnp.bfloat16)
        yg = jnp.dot(p, w_pool_ref[g], preferred_element_type=jnp.float32)
        ypool_ref[0, :, sl] = (yg * pool_scale_ref[:, sl]).astype(jnp.bfloat16)
    ue_ref[0:halo, :] = ue_ref[tm:tm + halo, :]

    lane = lax.broadcasted_iota(jnp.int32, (1, LANES), 1)
    is_nope = lane < QK_NOPE_DIM
    rope_mid = QK_NOPE_DIM + QK_ROPE_DIM // 2
    cos_t = cos_ref[...]
    sin_t = sin_ref[...]
    scale = QK_HEAD_DIM ** -0.5 * LOG2_E

    c_q = z[:, POOL_WIDTH:POOL_WIDTH + Q_LORA_RANK]
    cqn = (c_q * _rms_scale(c_q, Q_LORA_RANK) * qa_g_ref[...]).astype(jnp.bfloat16)
    qf = jnp.dot(cqn, wq_ref[...], preferred_element_type=jnp.float32)
    qfs = jnp.dot(cqn, wqs_ref[...], preferred_element_type=jnp.float32)
    cos_g = cos_t * (gq_ref[...] * scale)
    sin_g = sin_t * (gqs_ref[...] * scale)

    for hd in range(MLA_HEADS):
        sl = slice(hd * LANES, (hd + 1) * LANES)
        qh = qf[:, sl]
        sq = qh * qh
        ss_n = jnp.sum(jnp.where(is_nope, sq, 0.0), axis=-1, keepdims=True)
        ss_r = jnp.sum(jnp.where(is_nope, 0.0, sq), axis=-1, keepdims=True)
        r_n = lax.rsqrt(ss_n * (1.0 / QK_NOPE_DIM) + RMS_EPS)
        r_r = lax.rsqrt(ss_r * (1.0 / QK_ROPE_DIM) + RMS_EPS)
        rot = qh * cos_g + qfs[:, sl] * sin_g
        q_ref[0, hd] = (rot * jnp.where(is_nope, r_n, r_r)).astype(jnp.bfloat16)

    kv0 = POOL_WIDTH + Q_LORA_RANK
    c_kv = z[:, kv0:kv0 + KV_LORA_RANK]
    ckvn = (c_kv * _rms_scale(c_kv, KV_LORA_RANK) * kva_g_ref[...]).astype(jnp.bfloat16)
    v_ref[0] = jnp.dot(ckvn, wv_ref[...],
                       preferred_element_type=jnp.float32).astype(jnp.bfloat16)

    kr = z[:, kv0 + KV_LORA_RANK:]
    krn = kr * _rms_scale(kr, QK_ROPE_DIM) * gkr_ref[...]
    kr_rot = _rope(krn, cos_t, jnp.where(lane < rope_mid, sin_t, 0.0),
                   jnp.where(lane < rope_mid, 0.0, sin_t))
    kr_rot_t = kr_rot.T[QK_NOPE_DIM:, :].astype(jnp.bfloat16)

    kf_t = lax.dot_general(wkt_ref[...], ckvn, (((1,), (1,)), ((), ())),
                           preferred_element_type=jnp.float32)
    for hd in range(MLA_HEADS):
        kh = kf_t[hd * QK_NOPE_DIM:(hd + 1) * QK_NOPE_DIM, :]
        r_k = lax.rsqrt(jnp.sum(kh * kh, axis=0, keepdims=True) * (1.0 / QK_NOPE_DIM) + RMS_EPS)
        k_ref[0, hd, 0, 0:QK_NOPE_DIM, :] = (kh * r_k * gkn_ref[...]).astype(jnp.bfloat16)
        k_ref[0, hd, 0, QK_NOPE_DIM:, :] = kr_rot_t


def _attn_kernel(q_ref, k_ref, v_ref, o_ref, s_scr, mx_scr, m_scr, acc_scr):
    i = pl.program_id(2)
    tq = q_ref.shape[2]
    tk = s_scr.shape[2]
    n_slabs = tk // LANES
    lane = lax.broadcasted_iota(jnp.int32, (1, LANES), 1)
    low_half = lane < V_HEAD_DIM

    def scores(hh, j):
        s = jnp.dot(q_ref[0, hh], k_ref[0, hh, j], preferred_element_type=jnp.float32)
        s_scr[hh] = s
        mx = s[:, 0:LANES]
        for c in range(1, n_slabs):
            mx = jnp.maximum(mx, s[:, c * LANES:(c + 1) * LANES])
        mx_scr[hh] = mx

    def softmax_pv(hh, j, masked):
        if masked:
            row = lax.broadcasted_iota(jnp.int32, (tq, tk), 0)
            col = lax.broadcasted_iota(jnp.int32, (tq, tk), 1)
            s = jnp.where(col <= row, s_scr[hh], NEG_BIG)
            slabs = [s[:, c * LANES:(c + 1) * LANES] for c in range(n_slabs)]
            mx = slabs[0]
            for c in range(1, n_slabs):
                mx = jnp.maximum(mx, slabs[c])
        else:
            slabs = [s_scr[hh, :, c * LANES:(c + 1) * LANES] for c in range(n_slabs)]
            mx = mx_scr[hh]
        m_old = m_scr[hh]
        m_new = jnp.maximum(m_old, jnp.max(mx, axis=-1, keepdims=True))
        m_scr[hh] = m_new
        alpha = jnp.exp2(m_old - m_new)
        p = jnp.concatenate([jnp.exp2(sl - m_new).astype(jnp.bfloat16) for sl in slabs], axis=1)
        start = pl.multiple_of(j * tk, tk)
        pair = hh // 2
        vj = v_ref[0, pl.ds(start, tk), pair * LANES:(pair + 1) * LANES]
        own = low_half if hh % 2 == 0 else jnp.logical_not(low_half)
        vj = jnp.where(own, vj, jnp.ones_like(vj))
        acc_scr[hh] = alpha * acc_scr[hh] + jnp.dot(p, vj, preferred_element_type=jnp.float32)

    m_scr[...] = jnp.full(m_scr.shape, NEG_BIG, jnp.float32)
    acc_scr[...] = jnp.zeros(acc_scr.shape, jnp.float32)
    for hh in range(HEADS_PER_STEP):
        scores(hh, 0)

    def body(j, carry):
        for hh in range(HEADS_PER_STEP):
            softmax_pv(hh, j, masked=False)
            scores(hh, j + 1)
        return carry

    lax.fori_loop(0, i, body, 0)
    for hh in range(HEADS_PER_STEP):
        softmax_pv(hh, i, masked=True)

    for pair in range(HEADS_PER_STEP // 2):
        acc_e = acc_scr[2 * pair]
        acc_o = acc_scr[2 * pair + 1]
        out_e = acc_e / acc_e[:, V_HEAD_DIM:V_HEAD_DIM + 1]
        out_o = acc_o / acc_o[:, 0:1]
        o_ref[0, :, pair * LANES:(pair + 1) * LANES] = jnp.where(
            low_half, out_e, out_o).astype(jnp.bfloat16)


def _out_ffn_kernel(h_ref, ypool_ref, ymla_ref, wo_pool_ref, wo_mla_ref, ffn_g_ref,
                    wg_ref, wu_ref, wd_ref, o_ref, acc_ref):
    h1 = (h_ref[...]
          + jnp.dot(ypool_ref[...], wo_pool_ref[...], preferred_element_type=jnp.float32)
          + jnp.dot(ymla_ref[...], wo_mla_ref[...], preferred_element_type=jnp.float32))
    g = (h1 * _rms_scale(h1, D_MODEL) * ffn_g_ref[...]).astype(jnp.bfloat16)
    acc_ref[...] = h1

    def gate_up(c):
        gate = jnp.dot(g, wg_ref[c], preferred_element_type=jnp.float32)
        up = jnp.dot(g, wu_ref[c], preferred_element_type=jnp.float32)
        return (gate * (1.0 / (1.0 + jnp.exp(-gate))) * up).astype(jnp.bfloat16)

    def down(act, c):
        acc_ref[...] += jnp.dot(act, wd_ref[c], preferred_element_type=jnp.float32)

    n_chunks = wg_ref.shape[0]

    def chunk_pair(c2, carry):
        c = 2 * c2
        act0 = gate_up(c)
        act1 = gate_up(c + 1)
        down(act0, c)
        down(act1, c + 1)
        return carry

    lax.fori_loop(0, n_chunks // 2, chunk_pair, 0)
    if n_chunks % 2:
        down(gate_up(n_chunks - 1), n_chunks - 1)
    o_ref[...] = acc_ref[...]


def _const_spec(shape):
    nd = len(shape)
    return pl.BlockSpec(shape, lambda *_: (0,) * nd)


def _resident_spec(shape):
    nd = len(shape)
    return pl.BlockSpec(shape, lambda *_: (0,) * nd, pipeline_mode=pl.Buffered(1))


def _mix_in(h, attn_g, w_in_p, w_pool, pool_scale, qa_g, wq_p, wqs_p, kva_g, wk_t, wv,
            gq, gqs, gkn_t, gkr, cos_t, sin_t):
    batch, lp, _ = h.shape
    tm = SEQ_TILE
    n_tiles = lp // tm
    row_spec = lambda width: pl.BlockSpec((1, tm, width), lambda b, i: (b, i, 0))
    q_spec = pl.BlockSpec((1, MLA_HEADS, tm, LANES), lambda b, i: (b, 0, i, 0))
    kt_spec = pl.BlockSpec((1, MLA_HEADS, 1, LANES, tm), lambda b, i: (b, 0, i, 0, 0))
    table_spec = pl.BlockSpec((tm, LANES), lambda b, i: (i, 0))
    consts = (attn_g, w_in_p, w_pool, pool_scale, qa_g, wq_p, wqs_p, kva_g, wk_t, wv,
              gq, gqs, gkn_t, gkr)
    return pl.pallas_call(
        _mix_in_kernel,
        grid=(batch, n_tiles),
        in_specs=[row_spec(D_MODEL)] + [_const_spec(c.shape) for c in consts]
                 + [table_spec, table_spec],
        out_specs=[row_spec(POOL_WIDTH), q_spec, kt_spec, row_spec(MLA_HEADS * V_HEAD_DIM)],
        out_shape=[
            jax.ShapeDtypeStruct((batch, lp, POOL_WIDTH), jnp.bfloat16),
            jax.ShapeDtypeStruct((batch, MLA_HEADS, lp, LANES), jnp.bfloat16),
            jax.ShapeDtypeStruct((batch, MLA_HEADS, n_tiles, LANES, tm), jnp.bfloat16),
            jax.ShapeDtypeStruct((batch, lp, MLA_HEADS * V_HEAD_DIM), jnp.bfloat16),
        ],
        scratch_shapes=[pltpu.VMEM((tm + POOL_HALO, POOL_WIDTH - drop * POOL_GROUP_DIM), jnp.float32)
                        for drop in (0, 0, 1, 2)],
        compiler_params=pltpu.CompilerParams(
            dimension_semantics=("arbitrary", "arbitrary"),
            vmem_limit_bytes=VMEM_LIMIT_BYTES),
        name="mix_in",
    )(h, *consts, cos_t, sin_t)


def _attention(q, k_t, v):
    batch, heads, lp, _ = q.shape
    tq = SEQ_TILE
    hps = HEADS_PER_STEP
    n_chunks = k_t.shape[2]
    assert k_t.shape[4] == tq
    return pl.pallas_call(
        _attn_kernel,
        grid=(batch, heads // hps, lp // tq),
        in_specs=[
            pl.BlockSpec((1, hps, tq, LANES), lambda b, hp, i: (b, hp, i, 0)),
            pl.BlockSpec((1, hps, n_chunks, LANES, tq), lambda b, hp, i: (b, hp, 0, 0, 0)),
            pl.BlockSpec((1, lp, hps * V_HEAD_DIM), lambda b, hp, i: (b, 0, hp)),
        ],
        out_specs=pl.BlockSpec((1, tq, hps * V_HEAD_DIM), lambda b, hp, i: (b, i, hp)),
        out_shape=jax.ShapeDtypeStruct((batch, lp, heads * V_HEAD_DIM), jnp.bfloat16),
        scratch_shapes=[
            pltpu.VMEM((hps, tq, tq), jnp.float32),
            pltpu.VMEM((hps, tq, LANES), jnp.float32),
            pltpu.VMEM((hps, tq, LANES), jnp.float32),
            pltpu.VMEM((hps, tq, LANES), jnp.float32),
        ],
        compiler_params=pltpu.CompilerParams(
            dimension_semantics=("arbitrary", "arbitrary", "arbitrary"),
            vmem_limit_bytes=VMEM_LIMIT_BYTES),
        name="mla_attn",
    )(q, k_t, v)


def _out_ffn(h2d, ypool2d, ymla2d, wo_pool, wo_mla, ffn_g, wg, wu, wd):
    rows, d = h2d.shape
    tm = FFN_ROW_TILE
    row_spec = lambda width: pl.BlockSpec((tm, width), lambda i: (i, 0))
    return pl.pallas_call(
        _out_ffn_kernel,
        grid=(rows // tm,),
        in_specs=[
            row_spec(d), row_spec(POOL_WIDTH), row_spec(MLA_HEADS * V_HEAD_DIM),
            _resident_spec(wo_pool.shape), _resident_spec(wo_mla.shape),
            _const_spec(ffn_g.shape),
            _resident_spec(wg.shape), _resident_spec(wu.shape), _resident_spec(wd.shape),
        ],
        out_specs=row_spec(d),
        out_shape=jax.ShapeDtypeStruct((rows, d), jnp.float32),
        scratch_shapes=[pltpu.VMEM((tm, d), jnp.float32)],
        compiler_params=pltpu.CompilerParams(
            dimension_semantics=("arbitrary",),
            vmem_limit_bytes=VMEM_LIMIT_BYTES),
        name="out_ffn",
    )(h2d, ypool2d, ymla2d, wo_pool, wo_mla, ffn_g, wg, wu, wd)


def _rope_tables(lp):
    inv = 1.0 / (ROPE_THETA ** (jnp.arange(0, QK_ROPE_DIM, 2, dtype=jnp.float32) / QK_ROPE_DIM))
    ang = jnp.arange(lp, dtype=jnp.float32)[:, None] * inv[None, :]
    cos, sin = jnp.cos(ang), jnp.sin(ang)
    zeros = lambda n: jnp.zeros((lp, n), jnp.float32)
    tail = LANES - QK_HEAD_DIM
    cos_t = jnp.concatenate([jnp.ones((lp, QK_NOPE_DIM), jnp.float32), cos, cos, zeros(tail)], axis=1)
    sin_t = jnp.concatenate([zeros(QK_NOPE_DIM), -sin, sin, zeros(tail)], axis=1)
    return cos_t, sin_t


def _swap_rope_halves(a):
    half = QK_ROPE_DIM // 2
    return jnp.concatenate([jnp.zeros_like(a[..., :QK_NOPE_DIM]),
                            a[..., QK_NOPE_DIM + half:], a[..., QK_NOPE_DIM:QK_NOPE_DIM + half]],
                           axis=-1)


def _pad_lanes(a, width):
    return jnp.pad(a, [(0, 0)] * (a.ndim - 1) + [(0, width - a.shape[-1])])


def kernel(x, meta_tokens, attn_norm_g, w_in, w_pool, pool_scale, q_a_norm_g, w_q_b,
           kv_a_norm_g, w_kv_b, q_norm_g, k_norm_g, w_out, ffn_norm_g, w_gate, w_up, w_down):
    batch, seq, d = x.shape
    depth = w_in.shape[0]
    length = N_META + seq
    lp = -(-length // SEQ_TILE) * SEQ_TILE
    assert (batch * lp) % FFN_ROW_TILE == 0 and D_FF % FF_CHUNK == 0
    bf16 = jnp.bfloat16
    n_chunks = D_FF // FF_CHUNK

    meta = jnp.broadcast_to(meta_tokens.astype(x.dtype)[None], (batch, N_META, d))
    h = jnp.concatenate([meta, x, jnp.zeros((batch, lp - length, d), x.dtype)], axis=1)
    cos_t, sin_t = _rope_tables(lp)

    s1 = POOL_WIDTH
    s3 = s1 + Q_LORA_RANK + KV_LORA_RANK
    for l in range(depth):
        rope_block = jnp.pad(w_in[l][:, s3:], ((0, 0), (QK_NOPE_DIM, LANES - QK_HEAD_DIM)))
        w_in_p = jnp.concatenate([w_in[l][:, :s3], rope_block], axis=1).astype(bf16)
        wq = w_q_b[l].reshape(Q_LORA_RANK, MLA_HEADS, QK_HEAD_DIM)
        wq_p = _pad_lanes(wq, LANES).reshape(Q_LORA_RANK, MLA_HEADS * LANES).astype(bf16)
        wqs_p = _pad_lanes(_swap_rope_halves(wq), LANES)
        wqs_p = wqs_p.reshape(Q_LORA_RANK, MLA_HEADS * LANES).astype(bf16)
        wkv = w_kv_b[l].reshape(KV_LORA_RANK, MLA_HEADS, QK_NOPE_DIM + V_HEAD_DIM)
        wk_t = wkv[:, :, :QK_NOPE_DIM].reshape(KV_LORA_RANK, MLA_HEADS * QK_NOPE_DIM).T.astype(bf16)
        wv = wkv[:, :, QK_NOPE_DIM:].reshape(KV_LORA_RANK, MLA_HEADS * V_HEAD_DIM).astype(bf16)
        gq = _pad_lanes(q_norm_g[l][None, :], LANES)
        gqs = _pad_lanes(_swap_rope_halves(q_norm_g[l][None, :]), LANES)
        gkn_t = jnp.broadcast_to(k_norm_g[l][:QK_NOPE_DIM, None], (QK_NOPE_DIM, SEQ_TILE))
        gkr = jnp.pad(k_norm_g[l][None, QK_NOPE_DIM:], ((0, 0), (QK_NOPE_DIM, LANES - QK_HEAD_DIM)))

        ypool, q, k_t, v = _mix_in(
            h, attn_norm_g[l][None, :], w_in_p, w_pool[l].astype(bf16), pool_scale[l][None, :],
            q_a_norm_g[l][None, :], wq_p, wqs_p, kv_a_norm_g[l][None, :], wk_t, wv,
            gq, gqs, gkn_t, gkr, cos_t, sin_t)
        ymla = _attention(q, k_t, v)

        wo = w_out[l].astype(bf16)
        wg = w_gate[l].reshape(d, n_chunks, FF_CHUNK).transpose(1, 0, 2).astype(bf16)
        wu = w_up[l].reshape(d, n_chunks, FF_CHUNK).transpose(1, 0, 2).astype(bf16)
        wd = w_down[l].reshape(n_chunks, FF_CHUNK, d).astype(bf16)
        h = _out_ffn(h.reshape(batch * lp, d), ypool.reshape(batch * lp, POOL_WIDTH),
                     ymla.reshape(batch * lp, MLA_HEADS * V_HEAD_DIM),
                     wo[:POOL_WIDTH], wo[POOL_WIDTH:], ffn_norm_g[l][None, :],
                     wg, wu, wd).reshape(batch, lp, d)
    return h[:, N_META:length]
```

```python
import functools

import jax
import jax.numpy as jnp
from jax import lax
from jax.experimental import pallas as pl
from jax.experimental.pallas import tpu as pltpu

D_MODEL = 1024
N_META = 16
POOL_WIDTH = 512
POOL_WINDOWS = (2, 4, 8, 16)
POOL_GROUP_DIM = 128
MLA_HEADS = 8
QK_NOPE_DIM = 64
QK_ROPE_DIM = 32
QK_HEAD_DIM = QK_NOPE_DIM + QK_ROPE_DIM
V_HEAD_DIM = 64
Q_LORA_RANK = 384
KV_LORA_RANK = 256
D_FF = 2816
ROPE_THETA = 10000.0
RMS_EPS = 1e-6

LANES = 128
SUBLANES = 8
assert POOL_WINDOWS == tuple(2 ** (k + 1) for k in range(len(POOL_WINDOWS)))
POOL_HALO = SUBLANES * len(POOL_WINDOWS)
assert POOL_HALO >= max(POOL_WINDOWS)

SEQ_TILE = 384
FFN_ROW_TILE = 512
FF_CHUNK = 256
ROW_BLOCK = 128
HEADS_PER_STEP = 8
IN_WIDTH_PADDED = 1280
VMEM_LIMIT_BYTES = 56 * 1024 * 1024

NEG_BIG = -1e30
LOG2_E = 1.4426950408889634


def _rms_scale(x, width):
    return lax.rsqrt(jnp.sum(x * x, axis=-1, keepdims=True) * (1.0 / width) + RMS_EPS)


def _rope(xn, cos_t, sin_lo, sin_hi):
    half = QK_ROPE_DIM // 2
    return (xn * cos_t
            + pltpu.roll(xn, LANES - half, 1) * sin_lo
            + pltpu.roll(xn, half, 1) * sin_hi)


def _mix_in_kernel(h_ref, *refs):
    _mix_in_body(h_ref[0], *refs)


def _mix_in_first_kernel(x_ref, meta_ref, *refs, seq, n_tiles):
    *refs, xt_ref = refs
    n_scratch = 4
    h_out_ref = refs[-n_scratch - 1]
    refs = refs[:-n_scratch - 1] + refs[-n_scratch:]
    i = pl.program_id(1)
    tm = xt_ref.shape[0]
    blk = x_ref[0]
    last_shift = (n_tiles - 1) * tm - N_META - (seq - tm)
    last_rows = tm - last_shift

    @pl.when(i == 0)
    def _():
        xt_ref[0:N_META, :] = meta_ref[...]
        xt_ref[N_META:, :] = blk[0:tm - N_META]

    @pl.when(i == n_tiles - 1)
    def _():
        xt_ref[0:last_rows, :] = blk[last_shift:]
        xt_ref[last_rows:, :] = jnp.zeros((tm - last_rows, blk.shape[1]), blk.dtype)

    @pl.when(jnp.logical_and(i > 0, i < n_tiles - 1))
    def _():
        xt_ref[...] = blk

    x = xt_ref[...]
    h_out_ref[0] = x
    _mix_in_body(x, *refs)


def _mix_in_body(x, attn_g_ref, w_in_ref, w_pool_ref, pool_scale_ref,
                 qa_g_ref, wq_ref, wqs_ref, kva_g_ref, wkt_ref, wv_ref,
                 gq_ref, gqs_ref, gkn_ref, gkr_ref, cos_ref, sin_ref,
                 ypool_ref, q_ref, k_ref, v_ref, ue_ref, s2_ref, s4_ref, s8_ref):
    i = pl.program_id(1)
    tm = x.shape[0]
    halo = POOL_HALO

    xn = (x * _rms_scale(x, D_MODEL) * attn_g_ref[...]).astype(jnp.bfloat16)
    z = jnp.dot(xn, w_in_ref[...], preferred_element_type=jnp.float32)

    u = z[:, :POOL_WIDTH]

    @pl.when(i == 0)
    def _():
        ue_ref[0:halo, :] = jnp.zeros((halo, POOL_WIDTH), jnp.float32)

    gd = POOL_GROUP_DIM
    rows = tm + halo
    ue_ref[halo:, :] = u
    r1, r2, r3, r4 = (k * SUBLANES for k in (1, 2, 3, 4))
    s2_ref[r1:, :] = ue_ref[r1:, :] + ue_ref[r1 - 1:rows - 1, :]
    s4_ref[r2:, :] = s2_ref[r2:, gd:] + s2_ref[r2 - 2:rows - 2, gd:]
    s8_ref[r3:, :] = s4_ref[r3:, gd:] + s4_ref[r3 - 4:rows - 4, gd:]
    s16 = s8_ref[r4:, gd:] + s8_ref[r4 - 8:rows - 8, gd:]
    window_sums = (s2_ref[halo:, 0:gd], s4_ref[halo:, 0:gd], s8_ref[halo:, 0:gd], s16)

    pos = i * tm + lax.broadcasted_iota(jnp.int32, (tm, 1), 0)
    for g, w in enumerate(POOL_WINDOWS):
        sl = slice(g * gd, (g + 1) * gd)
        cnt = jnp.minimum(pos + 1, w).astype(jnp.float32)
        p = (window_sums[g] / cnt - u[:, sl]).astype(jnp.bfloat16)
        yg = jnp.dot(p, w_pool_ref[g], preferred_element_type=jnp.float32)
        ypool_ref[0, :, sl] = (yg * pool_scale_ref[:, sl]).astype(jnp.bfloat16)
    ue_ref[0:halo, :] = ue_ref[tm:tm + halo, :]

    lane = lax.broadcasted_iota(jnp.int32, (1, LANES), 1)
    is_nope = lane < QK_NOPE_DIM
    rope_mid = QK_NOPE_DIM + QK_ROPE_DIM // 2
    cos_t = cos_ref[...]
    sin_t = sin_ref[...]
    scale = QK_HEAD_DIM ** -0.5 * LOG2_E

    c_q = z[:, POOL_WIDTH:POOL_WIDTH + Q_LORA_RANK]
    cqn = (c_q * _rms_scale(c_q, Q_LORA_RANK) * qa_g_ref[...]).astype(jnp.bfloat16)
    qf = jnp.dot(cqn, wq_ref[...], preferred_element_type=jnp.float32)
    qfs = jnp.dot(cqn, wqs_ref[...], preferred_element_type=jnp.float32)
    cos_g = cos_t * (gq_ref[...] * scale)
    sin_g = sin_t * (gqs_ref[...] * scale)

    for hd in range(MLA_HEADS):
        sl = slice(hd * LANES, (hd + 1) * LANES)
        qh = qf[:, sl]
        sq = qh * qh
        ss_n = jnp.sum(jnp.where(is_nope, sq, 0.0), axis=-1, keepdims=True)
        ss_r = jnp.sum(jnp.where(is_nope, 0.0, sq), axis=-1, keepdims=True)
        r_n = lax.rsqrt(ss_n * (1.0 / QK_NOPE_DIM) + RMS_EPS)
        r_r = lax.rsqrt(ss_r * (1.0 / QK_ROPE_DIM) + RMS_EPS)
        rot = qh * cos_g + qfs[:, sl] * sin_g
        q_ref[0, hd] = (rot * jnp.where(is_nope, r_n, r_r)).astype(jnp.bfloat16)

    kv0 = POOL_WIDTH + Q_LORA_RANK
    c_kv = z[:, kv0:kv0 + KV_LORA_RANK]
    ckvn = (c_kv * _rms_scale(c_kv, KV_LORA_RANK) * kva_g_ref[...]).astype(jnp.bfloat16)
    v_ref[0] = jnp.dot(ckvn, wv_ref[...],
                       preferred_element_type=jnp.float32).astype(jnp.bfloat16)

    kr = z[:, kv0 + KV_LORA_RANK:]
    krn = kr * _rms_scale(kr, QK_ROPE_DIM) * gkr_ref[...]
    kr_rot = _rope(krn, cos_t, jnp.where(lane < rope_mid, sin_t, 0.0),
                   jnp.where(lane < rope_mid, 0.0, sin_t))
    kr_rot_t = kr_rot.T[QK_NOPE_DIM:, :].astype(jnp.bfloat16)

    kf_t = lax.dot_general(wkt_ref[...], ckvn, (((1,), (1,)), ((), ())),
                           preferred_element_type=jnp.float32)
    for hd in range(MLA_HEADS):
        kh = kf_t[hd * QK_NOPE_DIM:(hd + 1) * QK_NOPE_DIM, :]
        r_k = lax.rsqrt(jnp.sum(kh * kh, axis=0, keepdims=True) * (1.0 / QK_NOPE_DIM) + RMS_EPS)
        k_ref[0, hd, 0, 0:QK_NOPE_DIM, :] = (kh * r_k * gkn_ref[...]).astype(jnp.bfloat16)
        k_ref[0, hd, 0, QK_NOPE_DIM:, :] = kr_rot_t


def _attn_kernel(q_ref, k_ref, v_ref, o_ref, s_scr, m_scr, acc_scr):
    i = pl.program_id(2)
    tq = q_ref.shape[2]
    tk = s_scr.shape[2]
    n_slabs = tk // LANES
    lane = lax.broadcasted_iota(jnp.int32, (1, LANES), 1)
    low_half = lane < V_HEAD_DIM

    def scores(hh, j):
        s_scr[hh] = jnp.dot(q_ref[0, hh], k_ref[0, hh, j], preferred_element_type=jnp.float32)

    def softmax_pv(hh, j, masked):
        if masked:
            row = lax.broadcasted_iota(jnp.int32, (tq, tk), 0)
            col = lax.broadcasted_iota(jnp.int32, (tq, tk), 1)
            s = jnp.where(col <= row, s_scr[hh], NEG_BIG)
            slabs = [s[:, c * LANES:(c + 1) * LANES] for c in range(n_slabs)]
        else:
            slabs = None
        rb = ROW_BLOCK
        alphas, ps = [], []
        for r0 in range(0, tq, rb):
            if masked:
                blk = [sl[r0:r0 + rb] for sl in slabs]
            else:
                blk = [s_scr[hh, r0:r0 + rb, c * LANES:(c + 1) * LANES] for c in range(n_slabs)]
            mx = blk[0]
            for c in range(1, n_slabs):
                mx = jnp.maximum(mx, blk[c])
            m_old = m_scr[hh, r0:r0 + rb]
            m_new = jnp.maximum(m_old, jnp.max(mx, axis=-1, keepdims=True))
            m_scr[hh, r0:r0 + rb] = m_new
            alphas.append(jnp.exp2(m_old - m_new))
            ps.append(jnp.concatenate(
                [jnp.exp2(sl - m_new).astype(jnp.bfloat16) for sl in blk], axis=1))
        alpha = jnp.concatenate(alphas, axis=0)
        p = jnp.concatenate(ps, axis=0)
        start = pl.multiple_of(j * tk, tk)
        pair = hh // 2
        vj = v_ref[0, pl.ds(start, tk), pair * LANES:(pair + 1) * LANES]
        own = low_half if hh % 2 == 0 else jnp.logical_not(low_half)
        vj = jnp.where(own, vj, jnp.ones_like(vj))
        acc_scr[hh] = alpha * acc_scr[hh] + jnp.dot(p, vj, preferred_element_type=jnp.float32)

    m_scr[...] = jnp.full(m_scr.shape, NEG_BIG, jnp.float32)
    acc_scr[...] = jnp.zeros(acc_scr.shape, jnp.float32)
    for hh in range(HEADS_PER_STEP):
        scores(hh, 0)

    def body(j, carry):
        for hh in range(HEADS_PER_STEP):
            softmax_pv(hh, j, masked=False)
            scores(hh, j + 1)
        return carry

    lax.fori_loop(0, i, body, 0)
    for hh in range(HEADS_PER_STEP):
        softmax_pv(hh, i, masked=True)

    for pair in range(HEADS_PER_STEP // 2):
        acc_e = acc_scr[2 * pair]
        acc_o = acc_scr[2 * pair + 1]
        out_e = acc_e / acc_e[:, V_HEAD_DIM:V_HEAD_DIM + 1]
        out_o = acc_o / acc_o[:, 0:1]
        o_ref[0, :, pair * LANES:(pair + 1) * LANES] = jnp.where(
            low_half, out_e, out_o).astype(jnp.bfloat16)


def _out_ffn_kernel(h_ref, ypool_ref, ymla_ref, wo_pool_ref, wo_mla_ref, ffn_g_ref,
                    wg_ref, wu_ref, wd_ref, o_ref, acc_ref):
    h1 = (h_ref[...]
          + jnp.dot(ypool_ref[...], wo_pool_ref[...], preferred_element_type=jnp.float32)
          + jnp.dot(ymla_ref[...], wo_mla_ref[...], preferred_element_type=jnp.float32))
    g = (h1 * _rms_scale(h1, D_MODEL) * ffn_g_ref[...]).astype(jnp.bfloat16)
    acc_ref[...] = h1

    def chunk_slice(c):
        return pl.ds(pl.multiple_of(c * FF_CHUNK, FF_CHUNK), FF_CHUNK)

    def gate_up(c):
        cols = chunk_slice(c)
        gate = jnp.dot(g, wg_ref[:, cols], preferred_element_type=jnp.float32)
        up = jnp.dot(g, wu_ref[:, cols], preferred_element_type=jnp.float32)
        return (gate * (1.0 / (1.0 + jnp.exp(-gate))) * up).astype(jnp.bfloat16)

    def down(act, c):
        acc_ref[...] += jnp.dot(act, wd_ref[chunk_slice(c), :], preferred_element_type=jnp.float32)

    n_chunks = wg_ref.shape[1] // FF_CHUNK

    def chunk_pair(c2, carry):
        c = 2 * c2
        act0 = gate_up(c)
        act1 = gate_up(c + 1)
        down(act0, c)
        down(act1, c + 1)
        return carry

    lax.fori_loop(0, n_chunks // 2, chunk_pair, 0)
    if n_chunks % 2:
        down(gate_up(n_chunks - 1), n_chunks - 1)
    o_ref[...] = acc_ref[...]


def _const_spec(shape):
    nd = len(shape)
    return pl.BlockSpec(shape, lambda *_: (0,) * nd)


def _resident_spec(shape):
    nd = len(shape)
    return pl.BlockSpec(shape, lambda *_: (0,) * nd, pipeline_mode=pl.Buffered(1))


def _mix_in(h, consts, cos_t, sin_t, lp, tokens_meta=None):
    first = tokens_meta is not None
    batch = tokens_meta[0].shape[0] if first else h.shape[0]
    d = D_MODEL
    tm = SEQ_TILE
    n_tiles = lp // tm
    row_spec = lambda width: pl.BlockSpec((1, tm, width), lambda b, i: (b, i, 0))
    q_spec = pl.BlockSpec((1, MLA_HEADS, tm, LANES), lambda b, i: (b, 0, i, 0))
    kt_spec = pl.BlockSpec((1, MLA_HEADS, 1, LANES, tm), lambda b, i: (b, 0, i, 0, 0))
    table_spec = pl.BlockSpec((tm, LANES), lambda b, i: (i, 0))
    out_specs = [row_spec(POOL_WIDTH), q_spec, kt_spec, row_spec(MLA_HEADS * V_HEAD_DIM)]
    out_shape = [
        jax.ShapeDtypeStruct((batch, lp, POOL_WIDTH), jnp.bfloat16),
        jax.ShapeDtypeStruct((batch, MLA_HEADS, lp, LANES), jnp.bfloat16),
        jax.ShapeDtypeStruct((batch, MLA_HEADS, n_tiles, LANES, tm), jnp.bfloat16),
        jax.ShapeDtypeStruct((batch, lp, MLA_HEADS * V_HEAD_DIM), jnp.bfloat16),
    ]
    scratch = [pltpu.VMEM((tm + POOL_HALO, POOL_WIDTH - drop * POOL_GROUP_DIM), jnp.float32)
               for drop in (0, 0, 1, 2)]
    if first:
        x, meta = tokens_meta
        seq = x.shape[1]
        body = functools.partial(_mix_in_first_kernel, seq=seq, n_tiles=n_tiles)
        x_spec = pl.BlockSpec(
            (pl.Element(1), pl.Element(tm), pl.Element(d)),
            lambda b, i: (b, pl.multiple_of(jnp.clip(i * tm - N_META, 0, seq - tm), SUBLANES), 0))
        lead_specs = [x_spec, _const_spec(meta.shape)]
        lead_args = [x, meta]
        out_specs.append(row_spec(d))
        out_shape.append(jax.ShapeDtypeStruct((batch, lp, d), jnp.float32))
        scratch.append(pltpu.VMEM((tm, d), jnp.float32))
    else:
        body = _mix_in_kernel
        lead_specs = [row_spec(d)]
        lead_args = [h]
    return pl.pallas_call(
        body,
        grid=(batch, n_tiles),
        in_specs=lead_specs + [_const_spec(c.shape) for c in consts] + [table_spec, table_spec],
        out_specs=out_specs,
        out_shape=out_shape,
        scratch_shapes=scratch,
        compiler_params=pltpu.CompilerParams(
            dimension_semantics=("arbitrary", "arbitrary"),
            vmem_limit_bytes=VMEM_LIMIT_BYTES),
        name="mix_in",
    )(*lead_args, *consts, cos_t, sin_t)


def _attention(q, k_t, v):
    batch, heads, lp, _ = q.shape
    tq = SEQ_TILE
    hps = HEADS_PER_STEP
    n_chunks = k_t.shape[2]
    assert k_t.shape[4] == tq
    return pl.pallas_call(
        _attn_kernel,
        grid=(batch, heads // hps, lp // tq),
        in_specs=[
            pl.BlockSpec((1, hps, tq, LANES), lambda b, hp, i: (b, hp, i, 0)),
            pl.BlockSpec((1, hps, n_chunks, LANES, tq), lambda b, hp, i: (b, hp, 0, 0, 0)),
            pl.BlockSpec((1, lp, hps * V_HEAD_DIM), lambda b, hp, i: (b, 0, hp)),
        ],
        out_specs=pl.BlockSpec((1, tq, hps * V_HEAD_DIM), lambda b, hp, i: (b, i, hp)),
        out_shape=jax.ShapeDtypeStruct((batch, lp, heads * V_HEAD_DIM), jnp.bfloat16),
        scratch_shapes=[
            pltpu.VMEM((hps, tq, tq), jnp.float32),
            pltpu.VMEM((hps, tq, LANES), jnp.float32),
            pltpu.VMEM((hps, tq, LANES), jnp.float32),
        ],
        compiler_params=pltpu.CompilerParams(
            dimension_semantics=("arbitrary", "arbitrary", "arbitrary"),
            vmem_limit_bytes=VMEM_LIMIT_BYTES),
        name="mla_attn",
    )(q, k_t, v)


def _out_ffn(h2d, ypool2d, ymla2d, wo_pool, wo_mla, ffn_g, wg, wu, wd, real_rows=None):
    rows, d = h2d.shape
    tm = FFN_ROW_TILE
    if real_rows is None:
        grid = (rows // tm,)
        in_row_spec = lambda width: pl.BlockSpec((tm, width), lambda i: (i, 0))
        out_spec = in_row_spec(d)
        out_rows = rows
    else:
        batch, lp, first, count = real_rows
        tiles = count // tm
        assert count % tm == 0 and first % 16 == 0
        grid = (batch, tiles)
        in_row_spec = lambda width: pl.BlockSpec(
            (pl.Element(tm), pl.Element(width)),
            lambda b, j: (pl.multiple_of(b * lp + first + j * tm, 16), 0))
        out_spec = pl.BlockSpec((tm, d), lambda b, j: (b * tiles + j, 0))
        out_rows = batch * count
    return pl.pallas_call(
        _out_ffn_kernel,
        grid=grid,
        in_specs=[
            in_row_spec(d), in_row_spec(POOL_WIDTH), in_row_spec(MLA_HEADS * V_HEAD_DIM),
            _resident_spec(wo_pool.shape), _resident_spec(wo_mla.shape),
            _const_spec(ffn_g.shape),
            _resident_spec(wg.shape), _resident_spec(wu.shape), _resident_spec(wd.shape),
        ],
        out_specs=out_spec,
        out_shape=jax.ShapeDtypeStruct((out_rows, d), jnp.float32),
        scratch_shapes=[pltpu.VMEM((tm, d), jnp.float32)],
        compiler_params=pltpu.CompilerParams(
            dimension_semantics=("arbitrary",) * len(grid),
            vmem_limit_bytes=VMEM_LIMIT_BYTES),
        name="out_ffn",
    )(h2d, ypool2d, ymla2d, wo_pool, wo_mla, ffn_g, wg, wu, wd)


def _rope_tables(lp):
    inv = 1.0 / (ROPE_THETA ** (jnp.arange(0, QK_ROPE_DIM, 2, dtype=jnp.float32) / QK_ROPE_DIM))
    ang = jnp.arange(lp, dtype=jnp.float32)[:, None] * inv[None, :]
    cos, sin = jnp.cos(ang), jnp.sin(ang)
    zeros = lambda n: jnp.zeros((lp, n), jnp.float32)
    tail = LANES - QK_HEAD_DIM
    cos_t = jnp.concatenate([jnp.ones((lp, QK_NOPE_DIM), jnp.float32), cos, cos, zeros(tail)], axis=1)
    sin_t = jnp.concatenate([zeros(QK_NOPE_DIM), -sin, sin, zeros(tail)], axis=1)
    return cos_t, sin_t


def _swap_rope_halves(a):
    half = QK_ROPE_DIM // 2
    return jnp.concatenate([jnp.zeros_like(a[..., :QK_NOPE_DIM]),
                            a[..., QK_NOPE_DIM + half:], a[..., QK_NOPE_DIM:QK_NOPE_DIM + half]],
                           axis=-1)


def _pad_lanes(a, width):
    return jnp.pad(a, [(0, 0)] * (a.ndim - 1) + [(0, width - a.shape[-1])])


def kernel(x, meta_tokens, attn_norm_g, w_in, w_pool, pool_scale, q_a_norm_g, w_q_b,
           kv_a_norm_g, w_kv_b, q_norm_g, k_norm_g, w_out, ffn_norm_g, w_gate, w_up, w_down):
    batch, seq, d = x.shape
    depth = w_in.shape[0]
    length = N_META + seq
    lp = -(-length // SEQ_TILE) * SEQ_TILE
    assert (batch * lp) % FFN_ROW_TILE == 0 and D_FF % FF_CHUNK == 0
    bf16 = jnp.bfloat16

    cos_t, sin_t = _rope_tables(lp)
    h = None

    s1 = POOL_WIDTH
    s3 = s1 + Q_LORA_RANK + KV_LORA_RANK
    for l in range(depth):
        rope_block = jnp.pad(w_in[l][:, s3:], ((0, 0), (QK_NOPE_DIM, LANES - QK_HEAD_DIM)))
        w_in_p = jnp.concatenate([w_in[l][:, :s3], rope_block], axis=1).astype(bf16)
        wq = w_q_b[l].reshape(Q_LORA_RANK, MLA_HEADS, QK_HEAD_DIM)
        wq_p = _pad_lanes(wq, LANES).reshape(Q_LORA_RANK, MLA_HEADS * LANES).astype(bf16)
        wqs_p = _pad_lanes(_swap_rope_halves(wq), LANES)
        wqs_p = wqs_p.reshape(Q_LORA_RANK, MLA_HEADS * LANES).astype(bf16)
        wkv = w_kv_b[l].reshape(KV_LORA_RANK, MLA_HEADS, QK_NOPE_DIM + V_HEAD_DIM)
        wk_t = wkv[:, :, :QK_NOPE_DIM].reshape(KV_LORA_RANK, MLA_HEADS * QK_NOPE_DIM).T.astype(bf16)
        wv = wkv[:, :, QK_NOPE_DIM:].reshape(KV_LORA_RANK, MLA_HEADS * V_HEAD_DIM).astype(bf16)
        gq = _pad_lanes(q_norm_g[l][None, :], LANES)
        gqs = _pad_lanes(_swap_rope_halves(q_norm_g[l][None, :]), LANES)
        gkn_t = jnp.broadcast_to(k_norm_g[l][:QK_NOPE_DIM, None], (QK_NOPE_DIM, SEQ_TILE))
        gkr = jnp.pad(k_norm_g[l][None, QK_NOPE_DIM:], ((0, 0), (QK_NOPE_DIM, LANES - QK_HEAD_DIM)))

        consts = (attn_norm_g[l][None, :], w_in_p, w_pool[l].astype(bf16), pool_scale[l][None, :],
                  q_a_norm_g[l][None, :], wq_p, wqs_p, kv_a_norm_g[l][None, :], wk_t, wv,
                  gq, gqs, gkn_t, gkr)
        if l == 0:
            ypool, q, k_t, v, h = _mix_in(None, consts, cos_t, sin_t, lp,
                                          tokens_meta=(x, meta_tokens.astype(x.dtype)))
        else:
            ypool, q, k_t, v = _mix_in(h, consts, cos_t, sin_t, lp)
        ymla = _attention(q, k_t, v)

        wo = w_out[l].astype(bf16)
        wg = w_gate[l].astype(bf16)
        wu = w_up[l].astype(bf16)
        wd = w_down[l].astype(bf16)
        last = l == depth - 1
        h = _out_ffn(h.reshape(batch * lp, d), ypool.reshape(batch * lp, POOL_WIDTH),
                     ymla.reshape(batch * lp, MLA_HEADS * V_HEAD_DIM),
                     wo[:POOL_WIDTH], wo[POOL_WIDTH:], ffn_norm_g[l][None, :],
                     wg, wu, wd, real_rows=(batch, lp, N_META, seq) if last else None)
        h = h.reshape(batch, seq if last else lp, d)
    return h
```

```python
import functools

import jax
import jax.numpy as jnp
from jax import lax
from jax.experimental import pallas as pl
from jax.experimental.pallas import tpu as pltpu

D_MODEL = 1024
N_META = 16
POOL_WIDTH = 512
POOL_WINDOWS = (2, 4, 8, 16)
POOL_GROUP_DIM = 128
MLA_HEADS = 8
QK_NOPE_DIM = 64
QK_ROPE_DIM = 32
QK_HEAD_DIM = QK_NOPE_DIM + QK_ROPE_DIM
V_HEAD_DIM = 64
Q_LORA_RANK = 384
KV_LORA_RANK = 256
D_FF = 2816
ROPE_THETA = 10000.0
RMS_EPS = 1e-6

LANES = 128
SUBLANES = 8
assert POOL_WINDOWS == tuple(2 ** (k + 1) for k in range(len(POOL_WINDOWS)))
POOL_HALO = SUBLANES * len(POOL_WINDOWS)
assert POOL_HALO >= max(POOL_WINDOWS)

SEQ_TILE = 384
FFN_ROW_TILE = 512
FF_CHUNK = 256
FF_UNROLL = 5
HEADS_PER_STEP = 8
SCORE_LOOKAHEAD = 1
TAIL_LOOKAHEAD = 2
SCORE_BUFFERS = 4
assert SCORE_LOOKAHEAD <= TAIL_LOOKAHEAD < SCORE_BUFFERS and HEADS_PER_STEP % SCORE_BUFFERS == 0
IN_WIDTH_PADDED = 1280
VMEM_LIMIT_BYTES = 56 * 1024 * 1024

NEG_BIG = -1e30
LOG2_E = 1.4426950408889634


def _rms_scale(x, width):
    return lax.rsqrt(jnp.sum(x * x, axis=-1, keepdims=True) * (1.0 / width) + RMS_EPS)


def _rope(xn, cos_t, sin_lo, sin_hi):
    half = QK_ROPE_DIM // 2
    return (xn * cos_t
            + pltpu.roll(xn, LANES - half, 1) * sin_lo
            + pltpu.roll(xn, half, 1) * sin_hi)


def _mix_in_kernel(h_ref, *refs):
    _mix_in_body(h_ref[0], *refs)


def _mix_in_first_kernel(x_ref, meta_ref, *refs, seq, n_tiles):
    *refs, xt_ref = refs
    n_scratch = 4
    h_out_ref = refs[-n_scratch - 1]
    refs = refs[:-n_scratch - 1] + refs[-n_scratch:]
    i = pl.program_id(1)
    tm = xt_ref.shape[0]
    blk = x_ref[0]
    last_shift = (n_tiles - 1) * tm - N_META - (seq - tm)
    last_rows = tm - last_shift

    @pl.when(i == 0)
    def _():
        xt_ref[0:N_META, :] = meta_ref[...]
        xt_ref[N_META:, :] = blk[0:tm - N_META]

    @pl.when(i == n_tiles - 1)
    def _():
        xt_ref[0:last_rows, :] = blk[last_shift:]
        xt_ref[last_rows:, :] = jnp.zeros((tm - last_rows, blk.shape[1]), blk.dtype)

    @pl.when(jnp.logical_and(i > 0, i < n_tiles - 1))
    def _():
        xt_ref[...] = blk

    x = xt_ref[...]
    h_out_ref[0] = x
    _mix_in_body(x, *refs)


def _mix_in_body(x, attn_g_ref, w_in_ref, w_pool_ref, pool_scale_ref,
                 qa_g_ref, wq_ref, wqs_ref, kva_g_ref, wkt_ref, wv_ref,
                 gq_ref, gqs_ref, gkn_ref, gkr_ref, cos_ref, sin_ref,
                 ypool_ref, q_ref, k_ref, v_ref, ue_ref, s2_ref, s4_ref, s8_ref):
    i = pl.program_id(1)
    tm = x.shape[0]
    halo = POOL_HALO

    xn = (x * _rms_scale(x, D_MODEL) * attn_g_ref[...]).astype(jnp.bfloat16)
    z = jnp.dot(xn, w_in_ref[...], preferred_element_type=jnp.float32)

    u = z[:, :POOL_WIDTH]

    @pl.when(i == 0)
    def _():
        ue_ref[0:halo, :] = jnp.zeros((halo, POOL_WIDTH), jnp.float32)

    gd = POOL_GROUP_DIM
    rows = tm + halo
    ue_ref[halo:, :] = u
    r1, r2, r3, r4 = (k * SUBLANES for k in (1, 2, 3, 4))
    s2_ref[r1:, :] = ue_ref[r1:, :] + ue_ref[r1 - 1:rows - 1, :]
    s4_ref[r2:, :] = s2_ref[r2:, gd:] + s2_ref[r2 - 2:rows - 2, gd:]
    s8_ref[r3:, :] = s4_ref[r3:, gd:] + s4_ref[r3 - 4:rows - 4, gd:]
    s16 = s8_ref[r4:, gd:] + s8_ref[r4 - 8:rows - 8, gd:]
    window_sums = (s2_ref[halo:, 0:gd], s4_ref[halo:, 0:gd], s8_ref[halo:, 0:gd], s16)

    pos = i * tm + lax.broadcasted_iota(jnp.int32, (tm, 1), 0)
    for g, w in enumerate(POOL_WINDOWS):
        sl = slice(g * gd, (g + 1) * gd)
        cnt = jnp.minimum(pos + 1, w).astype(jnp.float32)
        p = (window_sums[g] / cnt - u[:, sl]).astype(jnp.bfloat16)
        yg = jnp.dot(p, w_pool_ref[g], preferred_element_type=jnp.float32)
        ypool_ref[0, :, sl] = (yg * pool_scale_ref[:, sl]).astype(jnp.bfloat16)
    ue_ref[0:halo, :] = ue_ref[tm:tm + halo, :]

    lane = lax.broadcasted_iota(jnp.int32, (1, LANES), 1)
    is_nope = lane < QK_NOPE_DIM
    rope_mid = QK_NOPE_DIM + QK_ROPE_DIM // 2
    cos_t = cos_ref[...]
    sin_t = sin_ref[...]
    scale = QK_HEAD_DIM ** -0.5 * LOG2_E

    c_q = z[:, POOL_WIDTH:POOL_WIDTH + Q_LORA_RANK]
    cqn = (c_q * _rms_scale(c_q, Q_LORA_RANK) * qa_g_ref[...]).astype(jnp.bfloat16)
    qf = jnp.dot(cqn, wq_ref[...], preferred_element_type=jnp.float32)
    qfs = jnp.dot(cqn, wqs_ref[...], preferred_element_type=jnp.float32)
    cos_g = cos_t * (gq_ref[...] * scale)
    sin_g = sin_t * (gqs_ref[...] * scale)

    for hd in range(MLA_HEADS):
        sl = slice(hd * LANES, (hd + 1) * LANES)
        qh = qf[:, sl]
        sq = qh * qh
        ss_n = jnp.sum(jnp.where(is_nope, sq, 0.0), axis=-1, keepdims=True)
        ss_r = jnp.sum(jnp.where(is_nope, 0.0, sq), axis=-1, keepdims=True)
        r_n = lax.rsqrt(ss_n * (1.0 / QK_NOPE_DIM) + RMS_EPS)
        r_r = lax.rsqrt(ss_r * (1.0 / QK_ROPE_DIM) + RMS_EPS)
        rot = qh * cos_g + qfs[:, sl] * sin_g
        q_ref[0, hd] = (rot * jnp.where(is_nope, r_n, r_r)).astype(jnp.bfloat16)

    kv0 = POOL_WIDTH + Q_LORA_RANK
    c_kv = z[:, kv0:kv0 + KV_LORA_RANK]
    ckvn = (c_kv * _rms_scale(c_kv, KV_LORA_RANK) * kva_g_ref[...]).astype(jnp.bfloat16)
    v_ref[0] = jnp.dot(ckvn, wv_ref[...],
                       preferred_element_type=jnp.float32).astype(jnp.bfloat16)

    kr = z[:, kv0 + KV_LORA_RANK:]
    krn = kr * _rms_scale(kr, QK_ROPE_DIM) * gkr_ref[...]
    kr_rot = _rope(krn, cos_t, jnp.where(lane < rope_mid, sin_t, 0.0),
                   jnp.where(lane < rope_mid, 0.0, sin_t))
    kr_rot_t = kr_rot.T[QK_NOPE_DIM:, :].astype(jnp.bfloat16)

    kf_t = lax.dot_general(wkt_ref[...], ckvn, (((1,), (1,)), ((), ())),
                           preferred_element_type=jnp.float32)
    for hd in range(MLA_HEADS):
        kh = kf_t[hd * QK_NOPE_DIM:(hd + 1) * QK_NOPE_DIM, :]
        r_k = lax.rsqrt(jnp.sum(kh * kh, axis=0, keepdims=True) * (1.0 / QK_NOPE_DIM) + RMS_EPS)
        k_ref[0, hd, 0, 0:QK_NOPE_DIM, :] = (kh * r_k * gkn_ref[...]).astype(jnp.bfloat16)
        k_ref[0, hd, 0, QK_NOPE_DIM:, :] = kr_rot_t


def _attn_kernel(q_ref, k_ref, v_ref, o_ref, s_scr, m_scr, acc_scr):
    i = pl.program_id(2)
    tq = q_ref.shape[2]
    tk = s_scr.shape[2]
    n_slabs = tk // LANES
    lane = lax.broadcasted_iota(jnp.int32, (1, LANES), 1)
    low_half = lane < V_HEAD_DIM

    n_buf = s_scr.shape[0]
    heads = HEADS_PER_STEP

    def scores(hh, j):
        s_scr[hh % n_buf] = jnp.dot(q_ref[0, hh], k_ref[0, hh, j],
                                    preferred_element_type=jnp.float32)

    def softmax_pv(hh, j, masked):
        rb = LANES
        if masked:
            tri = (lax.broadcasted_iota(jnp.int32, (rb, LANES), 1)
                   <= lax.broadcasted_iota(jnp.int32, (rb, LANES), 0))
        alphas, ps = [], []
        for rblk in range(tq // rb):
            r0 = rblk * rb
            blk = []
            for c in range(n_slabs if not masked else rblk + 1):
                sl = s_scr[hh % n_buf, r0:r0 + rb, c * LANES:(c + 1) * LANES]
                if masked and c == rblk:
                    sl = jnp.where(tri, sl, NEG_BIG)
                blk.append(sl)
            mx = blk[0]
            for sl in blk[1:]:
                mx = jnp.maximum(mx, sl)
            m_old = m_scr[hh, r0:r0 + rb]
            m_new = jnp.maximum(m_old, jnp.max(mx, axis=-1, keepdims=True))
            m_scr[hh, r0:r0 + rb] = m_new
            alphas.append(jnp.exp2(m_old - m_new))
            p_blk = [jnp.exp2(sl - m_new).astype(jnp.bfloat16) for sl in blk]
            p_blk += [jnp.zeros((rb, LANES), jnp.bfloat16)] * (n_slabs - len(blk))
            ps.append(jnp.concatenate(p_blk, axis=1))
        alpha = jnp.concatenate(alphas, axis=0)
        p = jnp.concatenate(ps, axis=0)
        start = pl.multiple_of(j * tk, tk)
        pair = hh // 2
        vj = v_ref[0, pl.ds(start, tk), pair * LANES:(pair + 1) * LANES]
        own = low_half if hh % 2 == 0 else jnp.logical_not(low_half)
        vj = jnp.where(own, vj, jnp.ones_like(vj))
        acc_scr[hh] = alpha * acc_scr[hh] + jnp.dot(p, vj, preferred_element_type=jnp.float32)

    m_scr[...] = jnp.full(m_scr.shape, NEG_BIG, jnp.float32)
    acc_scr[...] = jnp.zeros(acc_scr.shape, jnp.float32)
    ahead = SCORE_LOOKAHEAD
    for hh in range(ahead):
        scores(hh, 0)

    def body(j, carry):
        for hh in range(heads):
            scores((hh + ahead) % heads, j + (hh + ahead) // heads)
            softmax_pv(hh, j, masked=False)
        return carry

    lax.fori_loop(0, i, body, 0)
    tail_ahead = TAIL_LOOKAHEAD
    for hh in range(ahead, tail_ahead):
        scores(hh, i)
    for hh in range(heads):
        if hh + tail_ahead < heads:
            scores(hh + tail_ahead, i)
        softmax_pv(hh, i, masked=True)

    for pair in range(HEADS_PER_STEP // 2):
        acc_e = acc_scr[2 * pair]
        acc_o = acc_scr[2 * pair + 1]
        out_e = acc_e / acc_e[:, V_HEAD_DIM:V_HEAD_DIM + 1]
        out_o = acc_o / acc_o[:, 0:1]
        o_ref[0, :, pair * LANES:(pair + 1) * LANES] = jnp.where(
            low_half, out_e, out_o).astype(jnp.bfloat16)


def _out_ffn_kernel(h_ref, ypool_ref, ymla_ref, wo_pool_ref, wo_mla_ref, ffn_g_ref,
                    wg_ref, wu_ref, wd_ref, o_ref, acc_ref):
    h1 = (h_ref[...]
          + jnp.dot(ypool_ref[...], wo_pool_ref[...], preferred_element_type=jnp.float32)
          + jnp.dot(ymla_ref[...], wo_mla_ref[...], preferred_element_type=jnp.float32))
    g = (h1 * _rms_scale(h1, D_MODEL) * ffn_g_ref[...]).astype(jnp.bfloat16)
    acc_ref[...] = h1

    def chunk_slice(c):
        return pl.ds(pl.multiple_of(c * FF_CHUNK, FF_CHUNK), FF_CHUNK)

    def gate_up(c):
        cols = chunk_slice(c)
        gate = jnp.dot(g, wg_ref[:, cols], preferred_element_type=jnp.float32)
        up = jnp.dot(g, wu_ref[:, cols], preferred_element_type=jnp.float32)
        return (gate * (1.0 / (1.0 + jnp.exp(-gate))) * up).astype(jnp.bfloat16)

    def down(act, c):
        acc_ref[...] += jnp.dot(act, wd_ref[chunk_slice(c), :], preferred_element_type=jnp.float32)

    n_chunks = wg_ref.shape[1] // FF_CHUNK

    def chunk_group(first, count):
        act = gate_up(first)
        for k in range(count):
            nxt = gate_up(first + k + 1) if k + 1 < count else None
            down(act, first + k)
            act = nxt

    def loop_step(step, carry):
        chunk_group(step * FF_UNROLL, FF_UNROLL)
        return carry

    n_steps = n_chunks // FF_UNROLL
    lax.fori_loop(0, n_steps, loop_step, 0)
    if n_chunks % FF_UNROLL:
        chunk_group(n_steps * FF_UNROLL, n_chunks % FF_UNROLL)
    o_ref[...] = acc_ref[...]


def _const_spec(shape):
    nd = len(shape)
    return pl.BlockSpec(shape, lambda *_: (0,) * nd)


def _resident_spec(shape):
    nd = len(shape)
    return pl.BlockSpec(shape, lambda *_: (0,) * nd, pipeline_mode=pl.Buffered(1))


def _mix_in(h, consts, cos_t, sin_t, lp, tokens_meta=None):
    first = tokens_meta is not None
    batch = tokens_meta[0].shape[0] if first else h.shape[0]
    d = D_MODEL
    tm = SEQ_TILE
    n_tiles = lp // tm
    row_spec = lambda width: pl.BlockSpec((1, tm, width), lambda b, i: (b, i, 0))
    q_spec = pl.BlockSpec((1, MLA_HEADS, tm, LANES), lambda b, i: (b, 0, i, 0))
    kt_spec = pl.BlockSpec((1, MLA_HEADS, 1, LANES, tm), lambda b, i: (b, 0, i, 0, 0))
    table_spec = pl.BlockSpec((tm, LANES), lambda b, i: (i, 0))
    out_specs = [row_spec(POOL_WIDTH), q_spec, kt_spec, row_spec(MLA_HEADS * V_HEAD_DIM)]
    out_shape = [
        jax.ShapeDtypeStruct((batch, lp, POOL_WIDTH), jnp.bfloat16),
        jax.ShapeDtypeStruct((batch, MLA_HEADS, lp, LANES), jnp.bfloat16),
        jax.ShapeDtypeStruct((batch, MLA_HEADS, n_tiles, LANES, tm), jnp.bfloat16),
        jax.ShapeDtypeStruct((batch, lp, MLA_HEADS * V_HEAD_DIM), jnp.bfloat16),
    ]
    scratch = [pltpu.VMEM((tm + POOL_HALO, POOL_WIDTH - drop * POOL_GROUP_DIM), jnp.float32)
               for drop in (0, 0, 1, 2)]
    if first:
        x, meta = tokens_meta
        seq = x.shape[1]
        body = functools.partial(_mix_in_first_kernel, seq=seq, n_tiles=n_tiles)
        x_spec = pl.BlockSpec(
            (pl.Element(1), pl.Element(tm), pl.Element(d)),
            lambda b, i: (b, pl.multiple_of(jnp.clip(i * tm - N_META, 0, seq - tm), SUBLANES), 0))
        lead_specs = [x_spec, _const_spec(meta.shape)]
        lead_args = [x, meta]
        out_specs.append(row_spec(d))
        out_shape.append(jax.ShapeDtypeStruct((batch, lp, d), jnp.float32))
        scratch.append(pltpu.VMEM((tm, d), jnp.float32))
    else:
        body = _mix_in_kernel
        lead_specs = [row_spec(d)]
        lead_args = [h]
    return pl.pallas_call(
        body,
        grid=(batch, n_tiles),
        in_specs=lead_specs + [_const_spec(c.shape) for c in consts] + [table_spec, table_spec],
        out_specs=out_specs,
        out_shape=out_shape,
        scratch_shapes=scratch,
        compiler_params=pltpu.CompilerParams(
            dimension_semantics=("arbitrary", "arbitrary"),
            vmem_limit_bytes=VMEM_LIMIT_BYTES),
        name="mix_in",
    )(*lead_args, *consts, cos_t, sin_t)


def _attention(q, k_t, v):
    batch, heads, lp, _ = q.shape
    tq = SEQ_TILE
    hps = HEADS_PER_STEP
    n_chunks = k_t.shape[2]
    assert k_t.shape[4] == tq
    return pl.pallas_call(
        _attn_kernel,
        grid=(batch, heads // hps, lp // tq),
        in_specs=[
            pl.BlockSpec((1, hps, tq, LANES), lambda b, hp, i: (b, hp, i, 0)),
            pl.BlockSpec((1, hps, n_chunks, LANES, tq), lambda b, hp, i: (b, hp, 0, 0, 0)),
            pl.BlockSpec((1, lp, hps * V_HEAD_DIM), lambda b, hp, i: (b, 0, hp)),
        ],
        out_specs=pl.BlockSpec((1, tq, hps * V_HEAD_DIM), lambda b, hp, i: (b, i, hp)),
        out_shape=jax.ShapeDtypeStruct((batch, lp, heads * V_HEAD_DIM), jnp.bfloat16),
        scratch_shapes=[
            pltpu.VMEM((SCORE_BUFFERS, tq, tq), jnp.float32),
            pltpu.VMEM((hps, tq, LANES), jnp.float32),
            pltpu.VMEM((hps, tq, LANES), jnp.float32),
        ],
        compiler_params=pltpu.CompilerParams(
            dimension_semantics=("arbitrary", "arbitrary", "arbitrary"),
            vmem_limit_bytes=VMEM_LIMIT_BYTES),
        name="mla_attn",
    )(q, k_t, v)


def _out_ffn(h2d, ypool2d, ymla2d, wo_pool, wo_mla, ffn_g, wg, wu, wd, real_rows=None):
    rows, d = h2d.shape
    tm = FFN_ROW_TILE
    if real_rows is None:
        grid = (rows // tm,)
        in_row_spec = lambda width: pl.BlockSpec((tm, width), lambda i: (i, 0))
        out_spec = in_row_spec(d)
        out_rows = rows
    else:
        batch, lp, first, count = real_rows
        tiles = count // tm
        assert count % tm == 0 and first % 16 == 0
        grid = (batch, tiles)
        in_row_spec = lambda width: pl.BlockSpec(
            (pl.Element(tm), pl.Element(width)),
            lambda b, j: (pl.multiple_of(b * lp + first + j * tm, 16), 0))
        out_spec = pl.BlockSpec((tm, d), lambda b, j: (b * tiles + j, 0))
        out_rows = batch * count
    return pl.pallas_call(
        _out_ffn_kernel,
        grid=grid,
        in_specs=[
            in_row_spec(d), in_row_spec(POOL_WIDTH), in_row_spec(MLA_HEADS * V_HEAD_DIM),
            _resident_spec(wo_pool.shape), _resident_spec(wo_mla.shape),
            _const_spec(ffn_g.shape),
            _resident_spec(wg.shape), _resident_spec(wu.shape), _resident_spec(wd.shape),
        ],
        out_specs=out_spec,
        out_shape=jax.ShapeDtypeStruct((out_rows, d), jnp.float32),
        scratch_shapes=[pltpu.VMEM((tm, d), jnp.float32)],
        compiler_params=pltpu.CompilerParams(
            dimension_semantics=("arbitrary",) * len(grid),
            vmem_limit_bytes=VMEM_LIMIT_BYTES),
        name="out_ffn",
    )(h2d, ypool2d, ymla2d, wo_pool, wo_mla, ffn_g, wg, wu, wd)


def _rope_tables(lp):
    inv = 1.0 / (ROPE_THETA ** (jnp.arange(0, QK_ROPE_DIM, 2, dtype=jnp.float32) / QK_ROPE_DIM))
    ang = jnp.arange(lp, dtype=jnp.float32)[:, None] * inv[None, :]
    cos, sin = jnp.cos(ang), jnp.sin(ang)
    zeros = lambda n: jnp.zeros((lp, n), jnp.float32)
    tail = LANES - QK_HEAD_DIM
    cos_t = jnp.concatenate([jnp.ones((lp, QK_NOPE_DIM), jnp.float32), cos, cos, zeros(tail)], axis=1)
    sin_t = jnp.concatenate([zeros(QK_NOPE_DIM), -sin, sin, zeros(tail)], axis=1)
    return cos_t, sin_t


def _swap_rope_halves(a):
    half = QK_ROPE_DIM // 2
    return jnp.concatenate([jnp.zeros_like(a[..., :QK_NOPE_DIM]),
                            a[..., QK_NOPE_DIM + half:], a[..., QK_NOPE_DIM:QK_NOPE_DIM + half]],
                           axis=-1)


def _pad_lanes(a, width):
    return jnp.pad(a, [(0, 0)] * (a.ndim - 1) + [(0, width - a.shape[-1])])


def kernel(x, meta_tokens, attn_norm_g, w_in, w_pool, pool_scale, q_a_norm_g, w_q_b,
           kv_a_norm_g, w_kv_b, q_norm_g, k_norm_g, w_out, ffn_norm_g, w_gate, w_up, w_down):
    batch, seq, d = x.shape
    depth = w_in.shape[0]
    length = N_META + seq
    lp = -(-length // SEQ_TILE) * SEQ_TILE
    assert (batch * lp) % FFN_ROW_TILE == 0 and D_FF % FF_CHUNK == 0
    bf16 = jnp.bfloat16

    cos_t, sin_t = _rope_tables(lp)
    h = None

    s1 = POOL_WIDTH
    s3 = s1 + Q_LORA_RANK + KV_LORA_RANK
    for l in range(depth):
        rope_block = jnp.pad(w_in[l][:, s3:], ((0, 0), (QK_NOPE_DIM, LANES - QK_HEAD_DIM)))
        w_in_p = jnp.concatenate([w_in[l][:, :s3], rope_block], axis=1).astype(bf16)
        wq = w_q_b[l].reshape(Q_LORA_RANK, MLA_HEADS, QK_HEAD_DIM)
        wq_p = _pad_lanes(wq, LANES).reshape(Q_LORA_RANK, MLA_HEADS * LANES).astype(bf16)
        wqs_p = _pad_lanes(_swap_rope_halves(wq), LANES)
        wqs_p = wqs_p.reshape(Q_LORA_RANK, MLA_HEADS * LANES).astype(bf16)
        wkv = w_kv_b[l].reshape(KV_LORA_RANK, MLA_HEADS, QK_NOPE_DIM + V_HEAD_DIM)
        wk_t = wkv[:, :, :QK_NOPE_DIM].reshape(KV_LORA_RANK, MLA_HEADS * QK_NOPE_DIM).T.astype(bf16)
        wv = wkv[:, :, QK_NOPE_DIM:].reshape(KV_LORA_RANK, MLA_HEADS * V_HEAD_DIM).astype(bf16)
        gq = _pad_lanes(q_norm_g[l][None, :], LANES)
        gqs = _pad_lanes(_swap_rope_halves(q_norm_g[l][None, :]), LANES)
        gkn_t = jnp.broadcast_to(k_norm_g[l][:QK_NOPE_DIM, None], (QK_NOPE_DIM, SEQ_TILE))
        gkr = jnp.pad(k_norm_g[l][None, QK_NOPE_DIM:], ((0, 0), (QK_NOPE_DIM, LANES - QK_HEAD_DIM)))

        consts = (attn_norm_g[l][None, :], w_in_p, w_pool[l].astype(bf16), pool_scale[l][None, :],
                  q_a_norm_g[l][None, :], wq_p, wqs_p, kv_a_norm_g[l][None, :], wk_t, wv,
                  gq, gqs, gkn_t, gkr)
        if l == 0:
            ypool, q, k_t, v, h = _mix_in(None, consts, cos_t, sin_t, lp,
                                          tokens_meta=(x, meta_tokens.astype(x.dtype)))
        else:
            ypool, q, k_t, v = _mix_in(h, consts, cos_t, sin_t, lp)
        ymla = _attention(q, k_t, v)

        wo = w_out[l].astype(bf16)
        wg = w_gate[l].astype(bf16)
        wu = w_up[l].astype(bf16)
        wd = w_down[l].astype(bf16)
        last = l == depth - 1
        h = _out_ffn(h.reshape(batch * lp, d), ypool.reshape(batch * lp, POOL_WIDTH),
                     ymla.reshape(batch * lp, MLA_HEADS * V_HEAD_DIM),
                     wo[:POOL_WIDTH], wo[POOL_WIDTH:], ffn_norm_g[l][None, :],
                     wg, wu, wd, real_rows=(batch, lp, N_META, seq) if last else None)
        h = h.reshape(batch, seq if last else lp, d)
    return h
```

```python
import functools

import jax
import jax.numpy as jnp
from jax import lax
from jax.experimental import pallas as pl
from jax.experimental.pallas import tpu as pltpu

D_MODEL = 1024
N_META = 16
POOL_WIDTH = 512
POOL_WINDOWS = (2, 4, 8, 16)
POOL_GROUP_DIM = 128
MLA_HEADS = 8
QK_NOPE_DIM = 64
QK_ROPE_DIM = 32
QK_HEAD_DIM = QK_NOPE_DIM + QK_ROPE_DIM
V_HEAD_DIM = 64
Q_LORA_RANK = 384
KV_LORA_RANK = 256
D_FF = 2816
ROPE_THETA = 10000.0
RMS_EPS = 1e-6

LANES = 128
SUBLANES = 8
assert POOL_WINDOWS == tuple(2 ** (k + 1) for k in range(len(POOL_WINDOWS)))
POOL_HALO = SUBLANES * len(POOL_WINDOWS)
assert POOL_HALO >= max(POOL_WINDOWS)

SEQ_TILE = 384
FFN_ROW_TILE = 512
FF_CHUNK = 256
FF_UNROLL = 5
HEADS_PER_STEP = 8
SCORE_LOOKAHEAD = 1
TAIL_LOOKAHEAD = 2
SCORE_BUFFERS = 4
assert SCORE_LOOKAHEAD <= TAIL_LOOKAHEAD < SCORE_BUFFERS and HEADS_PER_STEP % SCORE_BUFFERS == 0
IN_WIDTH_PADDED = 1280
VMEM_LIMIT_BYTES = 56 * 1024 * 1024

NEG_BIG = -1e30
LOG2_E = 1.4426950408889634


def _rms_scale(x, width):
    return lax.rsqrt(jnp.sum(x * x, axis=-1, keepdims=True) * (1.0 / width) + RMS_EPS)


def _rope(xn, cos_t, sin_lo, sin_hi):
    half = QK_ROPE_DIM // 2
    return (xn * cos_t
            + pltpu.roll(xn, LANES - half, 1) * sin_lo
            + pltpu.roll(xn, half, 1) * sin_hi)


N_MIX_SCRATCH = 6


def _mix_in_kernel(h_ref, *refs, n_tiles):
    _mix_in_steps(lambda: h_ref[0], refs)


def _mix_in_first_kernel(x_ref, meta_ref, *refs, seq, n_tiles):
    *refs, xt_ref = refs
    h_out_ref = refs[-N_MIX_SCRATCH - 1]
    refs = refs[:-N_MIX_SCRATCH - 1] + refs[-N_MIX_SCRATCH:]
    i = jnp.minimum(pl.program_id(1), n_tiles - 1)
    tm = xt_ref.shape[0]
    blk = x_ref[0]
    last_shift = (n_tiles - 1) * tm - N_META - (seq - tm)
    last_rows = tm - last_shift

    @pl.when(i == 0)
    def _():
        xt_ref[0:N_META, :] = meta_ref[...]
        xt_ref[N_META:, :] = blk[0:tm - N_META]

    @pl.when(i == n_tiles - 1)
    def _():
        xt_ref[0:last_rows, :] = blk[last_shift:]
        xt_ref[last_rows:, :] = jnp.zeros((tm - last_rows, blk.shape[1]), blk.dtype)

    @pl.when(jnp.logical_and(i > 0, i < n_tiles - 1))
    def _():
        xt_ref[...] = blk

    h_out_ref[0] = xt_ref[...]
    _mix_in_steps(lambda: xt_ref[...], refs)


def _mix_in_steps(load_x, refs):
    attn_g_ref, w_in_ref, *post_refs, za_ref, zb_ref = refs
    ue_ref = post_refs[-4]
    s = pl.program_id(1)

    @pl.when(s == 0)
    def _():
        zb_ref[...] = jnp.zeros(zb_ref.shape, zb_ref.dtype)

    @pl.when(s == 1)
    def _():
        ue_ref[0:POOL_HALO, :] = jnp.zeros((POOL_HALO, POOL_WIDTH), jnp.float32)

    def step(z_write, z_read):
        x = load_x()
        xn = (x * _rms_scale(x, D_MODEL) * attn_g_ref[...]).astype(jnp.bfloat16)
        z_write[...] = jnp.dot(xn, w_in_ref[...], preferred_element_type=jnp.float32)
        _mix_in_finish(z_read[...], s - 1, *post_refs)

    pl.when(s % 2 == 0)(lambda: step(za_ref, zb_ref))
    pl.when(s % 2 == 1)(lambda: step(zb_ref, za_ref))


def _mix_in_finish(z, i, w_pool_ref, pool_scale_ref,
                   qa_g_ref, wq_ref, wqs_ref, kva_g_ref, wkt_ref, wv_ref,
                   gq_ref, gqs_ref, gkn_ref, gkr_ref, cos_ref, sin_ref,
                   ypool_ref, q_ref, k_ref, v_ref, ue_ref, s2_ref, s4_ref, s8_ref):
    tm = z.shape[0]
    halo = POOL_HALO

    u = z[:, :POOL_WIDTH]

    gd = POOL_GROUP_DIM
    rows = tm + halo
    ue_ref[halo:, :] = u
    r1, r2, r3, r4 = (k * SUBLANES for k in (1, 2, 3, 4))
    s2_ref[r1:, :] = ue_ref[r1:, :] + ue_ref[r1 - 1:rows - 1, :]
    s4_ref[r2:, :] = s2_ref[r2:, gd:] + s2_ref[r2 - 2:rows - 2, gd:]
    s8_ref[r3:, :] = s4_ref[r3:, gd:] + s4_ref[r3 - 4:rows - 4, gd:]
    s16 = s8_ref[r4:, gd:] + s8_ref[r4 - 8:rows - 8, gd:]
    window_sums = (s2_ref[halo:, 0:gd], s4_ref[halo:, 0:gd], s8_ref[halo:, 0:gd], s16)

    pos = i * tm + lax.broadcasted_iota(jnp.int32, (tm, 1), 0)
    for g, w in enumerate(POOL_WINDOWS):
        sl = slice(g * gd, (g + 1) * gd)
        cnt = jnp.minimum(pos + 1, w).astype(jnp.float32)
        p = (window_sums[g] / cnt - u[:, sl]).astype(jnp.bfloat16)
        yg = jnp.dot(p, w_pool_ref[g], preferred_element_type=jnp.float32)
        ypool_ref[0, :, sl] = (yg * pool_scale_ref[:, sl]).astype(jnp.bfloat16)
    ue_ref[0:halo, :] = ue_ref[tm:tm + halo, :]

    lane = lax.broadcasted_iota(jnp.int32, (1, LANES), 1)
    is_nope = lane < QK_NOPE_DIM
    rope_mid = QK_NOPE_DIM + QK_ROPE_DIM // 2
    cos_t = cos_ref[...]
    sin_t = sin_ref[...]
    scale = QK_HEAD_DIM ** -0.5 * LOG2_E

    c_q = z[:, POOL_WIDTH:POOL_WIDTH + Q_LORA_RANK]
    cqn = (c_q * _rms_scale(c_q, Q_LORA_RANK) * qa_g_ref[...]).astype(jnp.bfloat16)
    qf = jnp.dot(cqn, wq_ref[...], preferred_element_type=jnp.float32)
    qfs = jnp.dot(cqn, wqs_ref[...], preferred_element_type=jnp.float32)
    cos_g = cos_t * (gq_ref[...] * scale)
    sin_g = sin_t * (gqs_ref[...] * scale)

    for hd in range(MLA_HEADS):
        sl = slice(hd * LANES, (hd + 1) * LANES)
        qh = qf[:, sl]
        sq = qh * qh
        ss_n = jnp.sum(jnp.where(is_nope, sq, 0.0), axis=-1, keepdims=True)
        ss_r = jnp.sum(jnp.where(is_nope, 0.0, sq), axis=-1, keepdims=True)
        r_n = lax.rsqrt(ss_n * (1.0 / QK_NOPE_DIM) + RMS_EPS)
        r_r = lax.rsqrt(ss_r * (1.0 / QK_ROPE_DIM) + RMS_EPS)
        rot = qh * cos_g + qfs[:, sl] * sin_g
        q_ref[0, hd] = (rot * jnp.where(is_nope, r_n, r_r)).astype(jnp.bfloat16)

    kv0 = POOL_WIDTH + Q_LORA_RANK
    c_kv = z[:, kv0:kv0 + KV_LORA_RANK]
    ckvn = (c_kv * _rms_scale(c_kv, KV_LORA_RANK) * kva_g_ref[...]).astype(jnp.bfloat16)
    v_ref[0] = jnp.dot(ckvn, wv_ref[...],
                       preferred_element_type=jnp.float32).astype(jnp.bfloat16)

    kr = z[:, kv0 + KV_LORA_RANK:]
    krn = kr * _rms_scale(kr, QK_ROPE_DIM) * gkr_ref[...]
    kr_rot = _rope(krn, cos_t, jnp.where(lane < rope_mid, sin_t, 0.0),
                   jnp.where(lane < rope_mid, 0.0, sin_t))
    kr_rot_t = kr_rot.T[QK_NOPE_DIM:, :].astype(jnp.bfloat16)

    kf_t = lax.dot_general(wkt_ref[...], ckvn, (((1,), (1,)), ((), ())),
                           preferred_element_type=jnp.float32)
    for hd in range(MLA_HEADS):
        kh = kf_t[hd * QK_NOPE_DIM:(hd + 1) * QK_NOPE_DIM, :]
        r_k = lax.rsqrt(jnp.sum(kh * kh, axis=0, keepdims=True) * (1.0 / QK_NOPE_DIM) + RMS_EPS)
        k_ref[0, hd, 0, 0:QK_NOPE_DIM, :] = (kh * r_k * gkn_ref[...]).astype(jnp.bfloat16)
        k_ref[0, hd, 0, QK_NOPE_DIM:, :] = kr_rot_t


def _attn_kernel(q_ref, k_ref, v_ref, o_ref, s_scr, m_scr, acc_scr):
    i = pl.program_id(2)
    tq = q_ref.shape[2]
    tk = s_scr.shape[2]
    n_slabs = tk // LANES
    lane = lax.broadcasted_iota(jnp.int32, (1, LANES), 1)
    low_half = lane < V_HEAD_DIM

    n_buf = s_scr.shape[0]
    heads = HEADS_PER_STEP

    def scores(hh, j):
        s_scr[hh % n_buf] = jnp.dot(q_ref[0, hh], k_ref[0, hh, j],
                                    preferred_element_type=jnp.float32)

    def softmax_pv(hh, j, masked):
        rb = LANES
        if masked:
            tri = (lax.broadcasted_iota(jnp.int32, (rb, LANES), 1)
                   <= lax.broadcasted_iota(jnp.int32, (rb, LANES), 0))
        alphas, ps = [], []
        for rblk in range(tq // rb):
            r0 = rblk * rb
            blk = []
            for c in range(n_slabs if not masked else rblk + 1):
                sl = s_scr[hh % n_buf, r0:r0 + rb, c * LANES:(c + 1) * LANES]
                if masked and c == rblk:
                    sl = jnp.where(tri, sl, NEG_BIG)
                blk.append(sl)
            mx = blk[0]
            for sl in blk[1:]:
                mx = jnp.maximum(mx, sl)
            m_old = m_scr[hh, r0:r0 + rb]
            m_new = jnp.maximum(m_old, jnp.max(mx, axis=-1, keepdims=True))
            m_scr[hh, r0:r0 + rb] = m_new
            alphas.append(jnp.exp2(m_old - m_new))
            p_blk = [jnp.exp2(sl - m_new).astype(jnp.bfloat16) for sl in blk]
            p_blk += [jnp.zeros((rb, LANES), jnp.bfloat16)] * (n_slabs - len(blk))
            ps.append(jnp.concatenate(p_blk, axis=1))
        alpha = jnp.concatenate(alphas, axis=0)
        p = jnp.concatenate(ps, axis=0)
        start = pl.multiple_of(j * tk, tk)
        pair = hh // 2
        vj = v_ref[0, pl.ds(start, tk), pair * LANES:(pair + 1) * LANES]
        own = low_half if hh % 2 == 0 else jnp.logical_not(low_half)
        vj = jnp.where(own, vj, jnp.ones_like(vj))
        acc_scr[hh] = alpha * acc_scr[hh] + jnp.dot(p, vj, preferred_element_type=jnp.float32)

    m_scr[...] = jnp.full(m_scr.shape, NEG_BIG, jnp.float32)
    acc_scr[...] = jnp.zeros(acc_scr.shape, jnp.float32)
    ahead = SCORE_LOOKAHEAD
    for hh in range(ahead):
        scores(hh, 0)

    def body(j, carry):
        for hh in range(heads):
            scores((hh + ahead) % heads, j + (hh + ahead) // heads)
            softmax_pv(hh, j, masked=False)
        return carry

    lax.fori_loop(0, i, body, 0)
    tail_ahead = TAIL_LOOKAHEAD
    for hh in range(ahead, tail_ahead):
        scores(hh, i)
    for hh in range(heads):
        if hh + tail_ahead < heads:
            scores(hh + tail_ahead, i)
        softmax_pv(hh, i, masked=True)

    for pair in range(HEADS_PER_STEP // 2):
        acc_e = acc_scr[2 * pair]
        acc_o = acc_scr[2 * pair + 1]
        out_e = acc_e / acc_e[:, V_HEAD_DIM:V_HEAD_DIM + 1]
        out_o = acc_o / acc_o[:, 0:1]
        o_ref[0, :, pair * LANES:(pair + 1) * LANES] = jnp.where(
            low_half, out_e, out_o).astype(jnp.bfloat16)


def _out_ffn_kernel(h_ref, ypool_ref, ymla_ref, wo_ref, ffn_g_ref,
                    wg_ref, wu_ref, wd_ref, o_ref, acc_ref):
    h1 = (h_ref[...]
          + jnp.dot(ypool_ref[...], wo_ref[0, :POOL_WIDTH, :], preferred_element_type=jnp.float32)
          + jnp.dot(ymla_ref[...], wo_ref[0, POOL_WIDTH:, :], preferred_element_type=jnp.float32))
    g = (h1 * _rms_scale(h1, D_MODEL) * ffn_g_ref[0]).astype(jnp.bfloat16)
    acc_ref[...] = h1

    def chunk_slice(c):
        return pl.ds(pl.multiple_of(c * FF_CHUNK, FF_CHUNK), FF_CHUNK)

    def gate_up(c):
        cols = chunk_slice(c)
        gate = jnp.dot(g, wg_ref[0, :, cols], preferred_element_type=jnp.float32)
        up = jnp.dot(g, wu_ref[0, :, cols], preferred_element_type=jnp.float32)
        return (gate * (1.0 / (1.0 + jnp.exp(-gate))) * up).astype(jnp.bfloat16)

    def down(act, c):
        acc_ref[...] += jnp.dot(act, wd_ref[0, chunk_slice(c), :],
                                preferred_element_type=jnp.float32)

    n_chunks = wg_ref.shape[2] // FF_CHUNK

    def chunk_group(first, count):
        act = gate_up(first)
        for k in range(count):
            nxt = gate_up(first + k + 1) if k + 1 < count else None
            down(act, first + k)
            act = nxt

    def loop_step(step, carry):
        chunk_group(step * FF_UNROLL, FF_UNROLL)
        return carry

    n_steps = n_chunks // FF_UNROLL
    lax.fori_loop(0, n_steps, loop_step, 0)
    if n_chunks % FF_UNROLL:
        chunk_group(n_steps * FF_UNROLL, n_chunks % FF_UNROLL)
    o_ref[...] = acc_ref[...]


def _const_spec(shape):
    nd = len(shape)
    return pl.BlockSpec(shape, lambda *_: (0,) * nd)


def _resident_spec(shape):
    nd = len(shape)
    return pl.BlockSpec(shape, lambda *_: (0,) * nd, pipeline_mode=pl.Buffered(1))


def _mix_in(h, consts, cos_t, sin_t, lp, tokens_meta=None):
    first = tokens_meta is not None
    batch = tokens_meta[0].shape[0] if first else h.shape[0]
    d = D_MODEL
    tm = SEQ_TILE
    n_tiles = lp // tm
    proj = lambda s: jnp.minimum(s, n_tiles - 1)
    done = lambda s: jnp.maximum(s - 1, 0)
    in_row_spec = lambda width: pl.BlockSpec((1, tm, width), lambda b, s: (b, proj(s), 0))
    row_spec = lambda width: pl.BlockSpec((1, tm, width), lambda b, s: (b, done(s), 0))
    q_spec = pl.BlockSpec((1, MLA_HEADS, tm, LANES), lambda b, s: (b, 0, done(s), 0))
    kt_spec = pl.BlockSpec((1, MLA_HEADS, 1, LANES, tm), lambda b, s: (b, 0, done(s), 0, 0))
    table_spec = pl.BlockSpec((tm, LANES), lambda b, s: (done(s), 0))
    out_specs = [row_spec(POOL_WIDTH), q_spec, kt_spec, row_spec(MLA_HEADS * V_HEAD_DIM)]
    out_shape = [
        jax.ShapeDtypeStruct((batch, lp, POOL_WIDTH), jnp.bfloat16),
        jax.ShapeDtypeStruct((batch, MLA_HEADS, lp, LANES), jnp.bfloat16),
        jax.ShapeDtypeStruct((batch, MLA_HEADS, n_tiles, LANES, tm), jnp.bfloat16),
        jax.ShapeDtypeStruct((batch, lp, MLA_HEADS * V_HEAD_DIM), jnp.bfloat16),
    ]
    scratch = [pltpu.VMEM((tm + POOL_HALO, POOL_WIDTH - drop * POOL_GROUP_DIM), jnp.float32)
               for drop in (0, 0, 1, 2)]
    scratch += [pltpu.VMEM((tm, IN_WIDTH_PADDED), jnp.float32)] * 2
    assert len(scratch) == N_MIX_SCRATCH
    if first:
        x, meta = tokens_meta
        seq = x.shape[1]
        body = functools.partial(_mix_in_first_kernel, seq=seq, n_tiles=n_tiles)
        x_spec = pl.BlockSpec(
            (pl.Element(1), pl.Element(tm), pl.Element(d)),
            lambda b, s: (b, pl.multiple_of(jnp.clip(proj(s) * tm - N_META, 0, seq - tm),
                                            SUBLANES), 0))
        lead_specs = [x_spec, _const_spec(meta.shape)]
        lead_args = [x, meta]
        out_specs.append(in_row_spec(d))
        out_shape.append(jax.ShapeDtypeStruct((batch, lp, d), jnp.float32))
        scratch.append(pltpu.VMEM((tm, d), jnp.float32))
    else:
        body = functools.partial(_mix_in_kernel, n_tiles=n_tiles)
        lead_specs = [in_row_spec(d)]
        lead_args = [h]
    return pl.pallas_call(
        body,
        grid=(batch, n_tiles + 1),
        in_specs=lead_specs + [_const_spec(c.shape) for c in consts] + [table_spec, table_spec],
        out_specs=out_specs,
        out_shape=out_shape,
        scratch_shapes=scratch,
        compiler_params=pltpu.CompilerParams(
            dimension_semantics=("arbitrary", "arbitrary"),
            vmem_limit_bytes=VMEM_LIMIT_BYTES),
        name="mix_in",
    )(*lead_args, *consts, cos_t, sin_t)


def _attention(q, k_t, v):
    batch, heads, lp, _ = q.shape
    tq = SEQ_TILE
    hps = HEADS_PER_STEP
    n_chunks = k_t.shape[2]
    assert k_t.shape[4] == tq
    return pl.pallas_call(
        _attn_kernel,
        grid=(batch, heads // hps, lp // tq),
        in_specs=[
            pl.BlockSpec((1, hps, tq, LANES), lambda b, hp, i: (b, hp, i, 0)),
            pl.BlockSpec((1, hps, n_chunks, LANES, tq), lambda b, hp, i: (b, hp, 0, 0, 0)),
            pl.BlockSpec((1, lp, hps * V_HEAD_DIM), lambda b, hp, i: (b, 0, hp)),
        ],
        out_specs=pl.BlockSpec((1, tq, hps * V_HEAD_DIM), lambda b, hp, i: (b, i, hp)),
        out_shape=jax.ShapeDtypeStruct((batch, lp, heads * V_HEAD_DIM), jnp.bfloat16),
        scratch_shapes=[
            pltpu.VMEM((SCORE_BUFFERS, tq, tq), jnp.float32),
            pltpu.VMEM((hps, tq, LANES), jnp.float32),
            pltpu.VMEM((hps, tq, LANES), jnp.float32),
        ],
        compiler_params=pltpu.CompilerParams(
            dimension_semantics=("arbitrary", "arbitrary", "arbitrary"),
            vmem_limit_bytes=VMEM_LIMIT_BYTES),
        name="mla_attn",
    )(q, k_t, v)


def _layer_spec(stacked, layer):
    shape = (1,) + stacked.shape[1:]
    nd = len(shape)
    return pl.BlockSpec(shape, lambda *_: (layer,) + (0,) * (nd - 1),
                        pipeline_mode=pl.Buffered(1))


def _out_ffn(h2d, ypool2d, ymla2d, wo, ffn_g, wg, wu, wd, layer, real_rows=None):
    rows, d = h2d.shape
    tm = FFN_ROW_TILE
    if real_rows is None:
        grid = (rows // tm,)
        in_row_spec = lambda width: pl.BlockSpec((tm, width), lambda i: (i, 0))
        out_spec = in_row_spec(d)
        out_rows = rows
    else:
        batch, lp, first, count = real_rows
        tiles = count // tm
        assert count % tm == 0 and first % 16 == 0
        grid = (batch, tiles)
        in_row_spec = lambda width: pl.BlockSpec(
            (pl.Element(tm), pl.Element(width)),
            lambda b, j: (pl.multiple_of(b * lp + first + j * tm, 16), 0))
        out_spec = pl.BlockSpec((tm, d), lambda b, j: (b * tiles + j, 0))
        out_rows = batch * count
    return pl.pallas_call(
        _out_ffn_kernel,
        grid=grid,
        in_specs=[
            in_row_spec(d), in_row_spec(POOL_WIDTH), in_row_spec(MLA_HEADS * V_HEAD_DIM),
            _layer_spec(wo, layer), _layer_spec(ffn_g, layer),
            _layer_spec(wg, layer), _layer_spec(wu, layer), _layer_spec(wd, layer),
        ],
        out_specs=out_spec,
        out_shape=jax.ShapeDtypeStruct((out_rows, d), jnp.float32),
        scratch_shapes=[pltpu.VMEM((tm, d), jnp.float32)],
        compiler_params=pltpu.CompilerParams(
            dimension_semantics=("arbitrary",) * len(grid),
            vmem_limit_bytes=VMEM_LIMIT_BYTES),
        name="out_ffn",
    )(h2d, ypool2d, ymla2d, wo, ffn_g, wg, wu, wd)


def _rope_tables(lp):
    inv = 1.0 / (ROPE_THETA ** (jnp.arange(0, QK_ROPE_DIM, 2, dtype=jnp.float32) / QK_ROPE_DIM))
    ang = jnp.arange(lp, dtype=jnp.float32)[:, None] * inv[None, :]
    cos, sin = jnp.cos(ang), jnp.sin(ang)
    zeros = lambda n: jnp.zeros((lp, n), jnp.float32)
    tail = LANES - QK_HEAD_DIM
    cos_t = jnp.concatenate([jnp.ones((lp, QK_NOPE_DIM), jnp.float32), cos, cos, zeros(tail)], axis=1)
    sin_t = jnp.concatenate([zeros(QK_NOPE_DIM), -sin, sin, zeros(tail)], axis=1)
    return cos_t, sin_t


def _swap_rope_halves(a):
    half = QK_ROPE_DIM // 2
    return jnp.concatenate([jnp.zeros_like(a[..., :QK_NOPE_DIM]),
                            a[..., QK_NOPE_DIM + half:], a[..., QK_NOPE_DIM:QK_NOPE_DIM + half]],
                           axis=-1)


def _pad_lanes(a, width):
    return jnp.pad(a, [(0, 0)] * (a.ndim - 1) + [(0, width - a.shape[-1])])


def kernel(x, meta_tokens, attn_norm_g, w_in, w_pool, pool_scale, q_a_norm_g, w_q_b,
           kv_a_norm_g, w_kv_b, q_norm_g, k_norm_g, w_out, ffn_norm_g, w_gate, w_up, w_down):
    batch, seq, d = x.shape
    depth = w_in.shape[0]
    length = N_META + seq
    lp = -(-length // SEQ_TILE) * SEQ_TILE
    assert (batch * lp) % FFN_ROW_TILE == 0 and D_FF % FF_CHUNK == 0
    bf16 = jnp.bfloat16

    cos_t, sin_t = _rope_tables(lp)
    h = None
    wo_all, wg_all, wu_all, wd_all = (w.astype(bf16) for w in (w_out, w_gate, w_up, w_down))
    ffn_g_all = ffn_norm_g[:, None, :]

    s1 = POOL_WIDTH
    s3 = s1 + Q_LORA_RANK + KV_LORA_RANK
    for l in range(depth):
        rope_block = jnp.pad(w_in[l][:, s3:], ((0, 0), (QK_NOPE_DIM, LANES - QK_HEAD_DIM)))
        w_in_p = jnp.concatenate([w_in[l][:, :s3], rope_block], axis=1).astype(bf16)
        wq = w_q_b[l].reshape(Q_LORA_RANK, MLA_HEADS, QK_HEAD_DIM)
        wq_p = _pad_lanes(wq, LANES).reshape(Q_LORA_RANK, MLA_HEADS * LANES).astype(bf16)
        wqs_p = _pad_lanes(_swap_rope_halves(wq), LANES)
        wqs_p = wqs_p.reshape(Q_LORA_RANK, MLA_HEADS * LANES).astype(bf16)
        wkv = w_kv_b[l].reshape(KV_LORA_RANK, MLA_HEADS, QK_NOPE_DIM + V_HEAD_DIM)
        wk_t = wkv[:, :, :QK_NOPE_DIM].reshape(KV_LORA_RANK, MLA_HEADS * QK_NOPE_DIM).T.astype(bf16)
        wv = wkv[:, :, QK_NOPE_DIM:].reshape(KV_LORA_RANK, MLA_HEADS * V_HEAD_DIM).astype(bf16)
        gq = _pad_lanes(q_norm_g[l][None, :], LANES)
        gqs = _pad_lanes(_swap_rope_halves(q_norm_g[l][None, :]), LANES)
        gkn_t = jnp.broadcast_to(k_norm_g[l][:QK_NOPE_DIM, None], (QK_NOPE_DIM, SEQ_TILE))
        gkr = jnp.pad(k_norm_g[l][None, QK_NOPE_DIM:], ((0, 0), (QK_NOPE_DIM, LANES - QK_HEAD_DIM)))

        consts = (attn_norm_g[l][None, :], w_in_p, w_pool[l].astype(bf16), pool_scale[l][None, :],
                  q_a_norm_g[l][None, :], wq_p, wqs_p, kv_a_norm_g[l][None, :], wk_t, wv,
                  gq, gqs, gkn_t, gkr)
        if l == 0:
            ypool, q, k_t, v, h = _mix_in(None, consts, cos_t, sin_t, lp,
                                          tokens_meta=(x, meta_tokens.astype(x.dtype)))
        else:
            ypool, q, k_t, v = _mix_in(h, consts, cos_t, sin_t, lp)
        ymla = _attention(q, k_t, v)

        last = l == depth - 1
        h = _out_ffn(h.reshape(batch * lp, d), ypool.reshape(batch * lp, POOL_WIDTH),
                     ymla.reshape(batch * lp, MLA_HEADS * V_HEAD_DIM),
                     wo_all, ffn_g_all, wg_all, wu_all, wd_all, l,
                     real_rows=(batch, lp, N_META, seq) if last else None)
        h = h.reshape(batch, seq if last else lp, d)
    return h
```

```python
import functools

import jax
import jax.numpy as jnp
from jax import lax
from jax.experimental import pallas as pl
from jax.experimental.pallas import tpu as pltpu

D_MODEL = 1024
N_META = 16
POOL_WIDTH = 512
POOL_WINDOWS = (2, 4, 8, 16)
POOL_GROUP_DIM = 128
MLA_HEADS = 8
QK_NOPE_DIM = 64
QK_ROPE_DIM = 32
QK_HEAD_DIM = QK_NOPE_DIM + QK_ROPE_DIM
V_HEAD_DIM = 64
Q_LORA_RANK = 384
KV_LORA_RANK = 256
D_FF = 2816
ROPE_THETA = 10000.0
RMS_EPS = 1e-6

LANES = 128
SUBLANES = 8
assert POOL_WINDOWS == tuple(2 ** (k + 1) for k in range(len(POOL_WINDOWS)))
POOL_HALO = SUBLANES * len(POOL_WINDOWS)
assert POOL_HALO >= max(POOL_WINDOWS)

SEQ_TILE = 384
FFN_ROW_TILES = (512, 768, 1024)
FF_CHUNK = 256
FF_UNROLL = 5
HEADS_PER_STEP = 8
SCORE_LOOKAHEAD = 1
TAIL_LOOKAHEAD = 2
SCORE_BUFFERS = 4
assert SCORE_LOOKAHEAD <= TAIL_LOOKAHEAD < SCORE_BUFFERS and HEADS_PER_STEP % SCORE_BUFFERS == 0
IN_WIDTH_PADDED = 1280
VMEM_LIMIT_BYTES = 56 * 1024 * 1024

NEG_BIG = -1e30
LOG2_E = 1.4426950408889634


def _rms_scale(x, width):
    return lax.rsqrt(jnp.sum(x * x, axis=-1, keepdims=True) * (1.0 / width) + RMS_EPS)


def _rope(xn, cos_t, sin_lo, sin_hi):
    half = QK_ROPE_DIM // 2
    return (xn * cos_t
            + pltpu.roll(xn, LANES - half, 1) * sin_lo
            + pltpu.roll(xn, half, 1) * sin_hi)


N_MIX_SCRATCH = 6


def _mix_in_kernel(h_ref, *refs, n_tiles):
    _mix_in_steps(lambda: h_ref[0], refs)


def _mix_in_first_kernel(x_ref, meta_ref, *refs, seq, n_tiles):
    h_out_ref = refs[-N_MIX_SCRATCH - 1]
    refs = refs[:-N_MIX_SCRATCH - 1] + refs[-N_MIX_SCRATCH:]

    def load_x():
        i = jnp.minimum(pl.program_id(1), n_tiles - 1)
        blk = x_ref[0]
        tm = blk.shape[0]
        last_shift = (n_tiles - 1) * tm - N_META - (seq - tm)
        first_tile = jnp.concatenate([meta_ref[...], blk[:tm - N_META]], axis=0)
        last_tile = jnp.concatenate(
            [blk[last_shift:], jnp.zeros((last_shift, blk.shape[1]), blk.dtype)], axis=0)
        x = jnp.where(i == 0, first_tile, jnp.where(i == n_tiles - 1, last_tile, blk))
        h_out_ref[0] = x
        return x

    _mix_in_steps(load_x, refs)


def _mix_in_steps(load_x, refs):
    attn_g_ref, w_in_ref, *post_refs, za_ref, zb_ref = refs
    ue_ref = post_refs[-4]
    s = pl.program_id(1)

    @pl.when(s == 0)
    def _():
        zb_ref[...] = jnp.zeros(zb_ref.shape, zb_ref.dtype)

    @pl.when(s == 1)
    def _():
        ue_ref[0:POOL_HALO, :] = jnp.zeros((POOL_HALO, POOL_WIDTH), jnp.float32)

    def step(z_write, z_read):
        x = load_x()
        xn = (x * _rms_scale(x, D_MODEL) * attn_g_ref[...]).astype(jnp.bfloat16)
        z_write[...] = jnp.dot(xn, w_in_ref[...], preferred_element_type=jnp.float32)
        _mix_in_finish(z_read[...], s - 1, *post_refs)

    pl.when(s % 2 == 0)(lambda: step(za_ref, zb_ref))
    pl.when(s % 2 == 1)(lambda: step(zb_ref, za_ref))


def _mix_in_finish(z, i, w_pool_ref, pool_scale_ref,
                   qa_g_ref, wq_ref, wqs_ref, kva_g_ref, wkt_ref, wv_ref,
                   gq_ref, gqs_ref, gkn_ref, gkr_ref, cos_ref, sin_ref,
                   ypool_ref, q_ref, k_ref, v_ref, ue_ref, s2_ref, s4_ref, s8_ref):
    tm = z.shape[0]
    halo = POOL_HALO

    u = z[:, :POOL_WIDTH]

    gd = POOL_GROUP_DIM
    rows = tm + halo
    ue_ref[halo:, :] = u
    r1, r2, r3, r4 = (k * SUBLANES for k in (1, 2, 3, 4))
    s2_ref[r1:, :] = ue_ref[r1:, :] + ue_ref[r1 - 1:rows - 1, :]
    s4_ref[r2:, :] = s2_ref[r2:, gd:] + s2_ref[r2 - 2:rows - 2, gd:]
    s8_ref[r3:, :] = s4_ref[r3:, gd:] + s4_ref[r3 - 4:rows - 4, gd:]
    s16 = s8_ref[r4:, gd:] + s8_ref[r4 - 8:rows - 8, gd:]
    window_sums = (s2_ref[halo:, 0:gd], s4_ref[halo:, 0:gd], s8_ref[halo:, 0:gd], s16)

    pos = i * tm + lax.broadcasted_iota(jnp.int32, (tm, 1), 0)
    pooled = []
    for g, w in enumerate(POOL_WINDOWS):
        cnt = jnp.minimum(pos + 1, w).astype(jnp.float32)
        pooled.append((window_sums[g] / cnt - u[:, g * gd:(g + 1) * gd]).astype(jnp.bfloat16))
    for pair in range(len(POOL_WINDOWS) // 2):
        sl = slice(2 * pair * gd, 2 * (pair + 1) * gd)
        p2 = jnp.concatenate(pooled[2 * pair:2 * pair + 2], axis=1)
        y2 = jnp.dot(p2, w_pool_ref[pair], preferred_element_type=jnp.float32)
        ypool_ref[0, :, sl] = (y2 * pool_scale_ref[:, sl]).astype(jnp.bfloat16)
    ue_ref[0:halo, :] = ue_ref[tm:tm + halo, :]

    lane = lax.broadcasted_iota(jnp.int32, (1, LANES), 1)
    is_nope = lane < QK_NOPE_DIM
    rope_mid = QK_NOPE_DIM + QK_ROPE_DIM // 2
    cos_t = cos_ref[...]
    sin_t = sin_ref[...]
    scale = QK_HEAD_DIM ** -0.5 * LOG2_E

    c_q = z[:, POOL_WIDTH:POOL_WIDTH + Q_LORA_RANK]
    cqn = (c_q * _rms_scale(c_q, Q_LORA_RANK) * qa_g_ref[...]).astype(jnp.bfloat16)
    qf = jnp.dot(cqn, wq_ref[...], preferred_element_type=jnp.float32)
    qfs = jnp.dot(cqn, wqs_ref[...], preferred_element_type=jnp.float32)
    cos_g = cos_t * (gq_ref[...] * scale)
    sin_g = sin_t * (gqs_ref[...] * scale)

    for hd in range(MLA_HEADS):
        sl = slice(hd * LANES, (hd + 1) * LANES)
        qh = qf[:, sl]
        sq = qh * qh
        ss_n = jnp.sum(jnp.where(is_nope, sq, 0.0), axis=-1, keepdims=True)
        ss_r = jnp.sum(jnp.where(is_nope, 0.0, sq), axis=-1, keepdims=True)
        r_n = lax.rsqrt(ss_n * (1.0 / QK_NOPE_DIM) + RMS_EPS)
        r_r = lax.rsqrt(ss_r * (1.0 / QK_ROPE_DIM) + RMS_EPS)
        rot = qh * cos_g + qfs[:, sl] * sin_g
        q_ref[0, hd] = (rot * jnp.where(is_nope, r_n, r_r)).astype(jnp.bfloat16)

    kv0 = POOL_WIDTH + Q_LORA_RANK
    c_kv = z[:, kv0:kv0 + KV_LORA_RANK]
    ckvn = (c_kv * _rms_scale(c_kv, KV_LORA_RANK) * kva_g_ref[...]).astype(jnp.bfloat16)
    v_ref[0] = jnp.dot(ckvn, wv_ref[...],
                       preferred_element_type=jnp.float32).astype(jnp.bfloat16)

    kr = z[:, kv0 + KV_LORA_RANK:]
    krn = kr * _rms_scale(kr, QK_ROPE_DIM) * gkr_ref[...]
    kr_rot = _rope(krn, cos_t, jnp.where(lane < rope_mid, sin_t, 0.0),
                   jnp.where(lane < rope_mid, 0.0, sin_t))
    kr_rot_t = kr_rot.T[QK_NOPE_DIM:, :].astype(jnp.bfloat16)

    kf_t = lax.dot_general(wkt_ref[...], ckvn, (((1,), (1,)), ((), ())),
                           preferred_element_type=jnp.float32)
    for hd in range(MLA_HEADS):
        kh = kf_t[hd * QK_NOPE_DIM:(hd + 1) * QK_NOPE_DIM, :]
        r_k = lax.rsqrt(jnp.sum(kh * kh, axis=0, keepdims=True) * (1.0 / QK_NOPE_DIM) + RMS_EPS)
        k_ref[0, hd, 0, 0:QK_NOPE_DIM, :] = (kh * r_k * gkn_ref[...]).astype(jnp.bfloat16)
        k_ref[0, hd, 0, QK_NOPE_DIM:, :] = kr_rot_t


def _attn_kernel(q_ref, k_ref, v_ref, o_ref, s_scr, m_scr, acc_scr):
    i = pl.program_id(2)
    tq = q_ref.shape[2]
    tk = s_scr.shape[2]
    n_slabs = tk // LANES
    lane = lax.broadcasted_iota(jnp.int32, (1, LANES), 1)
    low_half = lane < V_HEAD_DIM

    n_buf = s_scr.shape[0]
    heads = HEADS_PER_STEP

    def scores(hh, j):
        s_scr[hh % n_buf] = jnp.dot(q_ref[0, hh], k_ref[0, hh, j],
                                    preferred_element_type=jnp.float32)

    def softmax_pv(hh, j, masked):
        rb = LANES
        if masked:
            tri = (lax.broadcasted_iota(jnp.int32, (rb, LANES), 1)
                   <= lax.broadcasted_iota(jnp.int32, (rb, LANES), 0))
        alphas, ps = [], []
        for rblk in range(tq // rb):
            r0 = rblk * rb
            blk = []
            for c in range(n_slabs if not masked else rblk + 1):
                sl = s_scr[hh % n_buf, r0:r0 + rb, c * LANES:(c + 1) * LANES]
                if masked and c == rblk:
                    sl = jnp.where(tri, sl, NEG_BIG)
                blk.append(sl)
            mx = blk[0]
            for sl in blk[1:]:
                mx = jnp.maximum(mx, sl)
            m_old = m_scr[hh, r0:r0 + rb]
            m_new = jnp.maximum(m_old, jnp.max(mx, axis=-1, keepdims=True))
            m_scr[hh, r0:r0 + rb] = m_new
            alphas.append(jnp.exp2(m_old - m_new))
            p_blk = [jnp.exp2(sl - m_new).astype(jnp.bfloat16) for sl in blk]
            p_blk += [jnp.zeros((rb, LANES), jnp.bfloat16)] * (n_slabs - len(blk))
            ps.append(jnp.concatenate(p_blk, axis=1))
        alpha = jnp.concatenate(alphas, axis=0)
        p = jnp.concatenate(ps, axis=0)
        start = pl.multiple_of(j * tk, tk)
        pair = hh // 2
        vj = v_ref[0, pl.ds(start, tk), pair * LANES:(pair + 1) * LANES]
        own = low_half if hh % 2 == 0 else jnp.logical_not(low_half)
        vj = jnp.where(own, vj, jnp.ones_like(vj))
        acc_scr[hh] = alpha * acc_scr[hh] + jnp.dot(p, vj, preferred_element_type=jnp.float32)

    m_scr[...] = jnp.full(m_scr.shape, NEG_BIG, jnp.float32)
    acc_scr[...] = jnp.zeros(acc_scr.shape, jnp.float32)
    ahead = SCORE_LOOKAHEAD
    for hh in range(ahead):
        scores(hh, 0)

    def chunk(j):
        for hh in range(heads):
            scores((hh + ahead) % heads, j + (hh + ahead) // heads)
            softmax_pv(hh, j, masked=False)

    def chunk_pair(jj, carry):
        chunk(2 * jj)
        chunk(2 * jj + 1)
        return carry

    lax.fori_loop(0, lax.shift_right_logical(i, 1), chunk_pair, 0)

    @pl.when(i % 2 == 1)
    def _():
        chunk(i - 1)

    tail_ahead = TAIL_LOOKAHEAD
    for hh in range(ahead, tail_ahead):
        scores(hh, i)
    for hh in range(heads):
        if hh + tail_ahead < heads:
            scores(hh + tail_ahead, i)
        softmax_pv(hh, i, masked=True)

    for pair in range(HEADS_PER_STEP // 2):
        acc_e = acc_scr[2 * pair]
        acc_o = acc_scr[2 * pair + 1]
        out_e = acc_e / acc_e[:, V_HEAD_DIM:V_HEAD_DIM + 1]
        out_o = acc_o / acc_o[:, 0:1]
        o_ref[0, :, pair * LANES:(pair + 1) * LANES] = jnp.where(
            low_half, out_e, out_o).astype(jnp.bfloat16)


def _out_ffn_kernel(h_ref, ypool_ref, ymla_ref, wo_ref, ffn_g_ref,
                    wg_ref, wu_ref, wd_ref, o_ref, acc_ref):
    h1 = (h_ref[...]
          + jnp.dot(ypool_ref[...], wo_ref[0, :POOL_WIDTH, :], preferred_element_type=jnp.float32)
          + jnp.dot(ymla_ref[...], wo_ref[0, POOL_WIDTH:, :], preferred_element_type=jnp.float32))
    g = (h1 * _rms_scale(h1, D_MODEL) * ffn_g_ref[0]).astype(jnp.bfloat16)
    acc_ref[...] = h1

    def chunk_slice(c):
        return pl.ds(pl.multiple_of(c * FF_CHUNK, FF_CHUNK), FF_CHUNK)

    def gate_up(c):
        cols = chunk_slice(c)
        gate = jnp.dot(g, wg_ref[0, :, cols], preferred_element_type=jnp.float32)
        up = jnp.dot(g, wu_ref[0, :, cols], preferred_element_type=jnp.float32)
        return (gate * (1.0 / (1.0 + jnp.exp(-gate))) * up).astype(jnp.bfloat16)

    def down(act, c):
        acc_ref[...] += jnp.dot(act, wd_ref[0, chunk_slice(c), :],
                                preferred_element_type=jnp.float32)

    n_chunks = wg_ref.shape[2] // FF_CHUNK

    def chunk_group(first, count):
        act = gate_up(first)
        for k in range(count):
            nxt = gate_up(first + k + 1) if k + 1 < count else None
            down(act, first + k)
            act = nxt

    def loop_step(step, carry):
        chunk_group(step * FF_UNROLL, FF_UNROLL)
        return carry

    n_steps = n_chunks // FF_UNROLL
    lax.fori_loop(0, n_steps, loop_step, 0)
    if n_chunks % FF_UNROLL:
        chunk_group(n_steps * FF_UNROLL, n_chunks % FF_UNROLL)
    o_ref[...] = acc_ref[...]


def _const_spec(shape):
    nd = len(shape)
    return pl.BlockSpec(shape, lambda *_: (0,) * nd)


def _resident_spec(shape):
    nd = len(shape)
    return pl.BlockSpec(shape, lambda *_: (0,) * nd, pipeline_mode=pl.Buffered(1))


def _mix_in(h, consts, cos_t, sin_t, lp, tokens_meta=None):
    first = tokens_meta is not None
    batch = tokens_meta[0].shape[0] if first else h.shape[0]
    d = D_MODEL
    tm = SEQ_TILE
    n_tiles = lp // tm
    proj = lambda s: jnp.minimum(s, n_tiles - 1)
    done = lambda s: jnp.maximum(s - 1, 0)
    in_row_spec = lambda width: pl.BlockSpec((1, tm, width), lambda b, s: (b, proj(s), 0))
    row_spec = lambda width: pl.BlockSpec((1, tm, width), lambda b, s: (b, done(s), 0))
    q_spec = pl.BlockSpec((1, MLA_HEADS, tm, LANES), lambda b, s: (b, 0, done(s), 0))
    kt_spec = pl.BlockSpec((1, MLA_HEADS, 1, LANES, tm), lambda b, s: (b, 0, done(s), 0, 0))
    table_spec = pl.BlockSpec((tm, LANES), lambda b, s: (done(s), 0))
    out_specs = [row_spec(POOL_WIDTH), q_spec, kt_spec, row_spec(MLA_HEADS * V_HEAD_DIM)]
    out_shape = [
        jax.ShapeDtypeStruct((batch, lp, POOL_WIDTH), jnp.bfloat16),
        jax.ShapeDtypeStruct((batch, MLA_HEADS, lp, LANES), jnp.bfloat16),
        jax.ShapeDtypeStruct((batch, MLA_HEADS, n_tiles, LANES, tm), jnp.bfloat16),
        jax.ShapeDtypeStruct((batch, lp, MLA_HEADS * V_HEAD_DIM), jnp.bfloat16),
    ]
    scratch = [pltpu.VMEM((tm + POOL_HALO, POOL_WIDTH - drop * POOL_GROUP_DIM), jnp.float32)
               for drop in (0, 0, 1, 2)]
    scratch += [pltpu.VMEM((tm, IN_WIDTH_PADDED), jnp.float32)] * 2
    assert len(scratch) == N_MIX_SCRATCH
    if first:
        x, meta = tokens_meta
        seq = x.shape[1]
        body = functools.partial(_mix_in_first_kernel, seq=seq, n_tiles=n_tiles)
        x_spec = pl.BlockSpec(
            (pl.Element(1), pl.Element(tm), pl.Element(d)),
            lambda b, s: (b, pl.multiple_of(jnp.clip(proj(s) * tm - N_META, 0, seq - tm),
                                            SUBLANES), 0))
        lead_specs = [x_spec, _const_spec(meta.shape)]
        lead_args = [x, meta]
        out_specs.append(in_row_spec(d))
        out_shape.append(jax.ShapeDtypeStruct((batch, lp, d), jnp.float32))
    else:
        body = functools.partial(_mix_in_kernel, n_tiles=n_tiles)
        lead_specs = [in_row_spec(d)]
        lead_args = [h]
    return pl.pallas_call(
        body,
        grid=(batch, n_tiles + 1),
        in_specs=lead_specs + [_const_spec(c.shape) for c in consts] + [table_spec, table_spec],
        out_specs=out_specs,
        out_shape=out_shape,
        scratch_shapes=scratch,
        compiler_params=pltpu.CompilerParams(
            dimension_semantics=("arbitrary", "arbitrary"),
            vmem_limit_bytes=VMEM_LIMIT_BYTES),
        name="mix_in",
    )(*lead_args, *consts, cos_t, sin_t)


def _attention(q, k_t, v):
    batch, heads, lp, _ = q.shape
    tq = SEQ_TILE
    hps = HEADS_PER_STEP
    n_chunks = k_t.shape[2]
    assert k_t.shape[4] == tq
    return pl.pallas_call(
        _attn_kernel,
        grid=(batch, heads // hps, lp // tq),
        in_specs=[
            pl.BlockSpec((1, hps, tq, LANES), lambda b, hp, i: (b, hp, i, 0)),
            pl.BlockSpec((1, hps, n_chunks, LANES, tq), lambda b, hp, i: (b, hp, 0, 0, 0)),
            pl.BlockSpec((1, lp, hps * V_HEAD_DIM), lambda b, hp, i: (b, 0, hp)),
        ],
        out_specs=pl.BlockSpec((1, tq, hps * V_HEAD_DIM), lambda b, hp, i: (b, i, hp)),
        out_shape=jax.ShapeDtypeStruct((batch, lp, heads * V_HEAD_DIM), jnp.bfloat16),
        scratch_shapes=[
            pltpu.VMEM((SCORE_BUFFERS, tq, tq), jnp.float32),
            pltpu.VMEM((hps, tq, LANES), jnp.float32),
            pltpu.VMEM((hps, tq, LANES), jnp.float32),
        ],
        compiler_params=pltpu.CompilerParams(
            dimension_semantics=("arbitrary", "arbitrary", "arbitrary"),
            vmem_limit_bytes=VMEM_LIMIT_BYTES),
        name="mla_attn",
    )(q, k_t, v)


def _layer_spec(stacked, layer):
    shape = (1,) + stacked.shape[1:]
    nd = len(shape)
    return pl.BlockSpec(shape, lambda *_: (layer,) + (0,) * (nd - 1),
                        pipeline_mode=pl.Buffered(1))


def _out_ffn(h2d, ypool2d, ymla2d, wo, ffn_g, wg, wu, wd, layer, real_rows=None):
    rows, d = h2d.shape
    per_call_rows = rows if real_rows is None else real_rows[3]
    tm = max(t for t in FFN_ROW_TILES if per_call_rows % t == 0)
    if real_rows is None:
        grid = (rows // tm,)
        in_row_spec = lambda width: pl.BlockSpec((tm, width), lambda i: (i, 0))
        out_spec = in_row_spec(d)
        out_rows = rows
    else:
        batch, lp, first, count = real_rows
        tiles = count // tm
        assert count % tm == 0 and first % 16 == 0
        grid = (batch, tiles)
        in_row_spec = lambda width: pl.BlockSpec(
            (pl.Element(tm), pl.Element(width)),
            lambda b, j: (pl.multiple_of(b * lp + first + j * tm, 16), 0))
        out_spec = pl.BlockSpec((tm, d), lambda b, j: (b * tiles + j, 0))
        out_rows = batch * count
    return pl.pallas_call(
        _out_ffn_kernel,
        grid=grid,
        in_specs=[
            in_row_spec(d), in_row_spec(POOL_WIDTH), in_row_spec(MLA_HEADS * V_HEAD_DIM),
            _layer_spec(wo, layer), _layer_spec(ffn_g, layer),
            _layer_spec(wg, layer), _layer_spec(wu, layer), _layer_spec(wd, layer),
        ],
        out_specs=out_spec,
        out_shape=jax.ShapeDtypeStruct((out_rows, d), jnp.float32),
        scratch_shapes=[pltpu.VMEM((tm, d), jnp.float32)],
        compiler_params=pltpu.CompilerParams(
            dimension_semantics=("arbitrary",) * len(grid),
            vmem_limit_bytes=VMEM_LIMIT_BYTES),
        name="out_ffn",
    )(h2d, ypool2d, ymla2d, wo, ffn_g, wg, wu, wd)


def _rope_tables(lp):
    inv = 1.0 / (ROPE_THETA ** (jnp.arange(0, QK_ROPE_DIM, 2, dtype=jnp.float32) / QK_ROPE_DIM))
    ang = jnp.arange(lp, dtype=jnp.float32)[:, None] * inv[None, :]
    cos, sin = jnp.cos(ang), jnp.sin(ang)
    zeros = lambda n: jnp.zeros((lp, n), jnp.float32)
    tail = LANES - QK_HEAD_DIM
    cos_t = jnp.concatenate([jnp.ones((lp, QK_NOPE_DIM), jnp.float32), cos, cos, zeros(tail)], axis=1)
    sin_t = jnp.concatenate([zeros(QK_NOPE_DIM), -sin, sin, zeros(tail)], axis=1)
    return cos_t, sin_t


def _swap_rope_halves(a):
    half = QK_ROPE_DIM // 2
    return jnp.concatenate([jnp.zeros_like(a[..., :QK_NOPE_DIM]),
                            a[..., QK_NOPE_DIM + half:], a[..., QK_NOPE_DIM:QK_NOPE_DIM + half]],
                           axis=-1)


def _pad_lanes(a, width):
    return jnp.pad(a, [(0, 0)] * (a.ndim - 1) + [(0, width - a.shape[-1])])


def kernel(x, meta_tokens, attn_norm_g, w_in, w_pool, pool_scale, q_a_norm_g, w_q_b,
           kv_a_norm_g, w_kv_b, q_norm_g, k_norm_g, w_out, ffn_norm_g, w_gate, w_up, w_down):
    batch, seq, d = x.shape
    depth = w_in.shape[0]
    length = N_META + seq
    lp = -(-length // SEQ_TILE) * SEQ_TILE
    assert (batch * lp) % min(FFN_ROW_TILES) == 0 and D_FF % FF_CHUNK == 0
    bf16 = jnp.bfloat16

    cos_t, sin_t = _rope_tables(lp)
    h = None
    wo_all, wg_all, wu_all, wd_all = (w.astype(bf16) for w in (w_out, w_gate, w_up, w_down))
    ffn_g_all = ffn_norm_g[:, None, :]

    s1 = POOL_WIDTH
    s3 = s1 + Q_LORA_RANK + KV_LORA_RANK
    for l in range(depth):
        rope_block = jnp.pad(w_in[l][:, s3:], ((0, 0), (QK_NOPE_DIM, LANES - QK_HEAD_DIM)))
        w_in_p = jnp.concatenate([w_in[l][:, :s3], rope_block], axis=1).astype(bf16)
        wq = w_q_b[l].reshape(Q_LORA_RANK, MLA_HEADS, QK_HEAD_DIM)
        wq_p = _pad_lanes(wq, LANES).reshape(Q_LORA_RANK, MLA_HEADS * LANES).astype(bf16)
        wqs_p = _pad_lanes(_swap_rope_halves(wq), LANES)
        wqs_p = wqs_p.reshape(Q_LORA_RANK, MLA_HEADS * LANES).astype(bf16)
        wkv = w_kv_b[l].reshape(KV_LORA_RANK, MLA_HEADS, QK_NOPE_DIM + V_HEAD_DIM)
        wk_t = wkv[:, :, :QK_NOPE_DIM].reshape(KV_LORA_RANK, MLA_HEADS * QK_NOPE_DIM).T.astype(bf16)
        wv = wkv[:, :, QK_NOPE_DIM:].reshape(KV_LORA_RANK, MLA_HEADS * V_HEAD_DIM).astype(bf16)
        gq = _pad_lanes(q_norm_g[l][None, :], LANES)
        gqs = _pad_lanes(_swap_rope_halves(q_norm_g[l][None, :]), LANES)
        gkn_t = jnp.broadcast_to(k_norm_g[l][:QK_NOPE_DIM, None], (QK_NOPE_DIM, SEQ_TILE))
        gkr = jnp.pad(k_norm_g[l][None, QK_NOPE_DIM:], ((0, 0), (QK_NOPE_DIM, LANES - QK_HEAD_DIM)))

        zero_blk = jnp.zeros((POOL_GROUP_DIM, POOL_GROUP_DIM), w_pool.dtype)
        w_pool2 = jnp.stack([jnp.block([[w_pool[l, 2 * p], zero_blk], [zero_blk, w_pool[l, 2 * p + 1]]])
                             for p in range(len(POOL_WINDOWS) // 2)]).astype(bf16)
        consts = (attn_norm_g[l][None, :], w_in_p, w_pool2, pool_scale[l][None, :],
                  q_a_norm_g[l][None, :], wq_p, wqs_p, kv_a_norm_g[l][None, :], wk_t, wv,
                  gq, gqs, gkn_t, gkr)
        if l == 0:
            ypool, q, k_t, v, h = _mix_in(None, consts, cos_t, sin_t, lp,
                                          tokens_meta=(x, meta_tokens.astype(x.dtype)))
        else:
            ypool, q, k_t, v = _mix_in(h, consts, cos_t, sin_t, lp)
        ymla = _attention(q, k_t, v)

        last = l == depth - 1
        h = _out_ffn(h.reshape(batch * lp, d), ypool.reshape(batch * lp, POOL_WIDTH),
                     ymla.reshape(batch * lp, MLA_HEADS * V_HEAD_DIM),
                     wo_all, ffn_g_all, wg_all, wu_all, wd_all, l,
                     real_rows=(batch, lp, N_META, seq) if last else None)
        h = h.reshape(batch, seq if last else lp, d)
    return h
```

```python
import functools

import jax
import jax.numpy as jnp
from jax import lax
from jax.experimental import pallas as pl
from jax.experimental.pallas import tpu as pltpu

D_MODEL = 1024
N_META = 16
POOL_WIDTH = 512
POOL_WINDOWS = (2, 4, 8, 16)
POOL_GROUP_DIM = 128
MLA_HEADS = 8
QK_NOPE_DIM = 64
QK_ROPE_DIM = 32
QK_HEAD_DIM = QK_NOPE_DIM + QK_ROPE_DIM
V_HEAD_DIM = 64
Q_LORA_RANK = 384
KV_LORA_RANK = 256
D_FF = 2816
ROPE_THETA = 10000.0
RMS_EPS = 1e-6

LANES = 128
SUBLANES = 8
assert POOL_WINDOWS == tuple(2 ** (k + 1) for k in range(len(POOL_WINDOWS)))
POOL_HALO = SUBLANES * len(POOL_WINDOWS)
assert POOL_HALO >= max(POOL_WINDOWS)

SEQ_TILE = 384
FFN_ROW_TILES = (512, 768, 1024)
FF_CHUNK = 256
FF_UNROLL = 5
HEADS_PER_STEP = 8
SCORE_LOOKAHEAD = 1
TAIL_LOOKAHEAD = 2
SCORE_BUFFERS = 4
assert SCORE_LOOKAHEAD <= TAIL_LOOKAHEAD < SCORE_BUFFERS and HEADS_PER_STEP % SCORE_BUFFERS == 0
IN_WIDTH_PADDED = 1280
VMEM_LIMIT_BYTES = 56 * 1024 * 1024

NEG_BIG = -1e30
LOG2_E = 1.4426950408889634


def _rms_scale(x, width):
    return lax.rsqrt(jnp.sum(x * x, axis=-1, keepdims=True) * (1.0 / width) + RMS_EPS)


def _rope(xn, cos_t, sin_lo, sin_hi):
    half = QK_ROPE_DIM // 2
    return (xn * cos_t
            + pltpu.roll(xn, LANES - half, 1) * sin_lo
            + pltpu.roll(xn, half, 1) * sin_hi)


N_MIX_SCRATCH = 6


def _mix_in_kernel(h_ref, *refs, n_tiles):
    _mix_in_steps(lambda: h_ref[0], refs)


def _mix_in_first_kernel(x_ref, meta_ref, *refs, seq, n_tiles):
    h_out_ref = refs[-N_MIX_SCRATCH - 1]
    refs = refs[:-N_MIX_SCRATCH - 1] + refs[-N_MIX_SCRATCH:]

    def load_x():
        i = jnp.minimum(pl.program_id(1), n_tiles - 1)
        blk = x_ref[0]
        tm = blk.shape[0]
        last_shift = (n_tiles - 1) * tm - N_META - (seq - tm)
        first_tile = jnp.concatenate([meta_ref[...], blk[:tm - N_META]], axis=0)
        last_tile = jnp.concatenate(
            [blk[last_shift:], jnp.zeros((last_shift, blk.shape[1]), blk.dtype)], axis=0)
        x = jnp.where(i == 0, first_tile, jnp.where(i == n_tiles - 1, last_tile, blk))
        h_out_ref[0] = x
        return x

    _mix_in_steps(load_x, refs)


def _mix_in_steps(load_x, refs):
    attn_g_ref, w_in_ref, *post_refs, za_ref, zb_ref = refs
    ue_ref = post_refs[-4]
    s = pl.program_id(1)

    @pl.when(s == 0)
    def _():
        zb_ref[...] = jnp.zeros(zb_ref.shape, zb_ref.dtype)

    @pl.when(s == 1)
    def _():
        ue_ref[0:POOL_HALO, :] = jnp.zeros((POOL_HALO, POOL_WIDTH), jnp.float32)

    def step(z_write, z_read):
        x = load_x()
        xn = (x * _rms_scale(x, D_MODEL) * attn_g_ref[0]).astype(jnp.bfloat16)
        z_write[...] = jnp.dot(xn, w_in_ref[0], preferred_element_type=jnp.float32)
        _mix_in_finish(z_read[...], s - 1, *post_refs)

    pl.when(s % 2 == 0)(lambda: step(za_ref, zb_ref))
    pl.when(s % 2 == 1)(lambda: step(zb_ref, za_ref))


def _mix_in_finish(z, i, w_pool_ref, pool_scale_ref,
                   qa_g_ref, wq_ref, wqs_ref, kva_g_ref, wkt_ref, wv_ref,
                   gq_ref, gqs_ref, gkn_ref, gkr_ref, cos_ref, sin_ref,
                   ypool_ref, q_ref, k_ref, v_ref, ue_ref, s2_ref, s4_ref, s8_ref):
    tm = z.shape[0]
    halo = POOL_HALO

    u = z[:, :POOL_WIDTH]

    gd = POOL_GROUP_DIM
    rows = tm + halo
    ue_ref[halo:, :] = u
    r1, r2, r3, r4 = (k * SUBLANES for k in (1, 2, 3, 4))
    s2_ref[r1:, :] = ue_ref[r1:, :] + ue_ref[r1 - 1:rows - 1, :]
    s4_ref[r2:, :] = s2_ref[r2:, gd:] + s2_ref[r2 - 2:rows - 2, gd:]
    s8_ref[r3:, :] = s4_ref[r3:, gd:] + s4_ref[r3 - 4:rows - 4, gd:]
    s16 = s8_ref[r4:, gd:] + s8_ref[r4 - 8:rows - 8, gd:]
    window_sums = (s2_ref[halo:, 0:gd], s4_ref[halo:, 0:gd], s8_ref[halo:, 0:gd], s16)

    pos = i * tm + lax.broadcasted_iota(jnp.int32, (tm, 1), 0)
    pooled = []
    for g, w in enumerate(POOL_WINDOWS):
        cnt = jnp.minimum(pos + 1, w).astype(jnp.float32)
        pooled.append((window_sums[g] / cnt - u[:, g * gd:(g + 1) * gd]).astype(jnp.bfloat16))
    for pair in range(len(POOL_WINDOWS) // 2):
        sl = slice(2 * pair * gd, 2 * (pair + 1) * gd)
        p2 = jnp.concatenate(pooled[2 * pair:2 * pair + 2], axis=1)
        y2 = jnp.dot(p2, w_pool_ref[0, pair], preferred_element_type=jnp.float32)
        ypool_ref[0, :, sl] = (y2 * pool_scale_ref[0, :, sl]).astype(jnp.bfloat16)
    ue_ref[0:halo, :] = ue_ref[tm:tm + halo, :]

    lane = lax.broadcasted_iota(jnp.int32, (1, LANES), 1)
    is_nope = lane < QK_NOPE_DIM
    rope_mid = QK_NOPE_DIM + QK_ROPE_DIM // 2
    cos_t = cos_ref[...]
    sin_t = sin_ref[...]
    scale = QK_HEAD_DIM ** -0.5 * LOG2_E

    c_q = z[:, POOL_WIDTH:POOL_WIDTH + Q_LORA_RANK]
    cqn = (c_q * _rms_scale(c_q, Q_LORA_RANK) * qa_g_ref[0]).astype(jnp.bfloat16)
    qf = jnp.dot(cqn, wq_ref[0], preferred_element_type=jnp.float32)
    qfs = jnp.dot(cqn, wqs_ref[0], preferred_element_type=jnp.float32)
    cos_g = cos_t * (gq_ref[0] * scale)
    sin_g = sin_t * (gqs_ref[0] * scale)

    for hd in range(MLA_HEADS):
        sl = slice(hd * LANES, (hd + 1) * LANES)
        qh = qf[:, sl]
        sq = qh * qh
        ss_n = jnp.sum(jnp.where(is_nope, sq, 0.0), axis=-1, keepdims=True)
        ss_r = jnp.sum(jnp.where(is_nope, 0.0, sq), axis=-1, keepdims=True)
        r_n = lax.rsqrt(ss_n * (1.0 / QK_NOPE_DIM) + RMS_EPS)
        r_r = lax.rsqrt(ss_r * (1.0 / QK_ROPE_DIM) + RMS_EPS)
        rot = qh * cos_g + qfs[:, sl] * sin_g
        q_ref[0, hd] = (rot * jnp.where(is_nope, r_n, r_r)).astype(jnp.bfloat16)

    kv0 = POOL_WIDTH + Q_LORA_RANK
    c_kv = z[:, kv0:kv0 + KV_LORA_RANK]
    ckvn = (c_kv * _rms_scale(c_kv, KV_LORA_RANK) * kva_g_ref[0]).astype(jnp.bfloat16)
    v_ref[0] = jnp.dot(ckvn, wv_ref[0],
                       preferred_element_type=jnp.float32).astype(jnp.bfloat16)

    kr = z[:, kv0 + KV_LORA_RANK:]
    krn = kr * _rms_scale(kr, QK_ROPE_DIM) * gkr_ref[0]
    kr_rot = _rope(krn, cos_t, jnp.where(lane < rope_mid, sin_t, 0.0),
                   jnp.where(lane < rope_mid, 0.0, sin_t))
    kr_rot_t = kr_rot.T[QK_NOPE_DIM:, :].astype(jnp.bfloat16)

    kf_t = lax.dot_general(wkt_ref[0], ckvn, (((1,), (1,)), ((), ())),
                           preferred_element_type=jnp.float32)
    for hd in range(MLA_HEADS):
        kh = kf_t[hd * QK_NOPE_DIM:(hd + 1) * QK_NOPE_DIM, :]
        r_k = lax.rsqrt(jnp.sum(kh * kh, axis=0, keepdims=True) * (1.0 / QK_NOPE_DIM) + RMS_EPS)
        k_ref[0, hd, 0, 0:QK_NOPE_DIM, :] = (kh * r_k * gkn_ref[0]).astype(jnp.bfloat16)
        k_ref[0, hd, 0, QK_NOPE_DIM:, :] = kr_rot_t


def _attn_kernel(q_ref, k_ref, v_ref, o_ref, s_scr, m_scr, acc_scr):
    i = pl.program_id(2)
    tq = q_ref.shape[2]
    tk = s_scr.shape[2]
    n_slabs = tk // LANES
    lane = lax.broadcasted_iota(jnp.int32, (1, LANES), 1)
    low_half = lane < V_HEAD_DIM

    n_buf = s_scr.shape[0]
    heads = HEADS_PER_STEP

    def scores(hh, j):
        s_scr[hh % n_buf] = jnp.dot(q_ref[0, hh], k_ref[0, hh, j],
                                    preferred_element_type=jnp.float32)

    def softmax_pv(hh, j, masked):
        rb = LANES
        if masked:
            tri = (lax.broadcasted_iota(jnp.int32, (rb, LANES), 1)
                   <= lax.broadcasted_iota(jnp.int32, (rb, LANES), 0))
        alphas, ps = [], []
        for rblk in range(tq // rb):
            r0 = rblk * rb
            blk = []
            for c in range(n_slabs if not masked else rblk + 1):
                sl = s_scr[hh % n_buf, r0:r0 + rb, c * LANES:(c + 1) * LANES]
                if masked and c == rblk:
                    sl = jnp.where(tri, sl, NEG_BIG)
                blk.append(sl)
            mx = blk[0]
            for sl in blk[1:]:
                mx = jnp.maximum(mx, sl)
            m_old = m_scr[hh, r0:r0 + rb]
            m_new = jnp.maximum(m_old, jnp.max(mx, axis=-1, keepdims=True))
            m_scr[hh, r0:r0 + rb] = m_new
            alphas.append(jnp.exp2(m_old - m_new))
            p_blk = [jnp.exp2(sl - m_new).astype(jnp.bfloat16) for sl in blk]
            p_blk += [jnp.zeros((rb, LANES), jnp.bfloat16)] * (n_slabs - len(blk))
            ps.append(jnp.concatenate(p_blk, axis=1))
        alpha = jnp.concatenate(alphas, axis=0)
        p = jnp.concatenate(ps, axis=0)
        start = pl.multiple_of(j * tk, tk)
        pair = hh // 2
        vj = v_ref[0, pl.ds(start, tk), pair * LANES:(pair + 1) * LANES]
        own = low_half if hh % 2 == 0 else jnp.logical_not(low_half)
        vj = jnp.where(own, vj, jnp.ones_like(vj))
        acc_scr[hh] = alpha * acc_scr[hh] + jnp.dot(p, vj, preferred_element_type=jnp.float32)

    m_scr[...] = jnp.full(m_scr.shape, NEG_BIG, jnp.float32)
    acc_scr[...] = jnp.zeros(acc_scr.shape, jnp.float32)
    ahead = SCORE_LOOKAHEAD
    for hh in range(ahead):
        scores(hh, 0)

    def chunk(j):
        for hh in range(heads):
            scores((hh + ahead) % heads, j + (hh + ahead) // heads)
            softmax_pv(hh, j, masked=False)

    def chunk_pair(jj, carry):
        chunk(2 * jj)
        chunk(2 * jj + 1)
        return carry

    lax.fori_loop(0, lax.shift_right_logical(i, 1), chunk_pair, 0)

    @pl.when(i % 2 == 1)
    def _():
        chunk(i - 1)

    tail_ahead = TAIL_LOOKAHEAD
    for hh in range(ahead, tail_ahead):
        scores(hh, i)
    def finalize(pair):
        acc_e = acc_scr[2 * pair]
        acc_o = acc_scr[2 * pair + 1]
        out_e = acc_e / acc_e[:, V_HEAD_DIM:V_HEAD_DIM + 1]
        out_o = acc_o / acc_o[:, 0:1]
        o_ref[0, :, pair * LANES:(pair + 1) * LANES] = jnp.where(
            low_half, out_e, out_o).astype(jnp.bfloat16)

    for hh in range(heads):
        if hh + tail_ahead < heads:
            scores(hh + tail_ahead, i)
        softmax_pv(hh, i, masked=True)
        if hh % 2 == 1:
            finalize(hh // 2)


def _out_ffn_kernel(h_ref, ypool_ref, ymla_ref, wo_ref, ffn_g_ref,
                    wg_ref, wu_ref, wd_ref, o_ref, acc_ref):
    h1 = (h_ref[...]
          + jnp.dot(ypool_ref[...], wo_ref[0, :POOL_WIDTH, :], preferred_element_type=jnp.float32)
          + jnp.dot(ymla_ref[...], wo_ref[0, POOL_WIDTH:, :], preferred_element_type=jnp.float32))
    g = (h1 * _rms_scale(h1, D_MODEL) * ffn_g_ref[0]).astype(jnp.bfloat16)
    acc_ref[...] = h1

    def chunk_slice(c):
        return pl.ds(pl.multiple_of(c * FF_CHUNK, FF_CHUNK), FF_CHUNK)

    def gate_up(c):
        cols = chunk_slice(c)
        gate = jnp.dot(g, wg_ref[0, :, cols], preferred_element_type=jnp.float32)
        up = jnp.dot(g, wu_ref[0, :, cols], preferred_element_type=jnp.float32)
        return (gate * (1.0 / (1.0 + jnp.exp(-gate))) * up).astype(jnp.bfloat16)

    def down(act, c):
        acc_ref[...] += jnp.dot(act, wd_ref[0, chunk_slice(c), :],
                                preferred_element_type=jnp.float32)

    n_chunks = wg_ref.shape[2] // FF_CHUNK

    def chunk_group(first, count):
        act = gate_up(first)
        for k in range(count):
            nxt = gate_up(first + k + 1) if k + 1 < count else None
            down(act, first + k)
            act = nxt

    def loop_step(step, carry):
        chunk_group(step * FF_UNROLL, FF_UNROLL)
        return carry

    n_steps = n_chunks // FF_UNROLL
    lax.fori_loop(0, n_steps, loop_step, 0)
    if n_chunks % FF_UNROLL:
        chunk_group(n_steps * FF_UNROLL, n_chunks % FF_UNROLL)
    o_ref[...] = acc_ref[...]


def _const_spec(shape):
    nd = len(shape)
    return pl.BlockSpec(shape, lambda *_: (0,) * nd)


def _layer_spec(stacked, layer):
    shape = (1,) + stacked.shape[1:]
    nd = len(shape)
    return pl.BlockSpec(shape, lambda *_: (layer,) + (0,) * (nd - 1),
                        pipeline_mode=pl.Buffered(1))


def _mix_in(h, consts, layer, cos_t, sin_t, lp, tokens_meta=None):
    first = tokens_meta is not None
    batch = tokens_meta[0].shape[0] if first else h.shape[0]
    d = D_MODEL
    tm = SEQ_TILE
    n_tiles = lp // tm
    proj = lambda s: jnp.minimum(s, n_tiles - 1)
    done = lambda s: jnp.maximum(s - 1, 0)
    in_row_spec = lambda width: pl.BlockSpec((1, tm, width), lambda b, s: (b, proj(s), 0))
    row_spec = lambda width: pl.BlockSpec((1, tm, width), lambda b, s: (b, done(s), 0))
    q_spec = pl.BlockSpec((1, MLA_HEADS, tm, LANES), lambda b, s: (b, 0, done(s), 0))
    kt_spec = pl.BlockSpec((1, MLA_HEADS, 1, LANES, tm), lambda b, s: (b, 0, done(s), 0, 0))
    table_spec = pl.BlockSpec((tm, LANES), lambda b, s: (done(s), 0))
    out_specs = [row_spec(POOL_WIDTH), q_spec, kt_spec, row_spec(MLA_HEADS * V_HEAD_DIM)]
    out_shape = [
        jax.ShapeDtypeStruct((batch, lp, POOL_WIDTH), jnp.bfloat16),
        jax.ShapeDtypeStruct((batch, MLA_HEADS, lp, LANES), jnp.bfloat16),
        jax.ShapeDtypeStruct((batch, MLA_HEADS, n_tiles, LANES, tm), jnp.bfloat16),
        jax.ShapeDtypeStruct((batch, lp, MLA_HEADS * V_HEAD_DIM), jnp.bfloat16),
    ]
    scratch = [pltpu.VMEM((tm + POOL_HALO, POOL_WIDTH - drop * POOL_GROUP_DIM), jnp.float32)
               for drop in (0, 0, 1, 2)]
    scratch += [pltpu.VMEM((tm, IN_WIDTH_PADDED), jnp.float32)] * 2
    assert len(scratch) == N_MIX_SCRATCH
    if first:
        x, meta = tokens_meta
        seq = x.shape[1]
        body = functools.partial(_mix_in_first_kernel, seq=seq, n_tiles=n_tiles)
        x_spec = pl.BlockSpec(
            (pl.Element(1), pl.Element(tm), pl.Element(d)),
            lambda b, s: (b, pl.multiple_of(jnp.clip(proj(s) * tm - N_META, 0, seq - tm),
                                            SUBLANES), 0))
        lead_specs = [x_spec, _const_spec(meta.shape)]
        lead_args = [x, meta]
        out_specs.append(in_row_spec(d))
        out_shape.append(jax.ShapeDtypeStruct((batch, lp, d), jnp.float32))
    else:
        body = functools.partial(_mix_in_kernel, n_tiles=n_tiles)
        lead_specs = [in_row_spec(d)]
        lead_args = [h]
    return pl.pallas_call(
        body,
        grid=(batch, n_tiles + 1),
        in_specs=lead_specs + [_layer_spec(c, layer) for c in consts] + [table_spec, table_spec],
        out_specs=out_specs,
        out_shape=out_shape,
        scratch_shapes=scratch,
        compiler_params=pltpu.CompilerParams(
            dimension_semantics=("arbitrary", "arbitrary"),
            vmem_limit_bytes=VMEM_LIMIT_BYTES),
        name="mix_in",
    )(*lead_args, *consts, cos_t, sin_t)


def _attention(q, k_t, v):
    batch, heads, lp, _ = q.shape
    tq = SEQ_TILE
    hps = HEADS_PER_STEP
    n_chunks = k_t.shape[2]
    assert k_t.shape[4] == tq
    return pl.pallas_call(
        _attn_kernel,
        grid=(batch, heads // hps, lp // tq),
        in_specs=[
            pl.BlockSpec((1, hps, tq, LANES), lambda b, hp, i: (b, hp, i, 0)),
            pl.BlockSpec((1, hps, n_chunks, LANES, tq), lambda b, hp, i: (b, hp, 0, 0, 0)),
            pl.BlockSpec((1, lp, hps * V_HEAD_DIM), lambda b, hp, i: (b, 0, hp)),
        ],
        out_specs=pl.BlockSpec((1, tq, hps * V_HEAD_DIM), lambda b, hp, i: (b, i, hp)),
        out_shape=jax.ShapeDtypeStruct((batch, lp, heads * V_HEAD_DIM), jnp.bfloat16),
        scratch_shapes=[
            pltpu.VMEM((SCORE_BUFFERS, tq, tq), jnp.float32),
            pltpu.VMEM((hps, tq, LANES), jnp.float32),
            pltpu.VMEM((hps, tq, LANES), jnp.float32),
        ],
        compiler_params=pltpu.CompilerParams(
            dimension_semantics=("arbitrary", "arbitrary", "arbitrary"),
            vmem_limit_bytes=VMEM_LIMIT_BYTES),
        name="mla_attn",
    )(q, k_t, v)


def _out_ffn(h2d, ypool2d, ymla2d, wo, ffn_g, wg, wu, wd, layer, real_rows=None):
    rows, d = h2d.shape
    per_call_rows = rows if real_rows is None else real_rows[3]
    tm = max(t for t in FFN_ROW_TILES if per_call_rows % t == 0)
    if real_rows is None:
        grid = (rows // tm,)
        in_row_spec = lambda width: pl.BlockSpec((tm, width), lambda i: (i, 0))
        out_spec = in_row_spec(d)
        out_rows = rows
    else:
        batch, lp, first, count = real_rows
        tiles = count // tm
        assert count % tm == 0 and first % 16 == 0
        grid = (batch, tiles)
        in_row_spec = lambda width: pl.BlockSpec(
            (pl.Element(tm), pl.Element(width)),
            lambda b, j: (pl.multiple_of(b * lp + first + j * tm, 16), 0))
        out_spec = pl.BlockSpec((tm, d), lambda b, j: (b * tiles + j, 0))
        out_rows = batch * count
    return pl.pallas_call(
        _out_ffn_kernel,
        grid=grid,
        in_specs=[
            in_row_spec(d), in_row_spec(POOL_WIDTH), in_row_spec(MLA_HEADS * V_HEAD_DIM),
            _layer_spec(wo, layer), _layer_spec(ffn_g, layer),
            _layer_spec(wg, layer), _layer_spec(wu, layer), _layer_spec(wd, layer),
        ],
        out_specs=out_spec,
        out_shape=jax.ShapeDtypeStruct((out_rows, d), jnp.float32),
        scratch_shapes=[pltpu.VMEM((tm, d), jnp.float32)],
        compiler_params=pltpu.CompilerParams(
            dimension_semantics=("arbitrary",) * len(grid),
            vmem_limit_bytes=VMEM_LIMIT_BYTES),
        name="out_ffn",
    )(h2d, ypool2d, ymla2d, wo, ffn_g, wg, wu, wd)


def _rope_tables(lp):
    inv = 1.0 / (ROPE_THETA ** (jnp.arange(0, QK_ROPE_DIM, 2, dtype=jnp.float32) / QK_ROPE_DIM))
    ang = jnp.arange(lp, dtype=jnp.float32)[:, None] * inv[None, :]
    cos, sin = jnp.cos(ang), jnp.sin(ang)
    zeros = lambda n: jnp.zeros((lp, n), jnp.float32)
    tail = LANES - QK_HEAD_DIM
    cos_t = jnp.concatenate([jnp.ones((lp, QK_NOPE_DIM), jnp.float32), cos, cos, zeros(tail)], axis=1)
    sin_t = jnp.concatenate([zeros(QK_NOPE_DIM), -sin, sin, zeros(tail)], axis=1)
    return cos_t, sin_t


def _swap_rope_halves(a):
    half = QK_ROPE_DIM // 2
    return jnp.concatenate([jnp.zeros_like(a[..., :QK_NOPE_DIM]),
                            a[..., QK_NOPE_DIM + half:], a[..., QK_NOPE_DIM:QK_NOPE_DIM + half]],
                           axis=-1)


def _pad_lanes(a, width):
    return jnp.pad(a, [(0, 0)] * (a.ndim - 1) + [(0, width - a.shape[-1])])


def kernel(x, meta_tokens, attn_norm_g, w_in, w_pool, pool_scale, q_a_norm_g, w_q_b,
           kv_a_norm_g, w_kv_b, q_norm_g, k_norm_g, w_out, ffn_norm_g, w_gate, w_up, w_down):
    batch, seq, d = x.shape
    depth = w_in.shape[0]
    length = N_META + seq
    lp = -(-length // SEQ_TILE) * SEQ_TILE
    assert (batch * lp) % min(FFN_ROW_TILES) == 0 and D_FF % FF_CHUNK == 0
    bf16 = jnp.bfloat16

    cos_t, sin_t = _rope_tables(lp)
    h = None
    wo_all, wg_all, wu_all, wd_all = (w.astype(bf16) for w in (w_out, w_gate, w_up, w_down))
    ffn_g_all = ffn_norm_g[:, None, :]

    s3 = POOL_WIDTH + Q_LORA_RANK + KV_LORA_RANK
    rope_block = jnp.pad(w_in[:, :, s3:], ((0, 0), (0, 0), (QK_NOPE_DIM, LANES - QK_HEAD_DIM)))
    w_in_p = jnp.concatenate([w_in[:, :, :s3], rope_block], axis=2).astype(bf16)
    wq = w_q_b.reshape(depth, Q_LORA_RANK, MLA_HEADS, QK_HEAD_DIM)
    wq_p = _pad_lanes(wq, LANES).reshape(depth, Q_LORA_RANK, MLA_HEADS * LANES).astype(bf16)
    wqs_p = _pad_lanes(_swap_rope_halves(wq), LANES)
    wqs_p = wqs_p.reshape(depth, Q_LORA_RANK, MLA_HEADS * LANES).astype(bf16)
    wkv = w_kv_b.reshape(depth, KV_LORA_RANK, MLA_HEADS, QK_NOPE_DIM + V_HEAD_DIM)
    wk_t = wkv[..., :QK_NOPE_DIM].reshape(depth, KV_LORA_RANK, MLA_HEADS * QK_NOPE_DIM)
    wk_t = wk_t.transpose(0, 2, 1).astype(bf16)
    wv = wkv[..., QK_NOPE_DIM:].reshape(depth, KV_LORA_RANK, MLA_HEADS * V_HEAD_DIM).astype(bf16)
    gq = _pad_lanes(q_norm_g[:, None, :], LANES)
    gqs = _pad_lanes(_swap_rope_halves(q_norm_g[:, None, :]), LANES)
    gkn_t = jnp.broadcast_to(k_norm_g[:, :QK_NOPE_DIM, None], (depth, QK_NOPE_DIM, SEQ_TILE))
    gkr = jnp.pad(k_norm_g[:, None, QK_NOPE_DIM:],
                  ((0, 0), (0, 0), (QK_NOPE_DIM, LANES - QK_HEAD_DIM)))
    n_pairs = len(POOL_WINDOWS) // 2
    wp = w_pool.reshape(depth, n_pairs, 2, POOL_GROUP_DIM, POOL_GROUP_DIM)
    zero_blk = jnp.zeros_like(wp[:, :, 0])
    w_pool2 = jnp.concatenate([jnp.concatenate([wp[:, :, 0], zero_blk], axis=-1),
                               jnp.concatenate([zero_blk, wp[:, :, 1]], axis=-1)],
                              axis=-2).astype(bf16)
    consts = (attn_norm_g[:, None, :], w_in_p, w_pool2, pool_scale[:, None, :],
              q_a_norm_g[:, None, :], wq_p, wqs_p, kv_a_norm_g[:, None, :], wk_t, wv,
              gq, gqs, gkn_t, gkr)

    for l in range(depth):
        if l == 0:
            ypool, q, k_t, v, h = _mix_in(None, consts, l, cos_t, sin_t, lp,
                                          tokens_meta=(x, meta_tokens.astype(x.dtype)))
        else:
            ypool, q, k_t, v = _mix_in(h, consts, l, cos_t, sin_t, lp)
        ymla = _attention(q, k_t, v)

        last = l == depth - 1
        h = _out_ffn(h.reshape(batch * lp, d), ypool.reshape(batch * lp, POOL_WIDTH),
                     ymla.reshape(batch * lp, MLA_HEADS * V_HEAD_DIM),
                     wo_all, ffn_g_all, wg_all, wu_all, wd_all, l,
                     real_rows=(batch, lp, N_META, seq) if last else None)
        h = h.reshape(batch, seq if last else lp, d)
    return h
```

```python
import functools

import jax
import jax.numpy as jnp
from jax import lax
from jax.experimental import pallas as pl
from jax.experimental.pallas import tpu as pltpu

D_MODEL = 1024
N_META = 16
POOL_WIDTH = 512
POOL_WINDOWS = (2, 4, 8, 16)
POOL_GROUP_DIM = 128
MLA_HEADS = 8
QK_NOPE_DIM = 64
QK_ROPE_DIM = 32
QK_HEAD_DIM = QK_NOPE_DIM + QK_ROPE_DIM
V_HEAD_DIM = 64
Q_LORA_RANK = 384
KV_LORA_RANK = 256
D_FF = 2816
ROPE_THETA = 10000.0
RMS_EPS = 1e-6

LANES = 128
SUBLANES = 8
BF16_ROWS = 2 * SUBLANES
assert POOL_WINDOWS == tuple(2 ** (k + 1) for k in range(len(POOL_WINDOWS)))
POOL_HALO = SUBLANES * len(POOL_WINDOWS)
assert POOL_HALO >= max(POOL_WINDOWS)

SEQ_TILE = 384
FFN_ROW_TILES = (512, 768, 1024)
FF_CHUNK = 256
FF_UNROLL = 5
HEADS_PER_STEP = 8
SCORE_LOOKAHEAD = 1
TAIL_LOOKAHEAD = 2
SCORE_BUFFERS = 4
assert SCORE_LOOKAHEAD <= TAIL_LOOKAHEAD < SCORE_BUFFERS and HEADS_PER_STEP % SCORE_BUFFERS == 0
IN_WIDTH_PADDED = 1280
VMEM_LIMIT_BYTES = 56 * 1024 * 1024

NEG_BIG = -1e30
LOG2_E = 1.4426950408889634


def _rms_scale(x, width):
    return lax.rsqrt(jnp.sum(x * x, axis=-1, keepdims=True) * (1.0 / width) + RMS_EPS)


def _rope(xn, cos_t, sin_lo, sin_hi):
    half = QK_ROPE_DIM // 2
    return (xn * cos_t
            + pltpu.roll(xn, LANES - half, 1) * sin_lo
            + pltpu.roll(xn, half, 1) * sin_hi)


N_MIX_SCRATCH = 6


def _mix_in_kernel(h_ref, *refs, n_tiles):
    _mix_in_steps(lambda: h_ref[0], refs, n_tiles)


def _mix_in_first_kernel(x_ref, meta_ref, *refs, seq, n_tiles):
    h_out_ref = refs[-N_MIX_SCRATCH - 1]
    refs = refs[:-N_MIX_SCRATCH - 1] + refs[-N_MIX_SCRATCH:]

    def load_x():
        i = jnp.minimum(pl.program_id(1), n_tiles - 1)
        blk = x_ref[0]
        tm = blk.shape[0]
        last_shift = (n_tiles - 1) * tm - N_META - (seq - tm)
        first_tile = jnp.concatenate([meta_ref[...], blk[:tm - N_META]], axis=0)
        last_tile = jnp.concatenate(
            [blk[last_shift:], jnp.zeros((last_shift, blk.shape[1]), blk.dtype)], axis=0)
        x = jnp.where(i == 0, first_tile, jnp.where(i == n_tiles - 1, last_tile, blk))
        h_out_ref[0] = x
        return x

    _mix_in_steps(load_x, refs, n_tiles)


def _mix_in_steps(load_x, refs, n_tiles):
    attn_g_ref, w_in_ref, *post_refs, za_ref, zb_ref = refs
    ue_ref = post_refs[-4]
    s = pl.program_id(1)
    even = s % 2 == 0

    @pl.when(s == 1)
    def _():
        ue_ref[0:POOL_HALO, :] = jnp.zeros((POOL_HALO, POOL_WIDTH), jnp.float32)

    def project(z_write):
        x = load_x()
        xn = (x * _rms_scale(x, D_MODEL) * attn_g_ref[0]).astype(jnp.bfloat16)
        z_write[...] = jnp.dot(xn, w_in_ref[0], preferred_element_type=jnp.float32)

    def finish(z_read):
        _mix_in_finish(z_read[...], s - 1, *post_refs)

    def both(z_write, z_read):
        project(z_write)
        finish(z_read)

    inner = jnp.logical_and(s > 0, s < n_tiles)
    pl.when(s == 0)(lambda: project(za_ref))
    pl.when(jnp.logical_and(inner, even))(lambda: both(za_ref, zb_ref))
    pl.when(jnp.logical_and(inner, jnp.logical_not(even)))(lambda: both(zb_ref, za_ref))
    pl.when(s == n_tiles)(lambda: finish(zb_ref if n_tiles % 2 == 0 else za_ref))


def _mix_in_finish(z, i, w_pool_ref, pool_scale_ref,
                   qa_g_ref, wq_ref, wqs_ref, kva_g_ref, wkt_ref, wv_ref,
                   gq_ref, gqs_ref, gkn_ref, gkr_ref, cos_ref, sin_ref,
                   ypool_ref, q_ref, k_ref, v_ref, ue_ref, s2_ref, s4_ref, s8_ref):
    tm = z.shape[0]
    halo = POOL_HALO

    u = z[:, :POOL_WIDTH]

    gd = POOL_GROUP_DIM
    rows = tm + halo
    ue_ref[halo:, :] = u
    r1, r2, r3, r4 = (k * SUBLANES for k in (1, 2, 3, 4))
    s2_ref[r1:, :] = ue_ref[r1:, :] + ue_ref[r1 - 1:rows - 1, :]
    s4_ref[r2:, :] = s2_ref[r2:, gd:] + s2_ref[r2 - 2:rows - 2, gd:]
    s8_ref[r3:, :] = s4_ref[r3:, gd:] + s4_ref[r3 - 4:rows - 4, gd:]
    s16 = s8_ref[r4:, gd:] + s8_ref[r4 - 8:rows - 8, gd:]
    window_sums = (s2_ref[halo:, 0:gd], s4_ref[halo:, 0:gd], s8_ref[halo:, 0:gd], s16)

    pos = i * tm + lax.broadcasted_iota(jnp.int32, (tm, 1), 0)
    pooled = []
    for g, w in enumerate(POOL_WINDOWS):
        cnt = jnp.minimum(pos + 1, w).astype(jnp.float32)
        pooled.append((window_sums[g] / cnt - u[:, g * gd:(g + 1) * gd]).astype(jnp.bfloat16))
    for pair in range(len(POOL_WINDOWS) // 2):
        sl = slice(2 * pair * gd, 2 * (pair + 1) * gd)
        p2 = jnp.concatenate(pooled[2 * pair:2 * pair + 2], axis=1)
        y2 = jnp.dot(p2, w_pool_ref[0, pair], preferred_element_type=jnp.float32)
        ypool_ref[0, :, sl] = (y2 * pool_scale_ref[0, :, sl]).astype(jnp.bfloat16)
    ue_ref[0:halo, :] = ue_ref[tm:tm + halo, :]

    lane = lax.broadcasted_iota(jnp.int32, (1, LANES), 1)
    is_nope = lane < QK_NOPE_DIM
    rope_mid = QK_NOPE_DIM + QK_ROPE_DIM // 2
    cos_t = cos_ref[...]
    sin_t = sin_ref[...]
    scale = QK_HEAD_DIM ** -0.5 * LOG2_E

    c_q = z[:, POOL_WIDTH:POOL_WIDTH + Q_LORA_RANK]
    cqn = (c_q * _rms_scale(c_q, Q_LORA_RANK) * qa_g_ref[0]).astype(jnp.bfloat16)
    qf = jnp.dot(cqn, wq_ref[0], preferred_element_type=jnp.float32)
    qfs = jnp.dot(cqn, wqs_ref[0], preferred_element_type=jnp.float32)
    cos_g = cos_t * (gq_ref[0] * scale)
    sin_g = sin_t * (gqs_ref[0] * scale)

    for hd in range(MLA_HEADS):
        sl = slice(hd * LANES, (hd + 1) * LANES)
        qh = qf[:, sl]
        sq = qh * qh
        ss_n = jnp.sum(jnp.where(is_nope, sq, 0.0), axis=-1, keepdims=True)
        ss_r = jnp.sum(jnp.where(is_nope, 0.0, sq), axis=-1, keepdims=True)
        r_n = lax.rsqrt(ss_n * (1.0 / QK_NOPE_DIM) + RMS_EPS)
        r_r = lax.rsqrt(ss_r * (1.0 / QK_ROPE_DIM) + RMS_EPS)
        rot = qh * cos_g + qfs[:, sl] * sin_g
        q_ref[0, hd] = (rot * jnp.where(is_nope, r_n, r_r)).astype(jnp.bfloat16)

    kv0 = POOL_WIDTH + Q_LORA_RANK
    c_kv = z[:, kv0:kv0 + KV_LORA_RANK]
    ckvn = (c_kv * _rms_scale(c_kv, KV_LORA_RANK) * kva_g_ref[0]).astype(jnp.bfloat16)
    v_ref[0] = jnp.dot(ckvn, wv_ref[0],
                       preferred_element_type=jnp.float32).astype(jnp.bfloat16)

    kr = z[:, kv0 + KV_LORA_RANK:]
    krn = kr * _rms_scale(kr, QK_ROPE_DIM) * gkr_ref[0]
    kr_rot = _rope(krn, cos_t, jnp.where(lane < rope_mid, sin_t, 0.0),
                   jnp.where(lane < rope_mid, 0.0, sin_t))
    kr_rot_t = kr_rot.T[QK_NOPE_DIM:, :].astype(jnp.bfloat16)

    kf_t = lax.dot_general(wkt_ref[0], ckvn, (((1,), (1,)), ((), ())),
                           preferred_element_type=jnp.float32)
    for hd in range(MLA_HEADS):
        kh = kf_t[hd * QK_NOPE_DIM:(hd + 1) * QK_NOPE_DIM, :]
        r_k = lax.rsqrt(jnp.sum(kh * kh, axis=0, keepdims=True) * (1.0 / QK_NOPE_DIM) + RMS_EPS)
        k_ref[0, hd, 0, 0:QK_NOPE_DIM, :] = (kh * r_k * gkn_ref[0]).astype(jnp.bfloat16)
        k_ref[0, hd, 0, QK_NOPE_DIM:, :] = kr_rot_t


def _attn_kernel(q_ref, k_ref, v_ref, o_ref, s_scr, m_scr, acc_scr):
    i = pl.program_id(2)
    tq = q_ref.shape[2]
    tk = s_scr.shape[2]
    n_slabs = tk // LANES
    lane = lax.broadcasted_iota(jnp.int32, (1, LANES), 1)
    low_half = lane < V_HEAD_DIM

    n_buf = s_scr.shape[0]
    heads = HEADS_PER_STEP

    def scores(hh, j):
        s_scr[hh % n_buf] = jnp.dot(q_ref[0, hh], k_ref[0, hh, j],
                                    preferred_element_type=jnp.float32)

    def softmax_pv(hh, j, masked):
        rb = LANES
        if masked:
            tri = (lax.broadcasted_iota(jnp.int32, (rb, LANES), 1)
                   <= lax.broadcasted_iota(jnp.int32, (rb, LANES), 0))
        alphas, ps = [], []
        for rblk in range(tq // rb):
            r0 = rblk * rb
            blk = []
            for c in range(n_slabs if not masked else rblk + 1):
                sl = s_scr[hh % n_buf, r0:r0 + rb, c * LANES:(c + 1) * LANES]
                if masked and c == rblk:
                    sl = jnp.where(tri, sl, NEG_BIG)
                blk.append(sl)
            mx = blk[0]
            for sl in blk[1:]:
                mx = jnp.maximum(mx, sl)
            m_old = m_scr[hh, r0:r0 + rb]
            m_new = jnp.maximum(m_old, jnp.max(mx, axis=-1, keepdims=True))
            m_scr[hh, r0:r0 + rb] = m_new
            alphas.append(jnp.exp2(m_old - m_new))
            p_blk = [jnp.exp2(sl - m_new).astype(jnp.bfloat16) for sl in blk]
            p_blk += [jnp.zeros((rb, LANES), jnp.bfloat16)] * (n_slabs - len(blk))
            ps.append(jnp.concatenate(p_blk, axis=1))
        alpha = jnp.concatenate(alphas, axis=0)
        p = jnp.concatenate(ps, axis=0)
        start = pl.multiple_of(j * tk, tk)
        pair = hh // 2
        vj = v_ref[0, pl.ds(start, tk), pair * LANES:(pair + 1) * LANES]
        own = low_half if hh % 2 == 0 else jnp.logical_not(low_half)
        vj = jnp.where(own, vj, jnp.ones_like(vj))
        acc_scr[hh] = alpha * acc_scr[hh] + jnp.dot(p, vj, preferred_element_type=jnp.float32)

    m_scr[...] = jnp.full(m_scr.shape, NEG_BIG, jnp.float32)
    acc_scr[...] = jnp.zeros(acc_scr.shape, jnp.float32)
    ahead = SCORE_LOOKAHEAD
    for hh in range(ahead):
        scores(hh, 0)

    def chunk(j):
        for hh in range(heads):
            scores((hh + ahead) % heads, j + (hh + ahead) // heads)
            softmax_pv(hh, j, masked=False)

    def chunk_pair(jj, carry):
        chunk(2 * jj)
        chunk(2 * jj + 1)
        return carry

    lax.fori_loop(0, lax.shift_right_logical(i, 1), chunk_pair, 0)

    @pl.when(i % 2 == 1)
    def _():
        chunk(i - 1)

    tail_ahead = TAIL_LOOKAHEAD
    for hh in range(ahead, tail_ahead):
        scores(hh, i)
    def finalize(pair):
        acc_e = acc_scr[2 * pair]
        acc_o = acc_scr[2 * pair + 1]
        out_e = acc_e / acc_e[:, V_HEAD_DIM:V_HEAD_DIM + 1]
        out_o = acc_o / acc_o[:, 0:1]
        o_ref[0, :, pair * LANES:(pair + 1) * LANES] = jnp.where(
            low_half, out_e, out_o).astype(jnp.bfloat16)

    for hh in range(heads):
        if hh + tail_ahead < heads:
            scores(hh + tail_ahead, i)
        softmax_pv(hh, i, masked=True)
        if hh % 2 == 1:
            finalize(hh // 2)


def _out_ffn_kernel(h_ref, ypool_ref, ymla_ref, wo_ref, ffn_g_ref,
                    wg_ref, wu_ref, wd_ref, o_ref, acc_ref):
    h1 = (h_ref[...]
          + jnp.dot(ypool_ref[...], wo_ref[0, :POOL_WIDTH, :], preferred_element_type=jnp.float32)
          + jnp.dot(ymla_ref[...], wo_ref[0, POOL_WIDTH:, :], preferred_element_type=jnp.float32))
    g = (h1 * _rms_scale(h1, D_MODEL) * ffn_g_ref[0]).astype(jnp.bfloat16)
    acc_ref[...] = h1

    def chunk_slice(c):
        return pl.ds(pl.multiple_of(c * FF_CHUNK, FF_CHUNK), FF_CHUNK)

    def gate_up(c):
        cols = chunk_slice(c)
        gate = jnp.dot(g, wg_ref[0, :, cols], preferred_element_type=jnp.float32)
        up = jnp.dot(g, wu_ref[0, :, cols], preferred_element_type=jnp.float32)
        return (gate * (1.0 / (1.0 + jnp.exp(-gate))) * up).astype(jnp.bfloat16)

    def down(act, c):
        acc_ref[...] += jnp.dot(act, wd_ref[0, chunk_slice(c), :],
                                preferred_element_type=jnp.float32)

    n_chunks = wg_ref.shape[2] // FF_CHUNK

    def chunk_group(first, count):
        act = gate_up(first)
        for k in range(count):
            nxt = gate_up(first + k + 1) if k + 1 < count else None
            down(act, first + k)
            act = nxt

    def loop_step(step, carry):
        chunk_group(step * FF_UNROLL, FF_UNROLL)
        return carry

    n_steps = n_chunks // FF_UNROLL
    lax.fori_loop(0, n_steps, loop_step, 0)
    if n_chunks % FF_UNROLL:
        chunk_group(n_steps * FF_UNROLL, n_chunks % FF_UNROLL)
    o_ref[...] = acc_ref[...]


def _const_spec(shape):
    nd = len(shape)
    return pl.BlockSpec(shape, lambda *_: (0,) * nd)


def _layer_spec(stacked, layer):
    shape = (1,) + stacked.shape[1:]
    nd = len(shape)
    return pl.BlockSpec(shape, lambda *_: (layer,) + (0,) * (nd - 1),
                        pipeline_mode=pl.Buffered(1))


def _mix_in(h, consts, layer, cos_t, sin_t, lp, tokens_meta=None):
    first = tokens_meta is not None
    batch = tokens_meta[0].shape[0] if first else h.shape[0]
    d = D_MODEL
    tm = SEQ_TILE
    n_tiles = lp // tm
    proj = lambda s: jnp.minimum(s, n_tiles - 1)
    done = lambda s: jnp.maximum(s - 1, 0)
    in_row_spec = lambda width: pl.BlockSpec((1, tm, width), lambda b, s: (b, proj(s), 0))
    row_spec = lambda width: pl.BlockSpec((1, tm, width), lambda b, s: (b, done(s), 0))
    q_spec = pl.BlockSpec((1, MLA_HEADS, tm, LANES), lambda b, s: (b, 0, done(s), 0))
    kt_spec = pl.BlockSpec((1, MLA_HEADS, 1, LANES, tm), lambda b, s: (b, 0, done(s), 0, 0))
    table_spec = pl.BlockSpec((tm, LANES), lambda b, s: (done(s), 0))
    out_specs = [row_spec(POOL_WIDTH), q_spec, kt_spec, row_spec(MLA_HEADS * V_HEAD_DIM)]
    out_shape = [
        jax.ShapeDtypeStruct((batch, lp, POOL_WIDTH), jnp.bfloat16),
        jax.ShapeDtypeStruct((batch, MLA_HEADS, lp, LANES), jnp.bfloat16),
        jax.ShapeDtypeStruct((batch, MLA_HEADS, n_tiles, LANES, tm), jnp.bfloat16),
        jax.ShapeDtypeStruct((batch, lp, MLA_HEADS * V_HEAD_DIM), jnp.bfloat16),
    ]
    scratch = [pltpu.VMEM((tm + POOL_HALO, POOL_WIDTH - drop * POOL_GROUP_DIM), jnp.float32)
               for drop in (0, 0, 1, 2)]
    scratch += [pltpu.VMEM((tm, IN_WIDTH_PADDED), jnp.float32)] * 2
    assert len(scratch) == N_MIX_SCRATCH
    if first:
        x, meta = tokens_meta
        seq = x.shape[1]
        body = functools.partial(_mix_in_first_kernel, seq=seq, n_tiles=n_tiles)
        x_spec = pl.BlockSpec(
            (pl.Element(1), pl.Element(tm), pl.Element(d)),
            lambda b, s: (b, pl.multiple_of(jnp.clip(proj(s) * tm - N_META, 0, seq - tm),
                                            SUBLANES), 0))
        lead_specs = [x_spec, _const_spec(meta.shape)]
        lead_args = [x, meta]
        out_specs.append(in_row_spec(d))
        out_shape.append(jax.ShapeDtypeStruct((batch, lp, d), jnp.float32))
    else:
        body = functools.partial(_mix_in_kernel, n_tiles=n_tiles)
        lead_specs = [in_row_spec(d)]
        lead_args = [h]
    return pl.pallas_call(
        body,
        grid=(batch, n_tiles + 1),
        in_specs=lead_specs + [_layer_spec(c, layer) for c in consts] + [table_spec, table_spec],
        out_specs=out_specs,
        out_shape=out_shape,
        scratch_shapes=scratch,
        compiler_params=pltpu.CompilerParams(
            dimension_semantics=("arbitrary", "arbitrary"),
            vmem_limit_bytes=VMEM_LIMIT_BYTES),
        name="mix_in",
    )(*lead_args, *consts, cos_t, sin_t)


def _attention(q, k_t, v):
    batch, heads, lp, _ = q.shape
    tq = SEQ_TILE
    hps = HEADS_PER_STEP
    n_chunks = k_t.shape[2]
    assert k_t.shape[4] == tq
    return pl.pallas_call(
        _attn_kernel,
        grid=(batch, heads // hps, lp // tq),
        in_specs=[
            pl.BlockSpec((1, hps, tq, LANES), lambda b, hp, i: (b, hp, i, 0)),
            pl.BlockSpec((1, hps, n_chunks, LANES, tq), lambda b, hp, i: (b, hp, 0, 0, 0)),
            pl.BlockSpec((1, lp, hps * V_HEAD_DIM), lambda b, hp, i: (b, 0, hp)),
        ],
        out_specs=pl.BlockSpec((1, tq, hps * V_HEAD_DIM), lambda b, hp, i: (b, i, hp)),
        out_shape=jax.ShapeDtypeStruct((batch, lp, heads * V_HEAD_DIM), jnp.bfloat16),
        scratch_shapes=[
            pltpu.VMEM((SCORE_BUFFERS, tq, tq), jnp.float32),
            pltpu.VMEM((hps, tq, LANES), jnp.float32),
            pltpu.VMEM((hps, tq, LANES), jnp.float32),
        ],
        compiler_params=pltpu.CompilerParams(
            dimension_semantics=("arbitrary", "arbitrary", "arbitrary"),
            vmem_limit_bytes=VMEM_LIMIT_BYTES),
        name="mla_attn",
    )(q, k_t, v)


def _out_ffn(h2d, ypool2d, ymla2d, wo, ffn_g, wg, wu, wd, layer, real_rows=None):
    rows, d = h2d.shape
    per_call_rows = rows if real_rows is None else real_rows[3]
    tm = max(t for t in FFN_ROW_TILES if per_call_rows % t == 0)
    if real_rows is None:
        grid = (rows // tm,)
        in_row_spec = lambda width: pl.BlockSpec((tm, width), lambda i: (i, 0))
        out_spec = in_row_spec(d)
        out_rows = rows
    else:
        batch, lp, first, count = real_rows
        tiles = count // tm
        assert count % tm == 0 and first % BF16_ROWS == 0 and lp % BF16_ROWS == 0
        grid = (batch, tiles)
        in_row_spec = lambda width: pl.BlockSpec(
            (pl.Element(tm), pl.Element(width)),
            lambda b, j: (pl.multiple_of(b * lp + first + j * tm, BF16_ROWS), 0))
        out_spec = pl.BlockSpec((tm, d), lambda b, j: (b * tiles + j, 0))
        out_rows = batch * count
    return pl.pallas_call(
        _out_ffn_kernel,
        grid=grid,
        in_specs=[
            in_row_spec(d), in_row_spec(POOL_WIDTH), in_row_spec(MLA_HEADS * V_HEAD_DIM),
            _layer_spec(wo, layer), _layer_spec(ffn_g, layer),
            _layer_spec(wg, layer), _layer_spec(wu, layer), _layer_spec(wd, layer),
        ],
        out_specs=out_spec,
        out_shape=jax.ShapeDtypeStruct((out_rows, d), jnp.float32),
        scratch_shapes=[pltpu.VMEM((tm, d), jnp.float32)],
        compiler_params=pltpu.CompilerParams(
            dimension_semantics=("arbitrary",) * len(grid),
            vmem_limit_bytes=VMEM_LIMIT_BYTES),
        name="out_ffn",
    )(h2d, ypool2d, ymla2d, wo, ffn_g, wg, wu, wd)


def _rope_tables(lp):
    inv = 1.0 / (ROPE_THETA ** (jnp.arange(0, QK_ROPE_DIM, 2, dtype=jnp.float32) / QK_ROPE_DIM))
    ang = jnp.arange(lp, dtype=jnp.float32)[:, None] * inv[None, :]
    cos, sin = jnp.cos(ang), jnp.sin(ang)
    zeros = lambda n: jnp.zeros((lp, n), jnp.float32)
    tail = LANES - QK_HEAD_DIM
    cos_t = jnp.concatenate([jnp.ones((lp, QK_NOPE_DIM), jnp.float32), cos, cos, zeros(tail)], axis=1)
    sin_t = jnp.concatenate([zeros(QK_NOPE_DIM), -sin, sin, zeros(tail)], axis=1)
    return cos_t, sin_t


def _swap_rope_halves(a):
    half = QK_ROPE_DIM // 2
    return jnp.concatenate([jnp.zeros_like(a[..., :QK_NOPE_DIM]),
                            a[..., QK_NOPE_DIM + half:], a[..., QK_NOPE_DIM:QK_NOPE_DIM + half]],
                           axis=-1)


def _pad_lanes(a, width):
    return jnp.pad(a, [(0, 0)] * (a.ndim - 1) + [(0, width - a.shape[-1])])


def kernel(x, meta_tokens, attn_norm_g, w_in, w_pool, pool_scale, q_a_norm_g, w_q_b,
           kv_a_norm_g, w_kv_b, q_norm_g, k_norm_g, w_out, ffn_norm_g, w_gate, w_up, w_down):
    batch, seq, d = x.shape
    depth = w_in.shape[0]
    length = N_META + seq
    lp = -(-length // SEQ_TILE) * SEQ_TILE
    assert (batch * lp) % min(FFN_ROW_TILES) == 0 and D_FF % FF_CHUNK == 0
    bf16 = jnp.bfloat16

    cos_t, sin_t = _rope_tables(lp)
    h = None
    wo_all, wg_all, wu_all, wd_all = (w.astype(bf16) for w in (w_out, w_gate, w_up, w_down))
    ffn_g_all = ffn_norm_g[:, None, :]

    s3 = POOL_WIDTH + Q_LORA_RANK + KV_LORA_RANK
    rope_block = jnp.pad(w_in[:, :, s3:], ((0, 0), (0, 0), (QK_NOPE_DIM, LANES - QK_HEAD_DIM)))
    w_in_p = jnp.concatenate([w_in[:, :, :s3], rope_block], axis=2).astype(bf16)
    wq = w_q_b.reshape(depth, Q_LORA_RANK, MLA_HEADS, QK_HEAD_DIM)
    wq_p = _pad_lanes(wq, LANES).reshape(depth, Q_LORA_RANK, MLA_HEADS * LANES).astype(bf16)
    wqs_p = _pad_lanes(_swap_rope_halves(wq), LANES)
    wqs_p = wqs_p.reshape(depth, Q_LORA_RANK, MLA_HEADS * LANES).astype(bf16)
    wkv = w_kv_b.reshape(depth, KV_LORA_RANK, MLA_HEADS, QK_NOPE_DIM + V_HEAD_DIM)
    wk_t = wkv[..., :QK_NOPE_DIM].reshape(depth, KV_LORA_RANK, MLA_HEADS * QK_NOPE_DIM)
    wk_t = wk_t.transpose(0, 2, 1).astype(bf16)
    wv = wkv[..., QK_NOPE_DIM:].reshape(depth, KV_LORA_RANK, MLA_HEADS * V_HEAD_DIM).astype(bf16)
    gq = _pad_lanes(q_norm_g[:, None, :], LANES)
    gqs = _pad_lanes(_swap_rope_halves(q_norm_g[:, None, :]), LANES)
    gkn_t = jnp.broadcast_to(k_norm_g[:, :QK_NOPE_DIM, None], (depth, QK_NOPE_DIM, SEQ_TILE))
    gkr = jnp.pad(k_norm_g[:, None, QK_NOPE_DIM:],
                  ((0, 0), (0, 0), (QK_NOPE_DIM, LANES - QK_HEAD_DIM)))
    n_pairs = len(POOL_WINDOWS) // 2
    wp = w_pool.reshape(depth, n_pairs, 2, POOL_GROUP_DIM, POOL_GROUP_DIM)
    zero_blk = jnp.zeros_like(wp[:, :, 0])
    w_pool2 = jnp.concatenate([jnp.concatenate([wp[:, :, 0], zero_blk], axis=-1),
                               jnp.concatenate([zero_blk, wp[:, :, 1]], axis=-1)],
                              axis=-2).astype(bf16)
    consts = (attn_norm_g[:, None, :], w_in_p, w_pool2, pool_scale[:, None, :],
              q_a_norm_g[:, None, :], wq_p, wqs_p, kv_a_norm_g[:, None, :], wk_t, wv,
              gq, gqs, gkn_t, gkr)

    for l in range(depth):
        if l == 0:
            ypool, q, k_t, v, h = _mix_in(None, consts, l, cos_t, sin_t, lp,
                                          tokens_meta=(x, meta_tokens.astype(x.dtype)))
        else:
            ypool, q, k_t, v = _mix_in(h, consts, l, cos_t, sin_t, lp)
        ymla = _attention(q, k_t, v)

        last = l == depth - 1
        h = _out_ffn(h.reshape(batch * lp, d), ypool.reshape(batch * lp, POOL_WIDTH),
                     ymla.reshape(batch * lp, MLA_HEADS * V_HEAD_DIM),
                     wo_all, ffn_g_all, wg_all, wu_all, wd_all, l,
                     real_rows=(batch, lp, N_META, seq) if last else None)
        h = h.reshape(batch, seq if last else lp, d)
    return h
```

```python
import functools

import jax
import jax.numpy as jnp
from jax import lax
from jax.experimental import pallas as pl
from jax.experimental.pallas import tpu as pltpu

D_MODEL = 1024
N_META = 16
POOL_WIDTH = 512
POOL_WINDOWS = (2, 4, 8, 16)
POOL_GROUP_DIM = 128
MLA_HEADS = 8
QK_NOPE_DIM = 64
QK_ROPE_DIM = 32
QK_HEAD_DIM = QK_NOPE_DIM + QK_ROPE_DIM
V_HEAD_DIM = 64
Q_LORA_RANK = 384
KV_LORA_RANK = 256
D_FF = 2816
ROPE_THETA = 10000.0
RMS_EPS = 1e-6

LANES = 128
SUBLANES = 8
BF16_ROWS = 2 * SUBLANES
assert POOL_WINDOWS == tuple(2 ** (k + 1) for k in range(len(POOL_WINDOWS)))
POOL_HALO = SUBLANES * len(POOL_WINDOWS)
assert POOL_HALO >= max(POOL_WINDOWS)

SEQ_TILE = 384
FFN_ROW_TILES = (512, 768, 1024)
FF_CHUNK = 256
FF_UNROLL = 5
HEADS_PER_STEP = 8
SCORE_LOOKAHEAD = 1
TAIL_LOOKAHEAD = 2
SCORE_BUFFERS = 4
assert SCORE_LOOKAHEAD <= TAIL_LOOKAHEAD < SCORE_BUFFERS and HEADS_PER_STEP % SCORE_BUFFERS == 0
IN_WIDTH_PADDED = 1280
VMEM_LIMIT_BYTES = 56 * 1024 * 1024

NEG_BIG = -1e30
LOG2_E = 1.4426950408889634


def _rms_scale(x, width):
    return lax.rsqrt(jnp.sum(x * x, axis=-1, keepdims=True) * (1.0 / width) + RMS_EPS)


def _rope(xn, cos_t, sin_lo, sin_hi):
    half = QK_ROPE_DIM // 2
    return (xn * cos_t
            + pltpu.roll(xn, LANES - half, 1) * sin_lo
            + pltpu.roll(xn, half, 1) * sin_hi)


N_MIX_SCRATCH = 6


def _mix_in_kernel(h_ref, *refs, n_tiles):
    _mix_in_steps(lambda: h_ref[0], refs, n_tiles)


def _mix_in_first_kernel(x_ref, meta_ref, *refs, seq, n_tiles):
    h_out_ref = refs[-N_MIX_SCRATCH - 1]
    refs = refs[:-N_MIX_SCRATCH - 1] + refs[-N_MIX_SCRATCH:]

    def load_x():
        i = jnp.minimum(pl.program_id(1), n_tiles - 1)
        blk = x_ref[0]
        tm = blk.shape[0]
        last_shift = (n_tiles - 1) * tm - N_META - (seq - tm)
        first_tile = jnp.concatenate([meta_ref[...], blk[:tm - N_META]], axis=0)
        last_tile = jnp.concatenate(
            [blk[last_shift:], jnp.zeros((last_shift, blk.shape[1]), blk.dtype)], axis=0)
        x = jnp.where(i == 0, first_tile, jnp.where(i == n_tiles - 1, last_tile, blk))
        h_out_ref[0] = x
        return x

    _mix_in_steps(load_x, refs, n_tiles)


def _mix_in_steps(load_x, refs, n_tiles):
    attn_g_ref, w_in_ref, *post_refs, za_ref, zb_ref = refs
    ue_ref = post_refs[-4]
    s = pl.program_id(1)
    even = s % 2 == 0

    @pl.when(s == 1)
    def _():
        ue_ref[0:POOL_HALO, :] = jnp.zeros((POOL_HALO, POOL_WIDTH), jnp.float32)

    def project(z_write):
        x = load_x()
        xn = (x * _rms_scale(x, D_MODEL) * attn_g_ref[0]).astype(jnp.bfloat16)
        z_write[...] = jnp.dot(xn, w_in_ref[0], preferred_element_type=jnp.float32)

    def finish(z_read):
        _mix_in_finish(z_read[...], s - 1, *post_refs)

    def both(z_write, z_read):
        project(z_write)
        finish(z_read)

    inner = jnp.logical_and(s > 0, s < n_tiles)
    pl.when(s == 0)(lambda: project(za_ref))
    pl.when(jnp.logical_and(inner, even))(lambda: both(za_ref, zb_ref))
    pl.when(jnp.logical_and(inner, jnp.logical_not(even)))(lambda: both(zb_ref, za_ref))
    pl.when(s == n_tiles)(lambda: finish(zb_ref if n_tiles % 2 == 0 else za_ref))


def _mix_in_finish(z, i, w_pool_ref, pool_scale_ref,
                   qa_g_ref, wq_ref, wqs_ref, kva_g_ref, wkt_ref, wv_ref,
                   gq_ref, gqs_ref, gkn_ref, gkr_ref, cos_ref, sin_ref,
                   ypool_ref, q_ref, k_ref, v_ref, ue_ref, s2_ref, s4_ref, s8_ref):
    tm = z.shape[0]
    halo = POOL_HALO

    u = z[:, :POOL_WIDTH]

    gd = POOL_GROUP_DIM
    rows = tm + halo
    ue_ref[halo:, :] = u
    r1, r2, r3, r4 = (k * SUBLANES for k in (1, 2, 3, 4))
    s2_ref[r1:, :] = ue_ref[r1:, :] + ue_ref[r1 - 1:rows - 1, :]
    s4_ref[r2:, :] = s2_ref[r2:, gd:] + s2_ref[r2 - 2:rows - 2, gd:]
    s8_ref[r3:, :] = s4_ref[r3:, gd:] + s4_ref[r3 - 4:rows - 4, gd:]
    s16 = s8_ref[r4:, gd:] + s8_ref[r4 - 8:rows - 8, gd:]
    window_sums = (s2_ref[halo:, 0:gd], s4_ref[halo:, 0:gd], s8_ref[halo:, 0:gd], s16)

    pos = i * tm + lax.broadcasted_iota(jnp.int32, (tm, 1), 0)
    pooled = []
    for g, w in enumerate(POOL_WINDOWS):
        cnt = jnp.minimum(pos + 1, w).astype(jnp.float32)
        pooled.append((window_sums[g] / cnt - u[:, g * gd:(g + 1) * gd]).astype(jnp.bfloat16))
    for pair in range(len(POOL_WINDOWS) // 2):
        sl = slice(2 * pair * gd, 2 * (pair + 1) * gd)
        p2 = jnp.concatenate(pooled[2 * pair:2 * pair + 2], axis=1)
        y2 = jnp.dot(p2, w_pool_ref[0, pair], preferred_element_type=jnp.float32)
        ypool_ref[0, :, sl] = (y2 * pool_scale_ref[0, :, sl]).astype(jnp.bfloat16)
    ue_ref[0:halo, :] = ue_ref[tm:tm + halo, :]

    lane = lax.broadcasted_iota(jnp.int32, (1, LANES), 1)
    is_nope = lane < QK_NOPE_DIM
    rope_mid = QK_NOPE_DIM + QK_ROPE_DIM // 2
    cos_t = cos_ref[...]
    sin_t = sin_ref[...]
    scale = QK_HEAD_DIM ** -0.5 * LOG2_E

    c_q = z[:, POOL_WIDTH:POOL_WIDTH + Q_LORA_RANK]
    cqn = (c_q * _rms_scale(c_q, Q_LORA_RANK) * qa_g_ref[0]).astype(jnp.bfloat16)
    qf = jnp.dot(cqn, wq_ref[0], preferred_element_type=jnp.float32)
    qfs = jnp.dot(cqn, wqs_ref[0], preferred_element_type=jnp.float32)
    cos_g = cos_t * (gq_ref[0] * scale)
    sin_g = sin_t * (gqs_ref[0] * scale)

    for hd in range(MLA_HEADS):
        sl = slice(hd * LANES, (hd + 1) * LANES)
        qh = qf[:, sl]
        sq = qh * qh
        ss_n = jnp.sum(jnp.where(is_nope, sq, 0.0), axis=-1, keepdims=True)
        ss_r = jnp.sum(jnp.where(is_nope, 0.0, sq), axis=-1, keepdims=True)
        r_n = lax.rsqrt(ss_n * (1.0 / QK_NOPE_DIM) + RMS_EPS)
        r_r = lax.rsqrt(ss_r * (1.0 / QK_ROPE_DIM) + RMS_EPS)
        rot = qh * cos_g + qfs[:, sl] * sin_g
        q_ref[0, hd] = (rot * jnp.where(is_nope, r_n, r_r)).astype(jnp.bfloat16)

    kv0 = POOL_WIDTH + Q_LORA_RANK
    c_kv = z[:, kv0:kv0 + KV_LORA_RANK]
    ckvn = (c_kv * _rms_scale(c_kv, KV_LORA_RANK) * kva_g_ref[0]).astype(jnp.bfloat16)
    v_ref[0] = jnp.dot(ckvn, wv_ref[0],
                       preferred_element_type=jnp.float32).astype(jnp.bfloat16)

    kr = z[:, kv0 + KV_LORA_RANK:]
    krn = kr * _rms_scale(kr, QK_ROPE_DIM) * gkr_ref[0]
    kr_rot = _rope(krn, cos_t, jnp.where(lane < rope_mid, sin_t, 0.0),
                   jnp.where(lane < rope_mid, 0.0, sin_t))
    kr_rot_t = kr_rot.T[QK_NOPE_DIM:, :].astype(jnp.bfloat16)

    kf_t = lax.dot_general(wkt_ref[0], ckvn, (((1,), (1,)), ((), ())),
                           preferred_element_type=jnp.float32)
    for hd in range(MLA_HEADS):
        kh = kf_t[hd * QK_NOPE_DIM:(hd + 1) * QK_NOPE_DIM, :]
        r_k = lax.rsqrt(jnp.sum(kh * kh, axis=0, keepdims=True) * (1.0 / QK_NOPE_DIM) + RMS_EPS)
        k_ref[0, hd, 0, 0:QK_NOPE_DIM, :] = (kh * r_k * gkn_ref[0]).astype(jnp.bfloat16)
        k_ref[0, hd, 0, QK_NOPE_DIM:, :] = kr_rot_t


def _attn_kernel(q_ref, k_ref, v_ref, o_ref, s_scr, m_scr, acc_scr):
    assert tuple(s_scr.shape[1:]) == (q_ref.shape[2], q_ref.shape[2])
    i = pl.program_id(2)
    tq = q_ref.shape[2]
    tk = s_scr.shape[2]
    n_slabs = tk // LANES
    lane = lax.broadcasted_iota(jnp.int32, (1, LANES), 1)
    low_half = lane < V_HEAD_DIM

    n_buf = s_scr.shape[0]
    heads = HEADS_PER_STEP

    def scores(hh, j):
        s_scr[hh % n_buf] = jnp.dot(q_ref[0, hh], k_ref[0, hh, j],
                                    preferred_element_type=jnp.float32)

    def softmax_pv(hh, j, masked):
        rb = LANES
        if masked:
            tri = (lax.broadcasted_iota(jnp.int32, (rb, LANES), 1)
                   <= lax.broadcasted_iota(jnp.int32, (rb, LANES), 0))
        alphas, ps = [], []
        for rblk in range(tq // rb):
            r0 = rblk * rb
            blk = []
            for c in range(n_slabs if not masked else rblk + 1):
                sl = s_scr[hh % n_buf, r0:r0 + rb, c * LANES:(c + 1) * LANES]
                if masked and c == rblk:
                    sl = jnp.where(tri, sl, NEG_BIG)
                blk.append(sl)
            mx = blk[0]
            for sl in blk[1:]:
                mx = jnp.maximum(mx, sl)
            m_old = m_scr[hh, r0:r0 + rb]
            m_new = jnp.maximum(m_old, jnp.max(mx, axis=-1, keepdims=True))
            m_scr[hh, r0:r0 + rb] = m_new
            alphas.append(jnp.exp2(m_old - m_new))
            p_blk = [jnp.exp2(sl - m_new).astype(jnp.bfloat16) for sl in blk]
            p_blk += [jnp.zeros((rb, LANES), jnp.bfloat16)] * (n_slabs - len(blk))
            ps.append(jnp.concatenate(p_blk, axis=1))
        alpha = jnp.concatenate(alphas, axis=0)
        p = jnp.concatenate(ps, axis=0)
        start = pl.multiple_of(j * tk, tk)
        pair = hh // 2
        vj = v_ref[0, pl.ds(start, tk), pair * LANES:(pair + 1) * LANES]
        own = low_half if hh % 2 == 0 else jnp.logical_not(low_half)
        vj = jnp.where(own, vj, jnp.ones_like(vj))
        acc_scr[hh] = alpha * acc_scr[hh] + jnp.dot(p, vj, preferred_element_type=jnp.float32)

    m_scr[...] = jnp.full(m_scr.shape, NEG_BIG, jnp.float32)
    acc_scr[...] = jnp.zeros(acc_scr.shape, jnp.float32)
    ahead = SCORE_LOOKAHEAD
    for hh in range(ahead):
        scores(hh, 0)

    def chunk(j):
        for hh in range(heads):
            scores((hh + ahead) % heads, j + (hh + ahead) // heads)
            softmax_pv(hh, j, masked=False)

    def chunk_pair(jj, carry):
        chunk(2 * jj)
        chunk(2 * jj + 1)
        return carry

    lax.fori_loop(0, lax.shift_right_logical(i, 1), chunk_pair, 0)

    @pl.when(i % 2 == 1)
    def _():
        chunk(i - 1)

    tail_ahead = TAIL_LOOKAHEAD
    for hh in range(ahead, tail_ahead):
        scores(hh, i)
    def finalize(pair):
        acc_e = acc_scr[2 * pair]
        acc_o = acc_scr[2 * pair + 1]
        out_e = acc_e / acc_e[:, V_HEAD_DIM:V_HEAD_DIM + 1]
        out_o = acc_o / acc_o[:, 0:1]
        o_ref[0, :, pair * LANES:(pair + 1) * LANES] = jnp.where(
            low_half, out_e, out_o).astype(jnp.bfloat16)

    for hh in range(heads):
        if hh + tail_ahead < heads:
            scores(hh + tail_ahead, i)
        softmax_pv(hh, i, masked=True)
        if hh % 2 == 1:
            finalize(hh // 2)


def _out_ffn_kernel(h_ref, ypool_ref, ymla_ref, wo_ref, ffn_g_ref,
                    wg_ref, wu_ref, wd_ref, o_ref, acc_ref):
    h1 = (h_ref[...]
          + jnp.dot(ypool_ref[...], wo_ref[0, :POOL_WIDTH, :], preferred_element_type=jnp.float32)
          + jnp.dot(ymla_ref[...], wo_ref[0, POOL_WIDTH:, :], preferred_element_type=jnp.float32))
    g = (h1 * _rms_scale(h1, D_MODEL) * ffn_g_ref[0]).astype(jnp.bfloat16)
    acc_ref[...] = h1

    def chunk_slice(c):
        return pl.ds(pl.multiple_of(c * FF_CHUNK, FF_CHUNK), FF_CHUNK)

    def gate_up(c):
        cols = chunk_slice(c)
        gate = jnp.dot(g, wg_ref[0, :, cols], preferred_element_type=jnp.float32)
        up = jnp.dot(g, wu_ref[0, :, cols], preferred_element_type=jnp.float32)
        return (gate * (1.0 / (1.0 + jnp.exp(-gate))) * up).astype(jnp.bfloat16)

    def down(act, c):
        acc_ref[...] += jnp.dot(act, wd_ref[0, chunk_slice(c), :],
                                preferred_element_type=jnp.float32)

    n_chunks = wg_ref.shape[2] // FF_CHUNK

    def chunk_group(first, count):
        act = gate_up(first)
        for k in range(count):
            nxt = gate_up(first + k + 1) if k + 1 < count else None
            down(act, first + k)
            act = nxt

    def loop_step(step, carry):
        chunk_group(step * FF_UNROLL, FF_UNROLL)
        return carry

    n_steps = n_chunks // FF_UNROLL
    lax.fori_loop(0, n_steps, loop_step, 0)
    if n_chunks % FF_UNROLL:
        chunk_group(n_steps * FF_UNROLL, n_chunks % FF_UNROLL)
    o_ref[...] = acc_ref[...]


def _const_spec(shape):
    nd = len(shape)
    return pl.BlockSpec(shape, lambda *_: (0,) * nd)


def _layer_spec(stacked, layer):
    shape = (1,) + stacked.shape[1:]
    nd = len(shape)
    return pl.BlockSpec(shape, lambda *_: (layer,) + (0,) * (nd - 1),
                        pipeline_mode=pl.Buffered(1))


def _mix_in(h, consts, layer, cos_t, sin_t, lp, tokens_meta=None):
    first = tokens_meta is not None
    batch = tokens_meta[0].shape[0] if first else h.shape[0]
    d = D_MODEL
    tm = SEQ_TILE
    n_tiles = lp // tm
    proj = lambda s: jnp.minimum(s, n_tiles - 1)
    done = lambda s: jnp.maximum(s - 1, 0)
    in_row_spec = lambda width: pl.BlockSpec((1, tm, width), lambda b, s: (b, proj(s), 0))
    row_spec = lambda width: pl.BlockSpec((1, tm, width), lambda b, s: (b, done(s), 0))
    q_spec = pl.BlockSpec((1, MLA_HEADS, tm, LANES), lambda b, s: (b, 0, done(s), 0))
    kt_spec = pl.BlockSpec((1, MLA_HEADS, 1, LANES, tm), lambda b, s: (b, 0, done(s), 0, 0))
    table_spec = pl.BlockSpec((tm, LANES), lambda b, s: (done(s), 0))
    out_specs = [row_spec(POOL_WIDTH), q_spec, kt_spec, row_spec(MLA_HEADS * V_HEAD_DIM)]
    out_shape = [
        jax.ShapeDtypeStruct((batch, lp, POOL_WIDTH), jnp.bfloat16),
        jax.ShapeDtypeStruct((batch, MLA_HEADS, lp, LANES), jnp.bfloat16),
        jax.ShapeDtypeStruct((batch, MLA_HEADS, n_tiles, LANES, tm), jnp.bfloat16),
        jax.ShapeDtypeStruct((batch, lp, MLA_HEADS * V_HEAD_DIM), jnp.bfloat16),
    ]
    scratch = [pltpu.VMEM((tm + POOL_HALO, POOL_WIDTH - drop * POOL_GROUP_DIM), jnp.float32)
               for drop in (0, 0, 1, 2)]
    scratch += [pltpu.VMEM((tm, IN_WIDTH_PADDED), jnp.float32)] * 2
    assert len(scratch) == N_MIX_SCRATCH
    if first:
        x, meta = tokens_meta
        seq = x.shape[1]
        body = functools.partial(_mix_in_first_kernel, seq=seq, n_tiles=n_tiles)
        x_spec = pl.BlockSpec(
            (pl.Element(1), pl.Element(tm), pl.Element(d)),
            lambda b, s: (b, pl.multiple_of(jnp.clip(proj(s) * tm - N_META, 0, seq - tm),
                                            SUBLANES), 0))
        lead_specs = [x_spec, _const_spec(meta.shape)]
        lead_args = [x, meta]
        out_specs.append(in_row_spec(d))
        out_shape.append(jax.ShapeDtypeStruct((batch, lp, d), jnp.float32))
    else:
        body = functools.partial(_mix_in_kernel, n_tiles=n_tiles)
        lead_specs = [in_row_spec(d)]
        lead_args = [h]
    return pl.pallas_call(
        body,
        grid=(batch, n_tiles + 1),
        in_specs=lead_specs + [_layer_spec(c, layer) for c in consts] + [table_spec, table_spec],
        out_specs=out_specs,
        out_shape=out_shape,
        scratch_shapes=scratch,
        compiler_params=pltpu.CompilerParams(
            dimension_semantics=("arbitrary", "arbitrary"),
            vmem_limit_bytes=VMEM_LIMIT_BYTES),
        name="mix_in",
    )(*lead_args, *consts, cos_t, sin_t)


def _attention(q, k_t, v):
    batch, heads, lp, _ = q.shape
    tq = SEQ_TILE
    hps = HEADS_PER_STEP
    n_chunks = k_t.shape[2]
    assert k_t.shape[4] == tq
    return pl.pallas_call(
        _attn_kernel,
        grid=(batch, heads // hps, lp // tq),
        in_specs=[
            pl.BlockSpec((1, hps, tq, LANES), lambda b, hp, i: (b, hp, i, 0)),
            pl.BlockSpec((1, hps, n_chunks, LANES, tq), lambda b, hp, i: (b, hp, 0, 0, 0)),
            pl.BlockSpec((1, lp, hps * V_HEAD_DIM), lambda b, hp, i: (b, 0, hp)),
        ],
        out_specs=pl.BlockSpec((1, tq, hps * V_HEAD_DIM), lambda b, hp, i: (b, i, hp)),
        out_shape=jax.ShapeDtypeStruct((batch, lp, heads * V_HEAD_DIM), jnp.bfloat16),
        scratch_shapes=[
            pltpu.VMEM((SCORE_BUFFERS, tq, tq), jnp.float32),
            pltpu.VMEM((hps, tq, LANES), jnp.float32),
            pltpu.VMEM((hps, tq, LANES), jnp.float32),
        ],
        compiler_params=pltpu.CompilerParams(
            dimension_semantics=("arbitrary", "arbitrary", "arbitrary"),
            vmem_limit_bytes=VMEM_LIMIT_BYTES),
        name="mla_attn",
    )(q, k_t, v)


def _out_ffn(h2d, ypool2d, ymla2d, wo, ffn_g, wg, wu, wd, layer, real_rows=None):
    rows, d = h2d.shape
    per_call_rows = rows if real_rows is None else real_rows[3]
    tm = max(t for t in FFN_ROW_TILES if per_call_rows % t == 0)
    if real_rows is None:
        grid = (rows // tm,)
        in_row_spec = lambda width: pl.BlockSpec((tm, width), lambda i: (i, 0))
        out_spec = in_row_spec(d)
        out_rows = rows
    else:
        batch, lp, first, count = real_rows
        tiles = count // tm
        assert count % tm == 0 and first % BF16_ROWS == 0 and lp % BF16_ROWS == 0
        grid = (batch, tiles)
        in_row_spec = lambda width: pl.BlockSpec(
            (pl.Element(tm), pl.Element(width)),
            lambda b, j: (pl.multiple_of(b * lp + first + j * tm, BF16_ROWS), 0))
        out_spec = pl.BlockSpec((tm, d), lambda b, j: (b * tiles + j, 0))
        out_rows = batch * count
    return pl.pallas_call(
        _out_ffn_kernel,
        grid=grid,
        in_specs=[
            in_row_spec(d), in_row_spec(POOL_WIDTH), in_row_spec(MLA_HEADS * V_HEAD_DIM),
            _layer_spec(wo, layer), _layer_spec(ffn_g, layer),
            _layer_spec(wg, layer), _layer_spec(wu, layer), _layer_spec(wd, layer),
        ],
        out_specs=out_spec,
        out_shape=jax.ShapeDtypeStruct((out_rows, d), jnp.float32),
        scratch_shapes=[pltpu.VMEM((tm, d), jnp.float32)],
        compiler_params=pltpu.CompilerParams(
            dimension_semantics=("arbitrary",) * len(grid),
            vmem_limit_bytes=VMEM_LIMIT_BYTES),
        name="out_ffn",
    )(h2d, ypool2d, ymla2d, wo, ffn_g, wg, wu, wd)


def _rope_tables(lp):
    inv = 1.0 / (ROPE_THETA ** (jnp.arange(0, QK_ROPE_DIM, 2, dtype=jnp.float32) / QK_ROPE_DIM))
    ang = jnp.arange(lp, dtype=jnp.float32)[:, None] * inv[None, :]
    cos, sin = jnp.cos(ang), jnp.sin(ang)
    zeros = lambda n: jnp.zeros((lp, n), jnp.float32)
    tail = LANES - QK_HEAD_DIM
    cos_t = jnp.concatenate([jnp.ones((lp, QK_NOPE_DIM), jnp.float32), cos, cos, zeros(tail)], axis=1)
    sin_t = jnp.concatenate([zeros(QK_NOPE_DIM), -sin, sin, zeros(tail)], axis=1)
    return cos_t, sin_t


def _swap_rope_halves(a):
    half = QK_ROPE_DIM // 2
    return jnp.concatenate([jnp.zeros_like(a[..., :QK_NOPE_DIM]),
                            a[..., QK_NOPE_DIM + half:], a[..., QK_NOPE_DIM:QK_NOPE_DIM + half]],
                           axis=-1)


def _pad_lanes(a, width):
    return jnp.pad(a, [(0, 0)] * (a.ndim - 1) + [(0, width - a.shape[-1])])


def kernel(x, meta_tokens, attn_norm_g, w_in, w_pool, pool_scale, q_a_norm_g, w_q_b,
           kv_a_norm_g, w_kv_b, q_norm_g, k_norm_g, w_out, ffn_norm_g, w_gate, w_up, w_down):
    batch, seq, d = x.shape
    depth = w_in.shape[0]
    length = N_META + seq
    lp = -(-length // SEQ_TILE) * SEQ_TILE
    assert (batch * lp) % min(FFN_ROW_TILES) == 0 and D_FF % FF_CHUNK == 0
    bf16 = jnp.bfloat16

    cos_t, sin_t = _rope_tables(lp)
    h = None
    wo_all, wg_all, wu_all, wd_all = (w.astype(bf16) for w in (w_out, w_gate, w_up, w_down))
    ffn_g_all = ffn_norm_g[:, None, :]

    s3 = POOL_WIDTH + Q_LORA_RANK + KV_LORA_RANK
    rope_block = jnp.pad(w_in[:, :, s3:], ((0, 0), (0, 0), (QK_NOPE_DIM, LANES - QK_HEAD_DIM)))
    w_in_p = jnp.concatenate([w_in[:, :, :s3], rope_block], axis=2).astype(bf16)
    wq = w_q_b.reshape(depth, Q_LORA_RANK, MLA_HEADS, QK_HEAD_DIM)
    wq_p = _pad_lanes(wq, LANES).reshape(depth, Q_LORA_RANK, MLA_HEADS * LANES).astype(bf16)
    wqs_p = _pad_lanes(_swap_rope_halves(wq), LANES)
    wqs_p = wqs_p.reshape(depth, Q_LORA_RANK, MLA_HEADS * LANES).astype(bf16)
    wkv = w_kv_b.reshape(depth, KV_LORA_RANK, MLA_HEADS, QK_NOPE_DIM + V_HEAD_DIM)
    wk_t = wkv[..., :QK_NOPE_DIM].reshape(depth, KV_LORA_RANK, MLA_HEADS * QK_NOPE_DIM)
    wk_t = wk_t.transpose(0, 2, 1).astype(bf16)
    wv = wkv[..., QK_NOPE_DIM:].reshape(depth, KV_LORA_RANK, MLA_HEADS * V_HEAD_DIM).astype(bf16)
    gq = _pad_lanes(q_norm_g[:, None, :], LANES)
    gqs = _pad_lanes(_swap_rope_halves(q_norm_g[:, None, :]), LANES)
    gkn_t = jnp.broadcast_to(k_norm_g[:, :QK_NOPE_DIM, None], (depth, QK_NOPE_DIM, SEQ_TILE))
    gkr = jnp.pad(k_norm_g[:, None, QK_NOPE_DIM:],
                  ((0, 0), (0, 0), (QK_NOPE_DIM, LANES - QK_HEAD_DIM)))
    n_pairs = len(POOL_WINDOWS) // 2
    wp = w_pool.reshape(depth, n_pairs, 2, POOL_GROUP_DIM, POOL_GROUP_DIM)
    zero_blk = jnp.zeros_like(wp[:, :, 0])
    w_pool2 = jnp.concatenate([jnp.concatenate([wp[:, :, 0], zero_blk], axis=-1),
                               jnp.concatenate([zero_blk, wp[:, :, 1]], axis=-1)],
                              axis=-2).astype(bf16)
    consts = (attn_norm_g[:, None, :], w_in_p, w_pool2, pool_scale[:, None, :],
              q_a_norm_g[:, None, :], wq_p, wqs_p, kv_a_norm_g[:, None, :], wk_t, wv,
              gq, gqs, gkn_t, gkr)

    for l in range(depth):
        if l == 0:
            ypool, q, k_t, v, h = _mix_in(None, consts, l, cos_t, sin_t, lp,
                                          tokens_meta=(x, meta_tokens.astype(x.dtype)))
        else:
            ypool, q, k_t, v = _mix_in(h, consts, l, cos_t, sin_t, lp)
        ymla = _attention(q, k_t, v)

        last = l == depth - 1
        h = _out_ffn(h.reshape(batch * lp, d), ypool.reshape(batch * lp, POOL_WIDTH),
                     ymla.reshape(batch * lp, MLA_HEADS * V_HEAD_DIM),
                     wo_all, ffn_g_all, wg_all, wu_all, wd_all, l,
                     real_rows=(batch, lp, N_META, seq) if last else None)
        h = h.reshape(batch, seq if last else lp, d)
    return h
```

```python
import functools

import jax
import jax.numpy as jnp
from jax import lax
from jax.experimental import pallas as pl
from jax.experimental.pallas import tpu as pltpu

D_MODEL = 1024
N_META = 16
POOL_WIDTH = 512
POOL_WINDOWS = (2, 4, 8, 16)
POOL_GROUP_DIM = 128
MLA_HEADS = 8
QK_NOPE_DIM = 64
QK_ROPE_DIM = 32
QK_HEAD_DIM = QK_NOPE_DIM + QK_ROPE_DIM
V_HEAD_DIM = 64
Q_LORA_RANK = 384
KV_LORA_RANK = 256
D_FF = 2816
ROPE_THETA = 10000.0
RMS_EPS = 1e-6

LANES = 128
SUBLANES = 8
BF16_ROWS = 2 * SUBLANES
assert POOL_WINDOWS == tuple(2 ** (k + 1) for k in range(len(POOL_WINDOWS)))
POOL_HALO = SUBLANES * len(POOL_WINDOWS)
assert POOL_HALO >= max(POOL_WINDOWS)

SEQ_TILE = 384
FFN_ROW_TILES = (512, 768, 1024)
FF_CHUNK = 256
FF_UNROLL = 5
HEADS_PER_STEP = 8
SCORE_LOOKAHEAD = 1
TAIL_LOOKAHEAD = 2
SCORE_BUFFERS = 4
assert SCORE_LOOKAHEAD <= TAIL_LOOKAHEAD < SCORE_BUFFERS and HEADS_PER_STEP % SCORE_BUFFERS == 0
IN_WIDTH_PADDED = 1280
VMEM_LIMIT_BYTES = 56 * 1024 * 1024

NEG_BIG = -1e30
LOG2_E = 1.4426950408889634


def _rms_scale(x, width):
    return lax.rsqrt(jnp.sum(x * x, axis=-1, keepdims=True) * (1.0 / width) + RMS_EPS)


def _rope(xn, cos_t, sin_lo, sin_hi):
    half = QK_ROPE_DIM // 2
    return (xn * cos_t
            + pltpu.roll(xn, LANES - half, 1) * sin_lo
            + pltpu.roll(xn, half, 1) * sin_hi)


N_MIX_SCRATCH = 6


def _mix_in_kernel(h_ref, *refs, n_tiles):
    _mix_in_steps(lambda: h_ref[0], refs, n_tiles)


def _mix_in_first_kernel(x_ref, meta_ref, *refs, seq, n_tiles):
    h_out_ref = refs[-N_MIX_SCRATCH - 1]
    refs = refs[:-N_MIX_SCRATCH - 1] + refs[-N_MIX_SCRATCH:]

    def load_x():
        i = jnp.minimum(pl.program_id(1), n_tiles - 1)
        blk = x_ref[0]
        tm = blk.shape[0]
        last_shift = (n_tiles - 1) * tm - N_META - (seq - tm)
        first_tile = jnp.concatenate([meta_ref[...], blk[:tm - N_META]], axis=0)
        last_tile = jnp.concatenate(
            [blk[last_shift:], jnp.zeros((last_shift, blk.shape[1]), blk.dtype)], axis=0)
        x = jnp.where(i == 0, first_tile, jnp.where(i == n_tiles - 1, last_tile, blk))
        h_out_ref[0] = x
        return x

    _mix_in_steps(load_x, refs, n_tiles)


def _mix_in_steps(load_x, refs, n_tiles):
    attn_g_ref, w_in_ref, *post_refs, za_ref, zb_ref = refs
    ue_ref = post_refs[-4]
    s = pl.program_id(1)
    even = s % 2 == 0

    @pl.when(s == 1)
    def _():
        ue_ref[0:POOL_HALO, :] = jnp.zeros((POOL_HALO, POOL_WIDTH), jnp.float32)

    def project(z_write):
        x = load_x()
        xn = (x * _rms_scale(x, D_MODEL) * attn_g_ref[0]).astype(jnp.bfloat16)
        z_write[...] = jnp.dot(xn, w_in_ref[0], preferred_element_type=jnp.float32)

    def finish(z_read):
        _mix_in_finish(z_read[...], s - 1, *post_refs)

    def both(z_write, z_read):
        project(z_write)
        finish(z_read)

    inner = jnp.logical_and(s > 0, s < n_tiles)
    pl.when(s == 0)(lambda: project(za_ref))
    pl.when(jnp.logical_and(inner, even))(lambda: both(za_ref, zb_ref))
    pl.when(jnp.logical_and(inner, jnp.logical_not(even)))(lambda: both(zb_ref, za_ref))
    pl.when(s == n_tiles)(lambda: finish(zb_ref if n_tiles % 2 == 0 else za_ref))


def _mix_in_finish(z, i, w_pool_ref, pool_scale_ref,
                   qa_g_ref, wqt_ref, kva_g_ref, wkt_ref, wv_ref,
                   gqn_ref, gq1_ref, gq2_ref, gkn_ref, gkr_ref,
                   cos_ref, sin_ref, cosr_ref, sinr_ref,
                   ypool_ref, q_ref, k_ref, v_ref, ue_ref, s2_ref, s4_ref, s8_ref):
    tm = z.shape[0]
    halo = POOL_HALO

    u = z[:, :POOL_WIDTH]

    gd = POOL_GROUP_DIM
    rows = tm + halo
    ue_ref[halo:, :] = u
    r1, r2, r3, r4 = (k * SUBLANES for k in (1, 2, 3, 4))
    s2_ref[r1:, :] = ue_ref[r1:, :] + ue_ref[r1 - 1:rows - 1, :]
    s4_ref[r2:, :] = s2_ref[r2:, gd:] + s2_ref[r2 - 2:rows - 2, gd:]
    s8_ref[r3:, :] = s4_ref[r3:, gd:] + s4_ref[r3 - 4:rows - 4, gd:]
    s16 = s8_ref[r4:, gd:] + s8_ref[r4 - 8:rows - 8, gd:]
    window_sums = (s2_ref[halo:, 0:gd], s4_ref[halo:, 0:gd], s8_ref[halo:, 0:gd], s16)

    pos = i * tm + lax.broadcasted_iota(jnp.int32, (tm, 1), 0)
    pooled = []
    for g, w in enumerate(POOL_WINDOWS):
        cnt = jnp.minimum(pos + 1, w).astype(jnp.float32)
        pooled.append((window_sums[g] / cnt - u[:, g * gd:(g + 1) * gd]).astype(jnp.bfloat16))
    for pair in range(len(POOL_WINDOWS) // 2):
        sl = slice(2 * pair * gd, 2 * (pair + 1) * gd)
        p2 = jnp.concatenate(pooled[2 * pair:2 * pair + 2], axis=1)
        y2 = jnp.dot(p2, w_pool_ref[0, pair], preferred_element_type=jnp.float32)
        ypool_ref[0, :, sl] = (y2 * pool_scale_ref[0, :, sl]).astype(jnp.bfloat16)
    ue_ref[0:halo, :] = ue_ref[tm:tm + halo, :]

    lane = lax.broadcasted_iota(jnp.int32, (1, LANES), 1)
    is_nope = lane < QK_NOPE_DIM
    rope_mid = QK_NOPE_DIM + QK_ROPE_DIM // 2
    cos_t = cos_ref[...]
    sin_t = sin_ref[...]
    scale = QK_HEAD_DIM ** -0.5 * LOG2_E

    c_q = z[:, POOL_WIDTH:POOL_WIDTH + Q_LORA_RANK]
    cqn = (c_q * _rms_scale(c_q, Q_LORA_RANK) * qa_g_ref[0]).astype(jnp.bfloat16)
    qf_t = lax.dot_general(wqt_ref[0], cqn, (((1,), (1,)), ((), ())),
                           preferred_element_type=jnp.float32)
    cos_r = cosr_ref[...]
    sin_r = sinr_ref[...]
    half = QK_ROPE_DIM // 2
    zero_rows = jnp.zeros((LANES - QK_HEAD_DIM, tm), jnp.float32)
    for hd in range(MLA_HEADS):
        r0 = hd * LANES
        xc = qf_t[r0:r0 + QK_NOPE_DIM, :]
        x1 = qf_t[r0 + QK_NOPE_DIM:r0 + QK_NOPE_DIM + half, :]
        x2 = qf_t[r0 + QK_NOPE_DIM + half:r0 + QK_HEAD_DIM, :]
        ss_n = jnp.sum(xc * xc, axis=0, keepdims=True)
        ss_r = jnp.sum(x1 * x1, axis=0, keepdims=True) + jnp.sum(x2 * x2, axis=0, keepdims=True)
        r_n = lax.rsqrt(ss_n * (1.0 / QK_NOPE_DIM) + RMS_EPS) * scale
        r_r = lax.rsqrt(ss_r * (1.0 / QK_ROPE_DIM) + RMS_EPS) * scale
        a = x1 * gq1_ref[0]
        b = x2 * gq2_ref[0]
        head_t = jnp.concatenate([xc * gqn_ref[0] * r_n,
                                  (a * cos_r - b * sin_r) * r_r,
                                  (a * sin_r + b * cos_r) * r_r,
                                  zero_rows], axis=0)
        q_ref[0, hd] = head_t.T.astype(jnp.bfloat16)

    kv0 = POOL_WIDTH + Q_LORA_RANK
    c_kv = z[:, kv0:kv0 + KV_LORA_RANK]
    ckvn = (c_kv * _rms_scale(c_kv, KV_LORA_RANK) * kva_g_ref[0]).astype(jnp.bfloat16)
    v_ref[0] = jnp.dot(ckvn, wv_ref[0],
                       preferred_element_type=jnp.float32).astype(jnp.bfloat16)

    kr = z[:, kv0 + KV_LORA_RANK:]
    krn = kr * _rms_scale(kr, QK_ROPE_DIM) * gkr_ref[0]
    kr_rot = _rope(krn, cos_t, jnp.where(lane < rope_mid, sin_t, 0.0),
                   jnp.where(lane < rope_mid, 0.0, sin_t))
    kr_rot_t = kr_rot.T[QK_NOPE_DIM:, :].astype(jnp.bfloat16)

    kf_t = lax.dot_general(wkt_ref[0], ckvn, (((1,), (1,)), ((), ())),
                           preferred_element_type=jnp.float32)
    for hd in range(MLA_HEADS):
        kh = kf_t[hd * QK_NOPE_DIM:(hd + 1) * QK_NOPE_DIM, :]
        r_k = lax.rsqrt(jnp.sum(kh * kh, axis=0, keepdims=True) * (1.0 / QK_NOPE_DIM) + RMS_EPS)
        k_ref[0, hd, 0, 0:QK_NOPE_DIM, :] = (kh * r_k * gkn_ref[0]).astype(jnp.bfloat16)
        k_ref[0, hd, 0, QK_NOPE_DIM:, :] = kr_rot_t


def _attn_kernel(q_ref, k_ref, v_ref, o_ref, s_scr, m_scr, acc_scr):
    assert tuple(s_scr.shape[1:]) == (q_ref.shape[2], q_ref.shape[2])
    i = pl.program_id(2)
    tq = q_ref.shape[2]
    tk = s_scr.shape[2]
    n_slabs = tk // LANES
    lane = lax.broadcasted_iota(jnp.int32, (1, LANES), 1)
    low_half = lane < V_HEAD_DIM

    n_buf = s_scr.shape[0]
    heads = HEADS_PER_STEP

    def scores(hh, j):
        s_scr[hh % n_buf] = jnp.dot(q_ref[0, hh], k_ref[0, hh, j],
                                    preferred_element_type=jnp.float32)

    def softmax_pv(hh, j, masked):
        rb = LANES
        if masked:
            tri = (lax.broadcasted_iota(jnp.int32, (rb, LANES), 1)
                   <= lax.broadcasted_iota(jnp.int32, (rb, LANES), 0))
        alphas, ps = [], []
        for rblk in range(tq // rb):
            r0 = rblk * rb
            blk = []
            for c in range(n_slabs if not masked else rblk + 1):
                sl = s_scr[hh % n_buf, r0:r0 + rb, c * LANES:(c + 1) * LANES]
                if masked and c == rblk:
                    sl = jnp.where(tri, sl, NEG_BIG)
                blk.append(sl)
            mx = blk[0]
            for sl in blk[1:]:
                mx = jnp.maximum(mx, sl)
            m_old = m_scr[hh, r0:r0 + rb]
            m_new = jnp.maximum(m_old, jnp.max(mx, axis=-1, keepdims=True))
            m_scr[hh, r0:r0 + rb] = m_new
            alphas.append(jnp.exp2(m_old - m_new))
            p_blk = [jnp.exp2(sl - m_new).astype(jnp.bfloat16) for sl in blk]
            p_blk += [jnp.zeros((rb, LANES), jnp.bfloat16)] * (n_slabs - len(blk))
            ps.append(jnp.concatenate(p_blk, axis=1))
        alpha = jnp.concatenate(alphas, axis=0)
        p = jnp.concatenate(ps, axis=0)
        start = pl.multiple_of(j * tk, tk)
        pair = hh // 2
        vj = v_ref[0, pl.ds(start, tk), pair * LANES:(pair + 1) * LANES]
        own = low_half if hh % 2 == 0 else jnp.logical_not(low_half)
        vj = jnp.where(own, vj, jnp.ones_like(vj))
        acc_scr[hh] = alpha * acc_scr[hh] + jnp.dot(p, vj, preferred_element_type=jnp.float32)

    m_scr[...] = jnp.full(m_scr.shape, NEG_BIG, jnp.float32)
    acc_scr[...] = jnp.zeros(acc_scr.shape, jnp.float32)
    ahead = SCORE_LOOKAHEAD
    for hh in range(ahead):
        scores(hh, 0)

    def chunk(j):
        for hh in range(heads):
            scores((hh + ahead) % heads, j + (hh + ahead) // heads)
            softmax_pv(hh, j, masked=False)

    def chunk_pair(jj, carry):
        chunk(2 * jj)
        chunk(2 * jj + 1)
        return carry

    lax.fori_loop(0, lax.shift_right_logical(i, 1), chunk_pair, 0)

    @pl.when(i % 2 == 1)
    def _():
        chunk(i - 1)

    tail_ahead = TAIL_LOOKAHEAD
    for hh in range(ahead, tail_ahead):
        scores(hh, i)
    def finalize(pair):
        acc_e = acc_scr[2 * pair]
        acc_o = acc_scr[2 * pair + 1]
        out_e = acc_e / acc_e[:, V_HEAD_DIM:V_HEAD_DIM + 1]
        out_o = acc_o / acc_o[:, 0:1]
        o_ref[0, :, pair * LANES:(pair + 1) * LANES] = jnp.where(
            low_half, out_e, out_o).astype(jnp.bfloat16)

    for hh in range(heads):
        if hh + tail_ahead < heads:
            scores(hh + tail_ahead, i)
        softmax_pv(hh, i, masked=True)
        if hh % 2 == 1:
            finalize(hh // 2)


def _out_ffn_kernel(h_ref, ypool_ref, ymla_ref, wo_ref, ffn_g_ref,
                    wg_ref, wu_ref, wd_ref, o_ref, acc_ref):
    h1 = (h_ref[...]
          + jnp.dot(ypool_ref[...], wo_ref[0, :POOL_WIDTH, :], preferred_element_type=jnp.float32)
          + jnp.dot(ymla_ref[...], wo_ref[0, POOL_WIDTH:, :], preferred_element_type=jnp.float32))
    g = (h1 * _rms_scale(h1, D_MODEL) * ffn_g_ref[0]).astype(jnp.bfloat16)
    acc_ref[...] = h1

    def chunk_slice(c):
        return pl.ds(pl.multiple_of(c * FF_CHUNK, FF_CHUNK), FF_CHUNK)

    def gate_up(c):
        cols = chunk_slice(c)
        gate = jnp.dot(g, wg_ref[0, :, cols], preferred_element_type=jnp.float32)
        up = jnp.dot(g, wu_ref[0, :, cols], preferred_element_type=jnp.float32)
        return (gate * (1.0 / (1.0 + jnp.exp(-gate))) * up).astype(jnp.bfloat16)

    def down(act, c):
        acc_ref[...] += jnp.dot(act, wd_ref[0, chunk_slice(c), :],
                                preferred_element_type=jnp.float32)

    n_chunks = wg_ref.shape[2] // FF_CHUNK

    def chunk_group(first, count):
        act = gate_up(first)
        for k in range(count):
            nxt = gate_up(first + k + 1) if k + 1 < count else None
            down(act, first + k)
            act = nxt

    def loop_step(step, carry):
        chunk_group(step * FF_UNROLL, FF_UNROLL)
        return carry

    n_steps = n_chunks // FF_UNROLL
    lax.fori_loop(0, n_steps, loop_step, 0)
    if n_chunks % FF_UNROLL:
        chunk_group(n_steps * FF_UNROLL, n_chunks % FF_UNROLL)
    o_ref[...] = acc_ref[...]


def _const_spec(shape):
    nd = len(shape)
    return pl.BlockSpec(shape, lambda *_: (0,) * nd)


def _layer_spec(stacked, layer):
    shape = (1,) + stacked.shape[1:]
    nd = len(shape)
    return pl.BlockSpec(shape, lambda *_: (layer,) + (0,) * (nd - 1),
                        pipeline_mode=pl.Buffered(1))


def _mix_in(h, consts, layer, tables, lp, tokens_meta=None):
    first = tokens_meta is not None
    batch = tokens_meta[0].shape[0] if first else h.shape[0]
    d = D_MODEL
    tm = SEQ_TILE
    n_tiles = lp // tm
    proj = lambda s: jnp.minimum(s, n_tiles - 1)
    done = lambda s: jnp.maximum(s - 1, 0)
    in_row_spec = lambda width: pl.BlockSpec((1, tm, width), lambda b, s: (b, proj(s), 0))
    row_spec = lambda width: pl.BlockSpec((1, tm, width), lambda b, s: (b, done(s), 0))
    q_spec = pl.BlockSpec((1, MLA_HEADS, tm, LANES), lambda b, s: (b, 0, done(s), 0))
    kt_spec = pl.BlockSpec((1, MLA_HEADS, 1, LANES, tm), lambda b, s: (b, 0, done(s), 0, 0))
    table_spec = pl.BlockSpec((tm, LANES), lambda b, s: (done(s), 0))
    table_t_spec = pl.BlockSpec((QK_ROPE_DIM // 2, tm), lambda b, s: (0, done(s)))
    out_specs = [row_spec(POOL_WIDTH), q_spec, kt_spec, row_spec(MLA_HEADS * V_HEAD_DIM)]
    out_shape = [
        jax.ShapeDtypeStruct((batch, lp, POOL_WIDTH), jnp.bfloat16),
        jax.ShapeDtypeStruct((batch, MLA_HEADS, lp, LANES), jnp.bfloat16),
        jax.ShapeDtypeStruct((batch, MLA_HEADS, n_tiles, LANES, tm), jnp.bfloat16),
        jax.ShapeDtypeStruct((batch, lp, MLA_HEADS * V_HEAD_DIM), jnp.bfloat16),
    ]
    scratch = [pltpu.VMEM((tm + POOL_HALO, POOL_WIDTH - drop * POOL_GROUP_DIM), jnp.float32)
               for drop in (0, 0, 1, 2)]
    scratch += [pltpu.VMEM((tm, IN_WIDTH_PADDED), jnp.float32)] * 2
    assert len(scratch) == N_MIX_SCRATCH
    if first:
        x, meta = tokens_meta
        seq = x.shape[1]
        body = functools.partial(_mix_in_first_kernel, seq=seq, n_tiles=n_tiles)
        x_spec = pl.BlockSpec(
            (pl.Element(1), pl.Element(tm), pl.Element(d)),
            lambda b, s: (b, pl.multiple_of(jnp.clip(proj(s) * tm - N_META, 0, seq - tm),
                                            SUBLANES), 0))
        lead_specs = [x_spec, _const_spec(meta.shape)]
        lead_args = [x, meta]
        out_specs.append(in_row_spec(d))
        out_shape.append(jax.ShapeDtypeStruct((batch, lp, d), jnp.float32))
    else:
        body = functools.partial(_mix_in_kernel, n_tiles=n_tiles)
        lead_specs = [in_row_spec(d)]
        lead_args = [h]
    return pl.pallas_call(
        body,
        grid=(batch, n_tiles + 1),
        in_specs=lead_specs + [_layer_spec(c, layer) for c in consts]
                 + [table_spec, table_spec, table_t_spec, table_t_spec],
        out_specs=out_specs,
        out_shape=out_shape,
        scratch_shapes=scratch,
        compiler_params=pltpu.CompilerParams(
            dimension_semantics=("arbitrary", "arbitrary"),
            vmem_limit_bytes=VMEM_LIMIT_BYTES),
        name="mix_in",
    )(*lead_args, *consts, *tables)


def _attention(q, k_t, v):
    batch, heads, lp, _ = q.shape
    tq = SEQ_TILE
    hps = HEADS_PER_STEP
    n_chunks = k_t.shape[2]
    assert k_t.shape[4] == tq
    return pl.pallas_call(
        _attn_kernel,
        grid=(batch, heads // hps, lp // tq),
        in_specs=[
            pl.BlockSpec((1, hps, tq, LANES), lambda b, hp, i: (b, hp, i, 0)),
            pl.BlockSpec((1, hps, n_chunks, LANES, tq), lambda b, hp, i: (b, hp, 0, 0, 0)),
            pl.BlockSpec((1, lp, hps * V_HEAD_DIM), lambda b, hp, i: (b, 0, hp)),
        ],
        out_specs=pl.BlockSpec((1, tq, hps * V_HEAD_DIM), lambda b, hp, i: (b, i, hp)),
        out_shape=jax.ShapeDtypeStruct((batch, lp, heads * V_HEAD_DIM), jnp.bfloat16),
        scratch_shapes=[
            pltpu.VMEM((SCORE_BUFFERS, tq, tq), jnp.float32),
            pltpu.VMEM((hps, tq, LANES), jnp.float32),
            pltpu.VMEM((hps, tq, LANES), jnp.float32),
        ],
        compiler_params=pltpu.CompilerParams(
            dimension_semantics=("arbitrary", "arbitrary", "arbitrary"),
            vmem_limit_bytes=VMEM_LIMIT_BYTES),
        name="mla_attn",
    )(q, k_t, v)


def _out_ffn(h2d, ypool2d, ymla2d, wo, ffn_g, wg, wu, wd, layer, real_rows=None):
    rows, d = h2d.shape
    per_call_rows = rows if real_rows is None else real_rows[3]
    tm = max(t for t in FFN_ROW_TILES if per_call_rows % t == 0)
    if real_rows is None:
        grid = (rows // tm,)
        in_row_spec = lambda width: pl.BlockSpec((tm, width), lambda i: (i, 0))
        out_spec = in_row_spec(d)
        out_rows = rows
    else:
        batch, lp, first, count = real_rows
        tiles = count // tm
        assert count % tm == 0 and first % BF16_ROWS == 0 and lp % BF16_ROWS == 0
        grid = (batch, tiles)
        in_row_spec = lambda width: pl.BlockSpec(
            (pl.Element(tm), pl.Element(width)),
            lambda b, j: (pl.multiple_of(b * lp + first + j * tm, BF16_ROWS), 0))
        out_spec = pl.BlockSpec((tm, d), lambda b, j: (b * tiles + j, 0))
        out_rows = batch * count
    return pl.pallas_call(
        _out_ffn_kernel,
        grid=grid,
        in_specs=[
            in_row_spec(d), in_row_spec(POOL_WIDTH), in_row_spec(MLA_HEADS * V_HEAD_DIM),
            _layer_spec(wo, layer), _layer_spec(ffn_g, layer),
            _layer_spec(wg, layer), _layer_spec(wu, layer), _layer_spec(wd, layer),
        ],
        out_specs=out_spec,
        out_shape=jax.ShapeDtypeStruct((out_rows, d), jnp.float32),
        scratch_shapes=[pltpu.VMEM((tm, d), jnp.float32)],
        compiler_params=pltpu.CompilerParams(
            dimension_semantics=("arbitrary",) * len(grid),
            vmem_limit_bytes=VMEM_LIMIT_BYTES),
        name="out_ffn",
    )(h2d, ypool2d, ymla2d, wo, ffn_g, wg, wu, wd)


def _rope_tables(lp):
    inv = 1.0 / (ROPE_THETA ** (jnp.arange(0, QK_ROPE_DIM, 2, dtype=jnp.float32) / QK_ROPE_DIM))
    ang = jnp.arange(lp, dtype=jnp.float32)[:, None] * inv[None, :]
    cos, sin = jnp.cos(ang), jnp.sin(ang)
    zeros = lambda n: jnp.zeros((lp, n), jnp.float32)
    tail = LANES - QK_HEAD_DIM
    cos_t = jnp.concatenate([jnp.ones((lp, QK_NOPE_DIM), jnp.float32), cos, cos, zeros(tail)], axis=1)
    sin_t = jnp.concatenate([zeros(QK_NOPE_DIM), -sin, sin, zeros(tail)], axis=1)
    return cos_t, sin_t, cos.T, sin.T


def _swap_rope_halves(a):
    half = QK_ROPE_DIM // 2
    return jnp.concatenate([jnp.zeros_like(a[..., :QK_NOPE_DIM]),
                            a[..., QK_NOPE_DIM + half:], a[..., QK_NOPE_DIM:QK_NOPE_DIM + half]],
                           axis=-1)


def _pad_lanes(a, width):
    return jnp.pad(a, [(0, 0)] * (a.ndim - 1) + [(0, width - a.shape[-1])])


def kernel(x, meta_tokens, attn_norm_g, w_in, w_pool, pool_scale, q_a_norm_g, w_q_b,
           kv_a_norm_g, w_kv_b, q_norm_g, k_norm_g, w_out, ffn_norm_g, w_gate, w_up, w_down):
    batch, seq, d = x.shape
    depth = w_in.shape[0]
    length = N_META + seq
    lp = -(-length // SEQ_TILE) * SEQ_TILE
    assert (batch * lp) % min(FFN_ROW_TILES) == 0 and D_FF % FF_CHUNK == 0
    bf16 = jnp.bfloat16

    tables = _rope_tables(lp)
    h = None
    wo_all, wg_all, wu_all, wd_all = (w.astype(bf16) for w in (w_out, w_gate, w_up, w_down))
    ffn_g_all = ffn_norm_g[:, None, :]

    s3 = POOL_WIDTH + Q_LORA_RANK + KV_LORA_RANK
    rope_block = jnp.pad(w_in[:, :, s3:], ((0, 0), (0, 0), (QK_NOPE_DIM, LANES - QK_HEAD_DIM)))
    w_in_p = jnp.concatenate([w_in[:, :, :s3], rope_block], axis=2).astype(bf16)
    wq = w_q_b.reshape(depth, Q_LORA_RANK, MLA_HEADS, QK_HEAD_DIM)
    wq_t = _pad_lanes(wq, LANES).reshape(depth, Q_LORA_RANK, MLA_HEADS * LANES)
    wq_t = wq_t.transpose(0, 2, 1).astype(bf16)
    wkv = w_kv_b.reshape(depth, KV_LORA_RANK, MLA_HEADS, QK_NOPE_DIM + V_HEAD_DIM)
    wk_t = wkv[..., :QK_NOPE_DIM].reshape(depth, KV_LORA_RANK, MLA_HEADS * QK_NOPE_DIM)
    wk_t = wk_t.transpose(0, 2, 1).astype(bf16)
    wv = wkv[..., QK_NOPE_DIM:].reshape(depth, KV_LORA_RANK, MLA_HEADS * V_HEAD_DIM).astype(bf16)
    half = QK_ROPE_DIM // 2
    rows_t = lambda g: jnp.broadcast_to(g[:, :, None], g.shape + (SEQ_TILE,))
    gqn_t = rows_t(q_norm_g[:, :QK_NOPE_DIM])
    gq1_t = rows_t(q_norm_g[:, QK_NOPE_DIM:QK_NOPE_DIM + half])
    gq2_t = rows_t(q_norm_g[:, QK_NOPE_DIM + half:])
    gkn_t = jnp.broadcast_to(k_norm_g[:, :QK_NOPE_DIM, None], (depth, QK_NOPE_DIM, SEQ_TILE))
    gkr = jnp.pad(k_norm_g[:, None, QK_NOPE_DIM:],
                  ((0, 0), (0, 0), (QK_NOPE_DIM, LANES - QK_HEAD_DIM)))
    n_pairs = len(POOL_WINDOWS) // 2
    wp = w_pool.reshape(depth, n_pairs, 2, POOL_GROUP_DIM, POOL_GROUP_DIM)
    zero_blk = jnp.zeros_like(wp[:, :, 0])
    w_pool2 = jnp.concatenate([jnp.concatenate([wp[:, :, 0], zero_blk], axis=-1),
                               jnp.concatenate([zero_blk, wp[:, :, 1]], axis=-1)],
                              axis=-2).astype(bf16)
    consts = (attn_norm_g[:, None, :], w_in_p, w_pool2, pool_scale[:, None, :],
              q_a_norm_g[:, None, :], wq_t, kv_a_norm_g[:, None, :], wk_t, wv,
              gqn_t, gq1_t, gq2_t, gkn_t, gkr)

    for l in range(depth):
        if l == 0:
            ypool, q, k_t, v, h = _mix_in(None, consts, l, tables, lp,
                                          tokens_meta=(x, meta_tokens.astype(x.dtype)))
        else:
            ypool, q, k_t, v = _mix_in(h, consts, l, tables, lp)
        ymla = _attention(q, k_t, v)

        last = l == depth - 1
        h = _out_ffn(h.reshape(batch * lp, d), ypool.reshape(batch * lp, POOL_WIDTH),
                     ymla.reshape(batch * lp, MLA_HEADS * V_HEAD_DIM),
                     wo_all, ffn_g_all, wg_all, wu_all, wd_all, l,
                     real_rows=(batch, lp, N_META, seq) if last else None)
        h = h.reshape(batch, seq if last else lp, d)
    return h
```

```python
import functools

import jax
import jax.numpy as jnp
from jax import lax
from jax.experimental import pallas as pl
from jax.experimental.pallas import tpu as pltpu

D_MODEL = 1024
N_META = 16
POOL_WIDTH = 512
POOL_WINDOWS = (2, 4, 8, 16)
POOL_GROUP_DIM = 128
MLA_HEADS = 8
QK_NOPE_DIM = 64
QK_ROPE_DIM = 32
QK_HEAD_DIM = QK_NOPE_DIM + QK_ROPE_DIM
V_HEAD_DIM = 64
Q_LORA_RANK = 384
KV_LORA_RANK = 256
D_FF = 2816
ROPE_THETA = 10000.0
RMS_EPS = 1e-6

LANES = 128
SUBLANES = 8
BF16_ROWS = 2 * SUBLANES
assert POOL_WINDOWS == tuple(2 ** (k + 1) for k in range(len(POOL_WINDOWS)))
POOL_HALO = SUBLANES * len(POOL_WINDOWS)
assert POOL_HALO >= max(POOL_WINDOWS)

SEQ_TILE = 384
FFN_ROW_TILES = (512, 768, 1024)
FF_CHUNK = 256
FF_UNROLL = 5
HEADS_PER_STEP = 8
SCORE_LOOKAHEAD = 1
TAIL_LOOKAHEAD = 2
SCORE_BUFFERS = 4
assert SCORE_LOOKAHEAD <= TAIL_LOOKAHEAD < SCORE_BUFFERS and HEADS_PER_STEP % SCORE_BUFFERS == 0
IN_WIDTH_PADDED = 1280
VMEM_LIMIT_BYTES = 56 * 1024 * 1024

NEG_BIG = -1e30
LOG2_E = 1.4426950408889634


def _rms_scale(x, width):
    return lax.rsqrt(jnp.sum(x * x, axis=-1, keepdims=True) * (1.0 / width) + RMS_EPS)


def _rope(xn, cos_t, sin_lo, sin_hi):
    half = QK_ROPE_DIM // 2
    return (xn * cos_t
            + pltpu.roll(xn, LANES - half, 1) * sin_lo
            + pltpu.roll(xn, half, 1) * sin_hi)


N_MIX_SCRATCH = 6


def _mix_in_kernel(h_ref, *refs, n_tiles):
    _mix_in_steps(lambda: h_ref[0], refs, n_tiles)


def _mix_in_first_kernel(x_ref, meta_ref, *refs, seq, n_tiles):
    h_out_ref = refs[-N_MIX_SCRATCH - 1]
    refs = refs[:-N_MIX_SCRATCH - 1] + refs[-N_MIX_SCRATCH:]

    def load_x():
        i = jnp.minimum(pl.program_id(1), n_tiles - 1)
        blk = x_ref[0]
        tm = blk.shape[0]
        last_shift = (n_tiles - 1) * tm - N_META - (seq - tm)
        first_tile = jnp.concatenate([meta_ref[...], blk[:tm - N_META]], axis=0)
        last_tile = jnp.concatenate(
            [blk[last_shift:], jnp.zeros((last_shift, blk.shape[1]), blk.dtype)], axis=0)
        x = jnp.where(i == 0, first_tile, jnp.where(i == n_tiles - 1, last_tile, blk))
        h_out_ref[0] = x
        return x

    _mix_in_steps(load_x, refs, n_tiles)


def _mix_in_steps(load_x, refs, n_tiles):
    attn_g_ref, w_in_ref, *post_refs, za_ref, zb_ref = refs
    ue_ref = post_refs[-4]
    s = pl.program_id(1)
    even = s % 2 == 0

    @pl.when(s == 1)
    def _():
        ue_ref[0:POOL_HALO, :] = jnp.zeros((POOL_HALO, POOL_WIDTH), jnp.float32)

    def project(z_write):
        x = load_x()
        xn = (x * _rms_scale(x, D_MODEL) * attn_g_ref[0]).astype(jnp.bfloat16)
        z_write[...] = jnp.dot(xn, w_in_ref[0], preferred_element_type=jnp.float32)

    def finish(z_read):
        _mix_in_finish(z_read[...], s - 1, *post_refs)

    def both(z_write, z_read):
        project(z_write)
        finish(z_read)

    inner = jnp.logical_and(s > 0, s < n_tiles)
    pl.when(s == 0)(lambda: project(za_ref))
    pl.when(jnp.logical_and(inner, even))(lambda: both(za_ref, zb_ref))
    pl.when(jnp.logical_and(inner, jnp.logical_not(even)))(lambda: both(zb_ref, za_ref))
    pl.when(s == n_tiles)(lambda: finish(zb_ref if n_tiles % 2 == 0 else za_ref))


def _mix_in_finish(z, i, w_pool_ref, pool_scale_ref,
                   qa_g_ref, wqt_ref, kva_g_ref, wkt_ref, wv_ref,
                   gqn_ref, gq1_ref, gq2_ref, gkn_ref, gkr_ref,
                   cos_ref, sin_ref, cosr_ref, sinr_ref,
                   ypool_ref, q_ref, k_ref, v_ref, ue_ref, s2_ref, s4_ref, s8_ref):
    tm = z.shape[0]
    halo = POOL_HALO

    u = z[:, :POOL_WIDTH]

    gd = POOL_GROUP_DIM
    rows = tm + halo
    ue_ref[halo:, :] = u
    r1, r2, r3, r4 = (k * SUBLANES for k in (1, 2, 3, 4))
    s2_ref[r1:, :] = ue_ref[r1:, :] + ue_ref[r1 - 1:rows - 1, :]
    s4_ref[r2:, :] = s2_ref[r2:, gd:] + s2_ref[r2 - 2:rows - 2, gd:]
    s8_ref[r3:, :] = s4_ref[r3:, gd:] + s4_ref[r3 - 4:rows - 4, gd:]
    s16 = s8_ref[r4:, gd:] + s8_ref[r4 - 8:rows - 8, gd:]
    window_sums = (s2_ref[halo:, 0:gd], s4_ref[halo:, 0:gd], s8_ref[halo:, 0:gd], s16)

    pos = i * tm + lax.broadcasted_iota(jnp.int32, (tm, 1), 0)
    pooled = []
    for g, w in enumerate(POOL_WINDOWS):
        cnt = jnp.minimum(pos + 1, w).astype(jnp.float32)
        pooled.append((window_sums[g] / cnt - u[:, g * gd:(g + 1) * gd]).astype(jnp.bfloat16))
    for pair in range(len(POOL_WINDOWS) // 2):
        sl = slice(2 * pair * gd, 2 * (pair + 1) * gd)
        p2 = jnp.concatenate(pooled[2 * pair:2 * pair + 2], axis=1)
        y2 = jnp.dot(p2, w_pool_ref[0, pair], preferred_element_type=jnp.float32)
        ypool_ref[0, :, sl] = (y2 * pool_scale_ref[0, :, sl]).astype(jnp.bfloat16)
    ue_ref[0:halo, :] = ue_ref[tm:tm + halo, :]

    lane = lax.broadcasted_iota(jnp.int32, (1, LANES), 1)
    rope_mid = QK_NOPE_DIM + QK_ROPE_DIM // 2
    cos_t = cos_ref[...]
    sin_t = sin_ref[...]
    scale = QK_HEAD_DIM ** -0.5 * LOG2_E

    c_q = z[:, POOL_WIDTH:POOL_WIDTH + Q_LORA_RANK]
    cqn = (c_q * _rms_scale(c_q, Q_LORA_RANK) * qa_g_ref[0]).astype(jnp.bfloat16)
    qf_t = lax.dot_general(wqt_ref[0], cqn, (((1,), (1,)), ((), ())),
                           preferred_element_type=jnp.float32)
    cos_r = cosr_ref[...]
    sin_r = sinr_ref[...]
    half = QK_ROPE_DIM // 2
    zero_rows = jnp.zeros((LANES - QK_HEAD_DIM, tm), jnp.float32)
    for hd in range(MLA_HEADS):
        r0 = hd * LANES
        xc = qf_t[r0:r0 + QK_NOPE_DIM, :]
        x1 = qf_t[r0 + QK_NOPE_DIM:r0 + QK_NOPE_DIM + half, :]
        x2 = qf_t[r0 + QK_NOPE_DIM + half:r0 + QK_HEAD_DIM, :]
        ss_n = jnp.sum(xc * xc, axis=0, keepdims=True)
        ss_r = jnp.sum(x1 * x1, axis=0, keepdims=True) + jnp.sum(x2 * x2, axis=0, keepdims=True)
        r_n = lax.rsqrt(ss_n * (1.0 / QK_NOPE_DIM) + RMS_EPS) * scale
        r_r = lax.rsqrt(ss_r * (1.0 / QK_ROPE_DIM) + RMS_EPS) * scale
        a = x1 * gq1_ref[0]
        b = x2 * gq2_ref[0]
        head_t = jnp.concatenate([xc * gqn_ref[0] * r_n,
                                  (a * cos_r - b * sin_r) * r_r,
                                  (a * sin_r + b * cos_r) * r_r,
                                  zero_rows], axis=0)
        q_ref[0, hd] = head_t.T.astype(jnp.bfloat16)

    kv0 = POOL_WIDTH + Q_LORA_RANK
    c_kv = z[:, kv0:kv0 + KV_LORA_RANK]
    ckvn = (c_kv * _rms_scale(c_kv, KV_LORA_RANK) * kva_g_ref[0]).astype(jnp.bfloat16)
    v_ref[0] = jnp.dot(ckvn, wv_ref[0],
                       preferred_element_type=jnp.float32).astype(jnp.bfloat16)

    kr = z[:, kv0 + KV_LORA_RANK:]
    krn = kr * _rms_scale(kr, QK_ROPE_DIM) * gkr_ref[0]
    kr_rot = _rope(krn, cos_t, jnp.where(lane < rope_mid, sin_t, 0.0),
                   jnp.where(lane < rope_mid, 0.0, sin_t))
    kr_rot_t = kr_rot.T[QK_NOPE_DIM:, :].astype(jnp.bfloat16)

    kf_t = lax.dot_general(wkt_ref[0], ckvn, (((1,), (1,)), ((), ())),
                           preferred_element_type=jnp.float32)
    for hd in range(MLA_HEADS):
        kh = kf_t[hd * QK_NOPE_DIM:(hd + 1) * QK_NOPE_DIM, :]
        r_k = lax.rsqrt(jnp.sum(kh * kh, axis=0, keepdims=True) * (1.0 / QK_NOPE_DIM) + RMS_EPS)
        k_ref[0, hd, 0, 0:QK_NOPE_DIM, :] = (kh * r_k * gkn_ref[0]).astype(jnp.bfloat16)
        k_ref[0, hd, 0, QK_NOPE_DIM:, :] = kr_rot_t


def _attn_kernel(q_ref, k_ref, v_ref, o_ref, s_scr, m_scr, acc_scr):
    assert tuple(s_scr.shape[1:]) == (q_ref.shape[2], q_ref.shape[2])
    i = pl.program_id(2)
    tq = q_ref.shape[2]
    tk = s_scr.shape[2]
    n_slabs = tk // LANES
    lane = lax.broadcasted_iota(jnp.int32, (1, LANES), 1)
    low_half = lane < V_HEAD_DIM

    n_buf = s_scr.shape[0]
    heads = HEADS_PER_STEP

    def scores(hh, j):
        s_scr[hh % n_buf] = jnp.dot(q_ref[0, hh], k_ref[0, hh, j],
                                    preferred_element_type=jnp.float32)

    def softmax_pv(hh, j, masked):
        rb = LANES
        if masked:
            tri = (lax.broadcasted_iota(jnp.int32, (rb, LANES), 1)
                   <= lax.broadcasted_iota(jnp.int32, (rb, LANES), 0))
        alphas, ps = [], []
        for rblk in range(tq // rb):
            r0 = rblk * rb
            blk = []
            for c in range(n_slabs if not masked else rblk + 1):
                sl = s_scr[hh % n_buf, r0:r0 + rb, c * LANES:(c + 1) * LANES]
                if masked and c == rblk:
                    sl = jnp.where(tri, sl, NEG_BIG)
                blk.append(sl)
            mx = blk[0]
            for sl in blk[1:]:
                mx = jnp.maximum(mx, sl)
            m_old = m_scr[hh, r0:r0 + rb]
            m_new = jnp.maximum(m_old, jnp.max(mx, axis=-1, keepdims=True))
            m_scr[hh, r0:r0 + rb] = m_new
            alphas.append(jnp.exp2(m_old - m_new))
            p_blk = [jnp.exp2(sl - m_new).astype(jnp.bfloat16) for sl in blk]
            p_blk += [jnp.zeros((rb, LANES), jnp.bfloat16)] * (n_slabs - len(blk))
            ps.append(jnp.concatenate(p_blk, axis=1))
        alpha = jnp.concatenate(alphas, axis=0)
        p = jnp.concatenate(ps, axis=0)
        start = pl.multiple_of(j * tk, tk)
        pair = hh // 2
        vj = v_ref[0, pl.ds(start, tk), pair * LANES:(pair + 1) * LANES]
        own = low_half if hh % 2 == 0 else jnp.logical_not(low_half)
        vj = jnp.where(own, vj, jnp.ones_like(vj))
        acc_scr[hh] = alpha * acc_scr[hh] + jnp.dot(p, vj, preferred_element_type=jnp.float32)

    m_scr[...] = jnp.full(m_scr.shape, NEG_BIG, jnp.float32)
    acc_scr[...] = jnp.zeros(acc_scr.shape, jnp.float32)
    ahead = SCORE_LOOKAHEAD
    for hh in range(ahead):
        scores(hh, 0)

    def chunk(j):
        for hh in range(heads):
            scores((hh + ahead) % heads, j + (hh + ahead) // heads)
            softmax_pv(hh, j, masked=False)

    def chunk_pair(jj, carry):
        chunk(2 * jj)
        chunk(2 * jj + 1)
        return carry

    lax.fori_loop(0, lax.shift_right_logical(i, 1), chunk_pair, 0)

    @pl.when(i % 2 == 1)
    def _():
        chunk(i - 1)

    tail_ahead = TAIL_LOOKAHEAD
    for hh in range(ahead, tail_ahead):
        scores(hh, i)
    def finalize(pair):
        acc_e = acc_scr[2 * pair]
        acc_o = acc_scr[2 * pair + 1]
        out_e = acc_e / acc_e[:, V_HEAD_DIM:V_HEAD_DIM + 1]
        out_o = acc_o / acc_o[:, 0:1]
        o_ref[0, :, pair * LANES:(pair + 1) * LANES] = jnp.where(
            low_half, out_e, out_o).astype(jnp.bfloat16)

    for hh in range(heads):
        if hh + tail_ahead < heads:
            scores(hh + tail_ahead, i)
        softmax_pv(hh, i, masked=True)
        if hh % 2 == 1:
            finalize(hh // 2)


def _out_ffn_kernel(h_ref, ypool_ref, ymla_ref, wo_ref, ffn_g_ref,
                    wg_ref, wu_ref, wd_ref, o_ref, acc_ref):
    h1 = (h_ref[...]
          + jnp.dot(ypool_ref[...], wo_ref[0, :POOL_WIDTH, :], preferred_element_type=jnp.float32)
          + jnp.dot(ymla_ref[...], wo_ref[0, POOL_WIDTH:, :], preferred_element_type=jnp.float32))
    g = (h1 * _rms_scale(h1, D_MODEL) * ffn_g_ref[0]).astype(jnp.bfloat16)
    acc_ref[...] = h1

    def chunk_slice(c):
        return pl.ds(pl.multiple_of(c * FF_CHUNK, FF_CHUNK), FF_CHUNK)

    def gate_up(c):
        cols = chunk_slice(c)
        gate = jnp.dot(g, wg_ref[0, :, cols], preferred_element_type=jnp.float32)
        up = jnp.dot(g, wu_ref[0, :, cols], preferred_element_type=jnp.float32)
        return (gate * (1.0 / (1.0 + jnp.exp(-gate))) * up).astype(jnp.bfloat16)

    def down(act, c):
        acc_ref[...] += jnp.dot(act, wd_ref[0, chunk_slice(c), :],
                                preferred_element_type=jnp.float32)

    n_chunks = wg_ref.shape[2] // FF_CHUNK

    def chunk_group(first, count):
        act = gate_up(first)
        for k in range(count):
            nxt = gate_up(first + k + 1) if k + 1 < count else None
            down(act, first + k)
            act = nxt

    def loop_step(step, carry):
        chunk_group(step * FF_UNROLL, FF_UNROLL)
        return carry

    n_steps = n_chunks // FF_UNROLL
    lax.fori_loop(0, n_steps, loop_step, 0)
    if n_chunks % FF_UNROLL:
        chunk_group(n_steps * FF_UNROLL, n_chunks % FF_UNROLL)
    o_ref[...] = acc_ref[...]


def _const_spec(shape):
    nd = len(shape)
    return pl.BlockSpec(shape, lambda *_: (0,) * nd)


def _layer_spec(stacked, layer):
    shape = (1,) + stacked.shape[1:]
    nd = len(shape)
    return pl.BlockSpec(shape, lambda *_: (layer,) + (0,) * (nd - 1),
                        pipeline_mode=pl.Buffered(1))


def _mix_in(h, consts, layer, tables, lp, tokens_meta=None):
    first = tokens_meta is not None
    batch = tokens_meta[0].shape[0] if first else h.shape[0]
    d = D_MODEL
    tm = SEQ_TILE
    n_tiles = lp // tm
    proj = lambda s: jnp.minimum(s, n_tiles - 1)
    done = lambda s: jnp.maximum(s - 1, 0)
    in_row_spec = lambda width: pl.BlockSpec((1, tm, width), lambda b, s: (b, proj(s), 0))
    row_spec = lambda width: pl.BlockSpec((1, tm, width), lambda b, s: (b, done(s), 0))
    q_spec = pl.BlockSpec((1, MLA_HEADS, tm, LANES), lambda b, s: (b, 0, done(s), 0))
    kt_spec = pl.BlockSpec((1, MLA_HEADS, 1, LANES, tm), lambda b, s: (b, 0, done(s), 0, 0))
    table_spec = pl.BlockSpec((tm, LANES), lambda b, s: (done(s), 0))
    table_t_spec = pl.BlockSpec((QK_ROPE_DIM // 2, tm), lambda b, s: (0, done(s)))
    out_specs = [row_spec(POOL_WIDTH), q_spec, kt_spec, row_spec(MLA_HEADS * V_HEAD_DIM)]
    out_shape = [
        jax.ShapeDtypeStruct((batch, lp, POOL_WIDTH), jnp.bfloat16),
        jax.ShapeDtypeStruct((batch, MLA_HEADS, lp, LANES), jnp.bfloat16),
        jax.ShapeDtypeStruct((batch, MLA_HEADS, n_tiles, LANES, tm), jnp.bfloat16),
        jax.ShapeDtypeStruct((batch, lp, MLA_HEADS * V_HEAD_DIM), jnp.bfloat16),
    ]
    scratch = [pltpu.VMEM((tm + POOL_HALO, POOL_WIDTH - drop * POOL_GROUP_DIM), jnp.float32)
               for drop in (0, 0, 1, 2)]
    scratch += [pltpu.VMEM((tm, IN_WIDTH_PADDED), jnp.float32)] * 2
    assert len(scratch) == N_MIX_SCRATCH
    if first:
        x, meta = tokens_meta
        seq = x.shape[1]
        body = functools.partial(_mix_in_first_kernel, seq=seq, n_tiles=n_tiles)
        x_spec = pl.BlockSpec(
            (pl.Element(1), pl.Element(tm), pl.Element(d)),
            lambda b, s: (b, pl.multiple_of(jnp.clip(proj(s) * tm - N_META, 0, seq - tm),
                                            SUBLANES), 0))
        lead_specs = [x_spec, _const_spec(meta.shape)]
        lead_args = [x, meta]
        out_specs.append(in_row_spec(d))
        out_shape.append(jax.ShapeDtypeStruct((batch, lp, d), jnp.float32))
    else:
        body = functools.partial(_mix_in_kernel, n_tiles=n_tiles)
        lead_specs = [in_row_spec(d)]
        lead_args = [h]
    return pl.pallas_call(
        body,
        grid=(batch, n_tiles + 1),
        in_specs=lead_specs + [_layer_spec(c, layer) for c in consts]
                 + [table_spec, table_spec, table_t_spec, table_t_spec],
        out_specs=out_specs,
        out_shape=out_shape,
        scratch_shapes=scratch,
        compiler_params=pltpu.CompilerParams(
            dimension_semantics=("arbitrary", "arbitrary"),
            vmem_limit_bytes=VMEM_LIMIT_BYTES),
        name="mix_in",
    )(*lead_args, *consts, *tables)


def _attention(q, k_t, v):
    batch, heads, lp, _ = q.shape
    tq = SEQ_TILE
    hps = HEADS_PER_STEP
    n_chunks = k_t.shape[2]
    assert k_t.shape[4] == tq
    return pl.pallas_call(
        _attn_kernel,
        grid=(batch, heads // hps, lp // tq),
        in_specs=[
            pl.BlockSpec((1, hps, tq, LANES), lambda b, hp, i: (b, hp, i, 0)),
            pl.BlockSpec((1, hps, n_chunks, LANES, tq), lambda b, hp, i: (b, hp, 0, 0, 0)),
            pl.BlockSpec((1, lp, hps * V_HEAD_DIM), lambda b, hp, i: (b, 0, hp)),
        ],
        out_specs=pl.BlockSpec((1, tq, hps * V_HEAD_DIM), lambda b, hp, i: (b, i, hp)),
        out_shape=jax.ShapeDtypeStruct((batch, lp, heads * V_HEAD_DIM), jnp.bfloat16),
        scratch_shapes=[
            pltpu.VMEM((SCORE_BUFFERS, tq, tq), jnp.float32),
            pltpu.VMEM((hps, tq, LANES), jnp.float32),
            pltpu.VMEM((hps, tq, LANES), jnp.float32),
        ],
        compiler_params=pltpu.CompilerParams(
            dimension_semantics=("arbitrary", "arbitrary", "arbitrary"),
            vmem_limit_bytes=VMEM_LIMIT_BYTES),
        name="mla_attn",
    )(q, k_t, v)


def _out_ffn(h2d, ypool2d, ymla2d, wo, ffn_g, wg, wu, wd, layer, real_rows=None):
    rows, d = h2d.shape
    per_call_rows = rows if real_rows is None else real_rows[3]
    tm = max(t for t in FFN_ROW_TILES if per_call_rows % t == 0)
    if real_rows is None:
        grid = (rows // tm,)
        in_row_spec = lambda width: pl.BlockSpec((tm, width), lambda i: (i, 0))
        out_spec = in_row_spec(d)
        out_rows = rows
    else:
        batch, lp, first, count = real_rows
        tiles = count // tm
        assert count % tm == 0 and first % BF16_ROWS == 0 and lp % BF16_ROWS == 0
        grid = (batch, tiles)
        in_row_spec = lambda width: pl.BlockSpec(
            (pl.Element(tm), pl.Element(width)),
            lambda b, j: (pl.multiple_of(b * lp + first + j * tm, BF16_ROWS), 0))
        out_spec = pl.BlockSpec((tm, d), lambda b, j: (b * tiles + j, 0))
        out_rows = batch * count
    return pl.pallas_call(
        _out_ffn_kernel,
        grid=grid,
        in_specs=[
            in_row_spec(d), in_row_spec(POOL_WIDTH), in_row_spec(MLA_HEADS * V_HEAD_DIM),
            _layer_spec(wo, layer), _layer_spec(ffn_g, layer),
            _layer_spec(wg, layer), _layer_spec(wu, layer), _layer_spec(wd, layer),
        ],
        out_specs=out_spec,
        out_shape=jax.ShapeDtypeStruct((out_rows, d), jnp.float32),
        scratch_shapes=[pltpu.VMEM((tm, d), jnp.float32)],
        compiler_params=pltpu.CompilerParams(
            dimension_semantics=("arbitrary",) * len(grid),
            vmem_limit_bytes=VMEM_LIMIT_BYTES),
        name="out_ffn",
    )(h2d, ypool2d, ymla2d, wo, ffn_g, wg, wu, wd)


def _rope_tables(lp):
    inv = 1.0 / (ROPE_THETA ** (jnp.arange(0, QK_ROPE_DIM, 2, dtype=jnp.float32) / QK_ROPE_DIM))
    ang = jnp.arange(lp, dtype=jnp.float32)[:, None] * inv[None, :]
    cos, sin = jnp.cos(ang), jnp.sin(ang)
    zeros = lambda n: jnp.zeros((lp, n), jnp.float32)
    tail = LANES - QK_HEAD_DIM
    cos_t = jnp.concatenate([jnp.ones((lp, QK_NOPE_DIM), jnp.float32), cos, cos, zeros(tail)], axis=1)
    sin_t = jnp.concatenate([zeros(QK_NOPE_DIM), -sin, sin, zeros(tail)], axis=1)
    return cos_t, sin_t, cos.T, sin.T


def _pad_lanes(a, width):
    return jnp.pad(a, [(0, 0)] * (a.ndim - 1) + [(0, width - a.shape[-1])])


def kernel(x, meta_tokens, attn_norm_g, w_in, w_pool, pool_scale, q_a_norm_g, w_q_b,
           kv_a_norm_g, w_kv_b, q_norm_g, k_norm_g, w_out, ffn_norm_g, w_gate, w_up, w_down):
    batch, seq, d = x.shape
    depth = w_in.shape[0]
    length = N_META + seq
    lp = -(-length // SEQ_TILE) * SEQ_TILE
    assert (batch * lp) % min(FFN_ROW_TILES) == 0 and D_FF % FF_CHUNK == 0
    bf16 = jnp.bfloat16

    tables = _rope_tables(lp)
    h = None
    wo_all, wg_all, wu_all, wd_all = (w.astype(bf16) for w in (w_out, w_gate, w_up, w_down))
    ffn_g_all = ffn_norm_g[:, None, :]

    s3 = POOL_WIDTH + Q_LORA_RANK + KV_LORA_RANK
    rope_block = jnp.pad(w_in[:, :, s3:], ((0, 0), (0, 0), (QK_NOPE_DIM, LANES - QK_HEAD_DIM)))
    w_in_p = jnp.concatenate([w_in[:, :, :s3], rope_block], axis=2).astype(bf16)
    wq = w_q_b.reshape(depth, Q_LORA_RANK, MLA_HEADS, QK_HEAD_DIM)
    wq_t = _pad_lanes(wq, LANES).reshape(depth, Q_LORA_RANK, MLA_HEADS * LANES)
    wq_t = wq_t.transpose(0, 2, 1).astype(bf16)
    wkv = w_kv_b.reshape(depth, KV_LORA_RANK, MLA_HEADS, QK_NOPE_DIM + V_HEAD_DIM)
    wk_t = wkv[..., :QK_NOPE_DIM].reshape(depth, KV_LORA_RANK, MLA_HEADS * QK_NOPE_DIM)
    wk_t = wk_t.transpose(0, 2, 1).astype(bf16)
    wv = wkv[..., QK_NOPE_DIM:].reshape(depth, KV_LORA_RANK, MLA_HEADS * V_HEAD_DIM).astype(bf16)
    half = QK_ROPE_DIM // 2
    rows_t = lambda g: jnp.broadcast_to(g[:, :, None], g.shape + (SEQ_TILE,))
    gqn_t = rows_t(q_norm_g[:, :QK_NOPE_DIM])
    gq1_t = rows_t(q_norm_g[:, QK_NOPE_DIM:QK_NOPE_DIM + half])
    gq2_t = rows_t(q_norm_g[:, QK_NOPE_DIM + half:])
    gkn_t = jnp.broadcast_to(k_norm_g[:, :QK_NOPE_DIM, None], (depth, QK_NOPE_DIM, SEQ_TILE))
    gkr = jnp.pad(k_norm_g[:, None, QK_NOPE_DIM:],
                  ((0, 0), (0, 0), (QK_NOPE_DIM, LANES - QK_HEAD_DIM)))
    n_pairs = len(POOL_WINDOWS) // 2
    wp = w_pool.reshape(depth, n_pairs, 2, POOL_GROUP_DIM, POOL_GROUP_DIM)
    zero_blk = jnp.zeros_like(wp[:, :, 0])
    w_pool2 = jnp.concatenate([jnp.concatenate([wp[:, :, 0], zero_blk], axis=-1),
                               jnp.concatenate([zero_blk, wp[:, :, 1]], axis=-1)],
                              axis=-2).astype(bf16)
    consts = (attn_norm_g[:, None, :], w_in_p, w_pool2, pool_scale[:, None, :],
              q_a_norm_g[:, None, :], wq_t, kv_a_norm_g[:, None, :], wk_t, wv,
              gqn_t, gq1_t, gq2_t, gkn_t, gkr)

    for l in range(depth):
        if l == 0:
            ypool, q, k_t, v, h = _mix_in(None, consts, l, tables, lp,
                                          tokens_meta=(x, meta_tokens.astype(x.dtype)))
        else:
            ypool, q, k_t, v = _mix_in(h, consts, l, tables, lp)
        ymla = _attention(q, k_t, v)

        last = l == depth - 1
        h = _out_ffn(h.reshape(batch * lp, d), ypool.reshape(batch * lp, POOL_WIDTH),
                     ymla.reshape(batch * lp, MLA_HEADS * V_HEAD_DIM),
                     wo_all, ffn_g_all, wg_all, wu_all, wd_all, l,
                     real_rows=(batch, lp, N_META, seq) if last else None)
        h = h.reshape(batch, seq if last else lp, d)
    return h
```

```python
import functools

import jax
import jax.numpy as jnp
from jax import lax
from jax.experimental import pallas as pl
from jax.experimental.pallas import tpu as pltpu

D_MODEL = 1024
N_META = 16
POOL_WIDTH = 512
POOL_WINDOWS = (2, 4, 8, 16)
POOL_GROUP_DIM = 128
MLA_HEADS = 8
QK_NOPE_DIM = 64
QK_ROPE_DIM = 32
QK_HEAD_DIM = QK_NOPE_DIM + QK_ROPE_DIM
V_HEAD_DIM = 64
Q_LORA_RANK = 384
KV_LORA_RANK = 256
D_FF = 2816
ROPE_THETA = 10000.0
RMS_EPS = 1e-6

LANES = 128
SUBLANES = 8
BF16_ROWS = 2 * SUBLANES
assert POOL_WINDOWS == tuple(2 ** (k + 1) for k in range(len(POOL_WINDOWS)))
POOL_HALO = SUBLANES * len(POOL_WINDOWS)
assert POOL_HALO >= max(POOL_WINDOWS)

SEQ_TILE = 384
FFN_ROW_TILES = (512, 768, 1024)
FF_CHUNK = 256
FF_UNROLL = 5
HEADS_PER_STEP = 8
SCORE_LOOKAHEAD = 1
TAIL_LOOKAHEAD = 2
SCORE_BUFFERS = 4
assert SCORE_LOOKAHEAD <= TAIL_LOOKAHEAD < SCORE_BUFFERS and HEADS_PER_STEP % SCORE_BUFFERS == 0
IN_WIDTH_PADDED = 1280
VMEM_LIMIT_BYTES = 56 * 1024 * 1024

NEG_BIG = -1e30
LOG2_E = 1.4426950408889634


def _rms_scale(x, width):
    return lax.rsqrt(jnp.sum(x * x, axis=-1, keepdims=True) * (1.0 / width) + RMS_EPS)


def _rope(xn, cos_t, sin_lo, sin_hi):
    half = QK_ROPE_DIM // 2
    return (xn * cos_t
            + pltpu.roll(xn, LANES - half, 1) * sin_lo
            + pltpu.roll(xn, half, 1) * sin_hi)


N_MIX_SCRATCH = 6


def _mix_in_kernel(h_ref, *refs, n_tiles):
    _mix_in_steps(lambda: h_ref[0], refs, n_tiles)


def _mix_in_first_kernel(x_ref, meta_ref, *refs, seq, n_tiles):
    h_out_ref = refs[-N_MIX_SCRATCH - 1]
    refs = refs[:-N_MIX_SCRATCH - 1] + refs[-N_MIX_SCRATCH:]

    def load_x():
        i = jnp.minimum(pl.program_id(1), n_tiles - 1)
        blk = x_ref[0]
        tm = blk.shape[0]
        last_shift = (n_tiles - 1) * tm - N_META - (seq - tm)
        first_tile = jnp.concatenate([meta_ref[...], blk[:tm - N_META]], axis=0)
        last_tile = jnp.concatenate(
            [blk[last_shift:], jnp.zeros((last_shift, blk.shape[1]), blk.dtype)], axis=0)
        x = jnp.where(i == 0, first_tile, jnp.where(i == n_tiles - 1, last_tile, blk))
        h_out_ref[0] = x
        return x

    _mix_in_steps(load_x, refs, n_tiles)


def _mix_in_steps(load_x, refs, n_tiles):
    attn_g_ref, w_in_ref, *post_refs, za_ref, zb_ref = refs
    ue_ref = post_refs[-4]
    s = pl.program_id(1)
    even = s % 2 == 0

    @pl.when(s == 1)
    def _():
        ue_ref[0:POOL_HALO, :] = jnp.zeros((POOL_HALO, POOL_WIDTH), jnp.float32)

    def project(z_write):
        x = load_x()
        xn = (x * _rms_scale(x, D_MODEL) * attn_g_ref[0]).astype(jnp.bfloat16)
        z_write[...] = jnp.dot(xn, w_in_ref[0], preferred_element_type=jnp.float32)

    def finish(z_read):
        _mix_in_finish(z_read[...], s - 1, *post_refs)

    def both(z_write, z_read):
        project(z_write)
        finish(z_read)

    inner = jnp.logical_and(s > 0, s < n_tiles)
    pl.when(s == 0)(lambda: project(za_ref))
    pl.when(jnp.logical_and(inner, even))(lambda: both(za_ref, zb_ref))
    pl.when(jnp.logical_and(inner, jnp.logical_not(even)))(lambda: both(zb_ref, za_ref))
    pl.when(s == n_tiles)(lambda: finish(zb_ref if n_tiles % 2 == 0 else za_ref))


def _mix_in_finish(z, i, w_pool_ref, pool_scale_ref,
                   qa_g_ref, wqt_ref, kva_g_ref, wkt_ref, wv_ref,
                   gqn_ref, gq1_ref, gq2_ref, gkn_ref, gkr_ref,
                   cos_ref, sin_ref, cosr_ref, sinr_ref,
                   ypool_ref, q_ref, k_ref, v_ref, ue_ref, s2_ref, s4_ref, s8_ref):
    tm = z.shape[0]
    halo = POOL_HALO

    u = z[:, :POOL_WIDTH]

    gd = POOL_GROUP_DIM
    rows = tm + halo
    ue_ref[halo:, :] = u
    r1, r2, r3, r4 = (k * SUBLANES for k in (1, 2, 3, 4))
    s2_ref[r1:, :] = ue_ref[r1:, :] + ue_ref[r1 - 1:rows - 1, :]
    s4_ref[r2:, :] = s2_ref[r2:, gd:] + s2_ref[r2 - 2:rows - 2, gd:]
    s8_ref[r3:, :] = s4_ref[r3:, gd:] + s4_ref[r3 - 4:rows - 4, gd:]
    s16 = s8_ref[r4:, gd:] + s8_ref[r4 - 8:rows - 8, gd:]
    window_sums = (s2_ref[halo:, 0:gd], s4_ref[halo:, 0:gd], s8_ref[halo:, 0:gd], s16)

    pos = i * tm + lax.broadcasted_iota(jnp.int32, (tm, 1), 0)
    pooled = []
    for g, w in enumerate(POOL_WINDOWS):
        cnt = jnp.minimum(pos + 1, w).astype(jnp.float32)
        pooled.append((window_sums[g] / cnt - u[:, g * gd:(g + 1) * gd]).astype(jnp.bfloat16))
    for pair in range(len(POOL_WINDOWS) // 2):
        sl = slice(2 * pair * gd, 2 * (pair + 1) * gd)
        p2 = jnp.concatenate(pooled[2 * pair:2 * pair + 2], axis=1)
        y2 = jnp.dot(p2, w_pool_ref[0, pair], preferred_element_type=jnp.float32)
        ypool_ref[0, :, sl] = (y2 * pool_scale_ref[0, :, sl]).astype(jnp.bfloat16)
    ue_ref[0:halo, :] = ue_ref[tm:tm + halo, :]

    lane = lax.broadcasted_iota(jnp.int32, (1, LANES), 1)
    rope_mid = QK_NOPE_DIM + QK_ROPE_DIM // 2
    cos_t = cos_ref[...]
    sin_t = sin_ref[...]
    scale = QK_HEAD_DIM ** -0.5 * LOG2_E

    c_q = z[:, POOL_WIDTH:POOL_WIDTH + Q_LORA_RANK]
    cqn = (c_q * _rms_scale(c_q, Q_LORA_RANK) * qa_g_ref[0]).astype(jnp.bfloat16)
    qf_t = lax.dot_general(wqt_ref[0], cqn, (((1,), (1,)), ((), ())),
                           preferred_element_type=jnp.float32)
    cos_r = cosr_ref[...]
    sin_r = sinr_ref[...]
    half = QK_ROPE_DIM // 2
    zero_rows = jnp.zeros((LANES - QK_HEAD_DIM, tm), jnp.float32)
    for hd in range(MLA_HEADS):
        r0 = hd * QK_HEAD_DIM
        xc = qf_t[r0:r0 + QK_NOPE_DIM, :]
        x1 = qf_t[r0 + QK_NOPE_DIM:r0 + QK_NOPE_DIM + half, :]
        x2 = qf_t[r0 + QK_NOPE_DIM + half:r0 + QK_HEAD_DIM, :]
        ss_n = jnp.sum(xc * xc, axis=0, keepdims=True)
        ss_r = jnp.sum(x1 * x1, axis=0, keepdims=True) + jnp.sum(x2 * x2, axis=0, keepdims=True)
        r_n = lax.rsqrt(ss_n * (1.0 / QK_NOPE_DIM) + RMS_EPS) * scale
        r_r = lax.rsqrt(ss_r * (1.0 / QK_ROPE_DIM) + RMS_EPS) * scale
        a = x1 * gq1_ref[0]
        b = x2 * gq2_ref[0]
        head_t = jnp.concatenate([xc * gqn_ref[0] * r_n,
                                  (a * cos_r - b * sin_r) * r_r,
                                  (a * sin_r + b * cos_r) * r_r,
                                  zero_rows], axis=0)
        q_ref[0, hd] = head_t.T.astype(jnp.bfloat16)

    kv0 = POOL_WIDTH + Q_LORA_RANK
    c_kv = z[:, kv0:kv0 + KV_LORA_RANK]
    ckvn = (c_kv * _rms_scale(c_kv, KV_LORA_RANK) * kva_g_ref[0]).astype(jnp.bfloat16)
    v_ref[0] = jnp.dot(ckvn, wv_ref[0],
                       preferred_element_type=jnp.float32).astype(jnp.bfloat16)

    kr = z[:, kv0 + KV_LORA_RANK:]
    krn = kr * _rms_scale(kr, QK_ROPE_DIM) * gkr_ref[0]
    kr_rot = _rope(krn, cos_t, jnp.where(lane < rope_mid, sin_t, 0.0),
                   jnp.where(lane < rope_mid, 0.0, sin_t))
    kr_rot_t = kr_rot.T[QK_NOPE_DIM:, :].astype(jnp.bfloat16)

    kf_t = lax.dot_general(wkt_ref[0], ckvn, (((1,), (1,)), ((), ())),
                           preferred_element_type=jnp.float32)
    for hd in range(MLA_HEADS):
        kh = kf_t[hd * QK_NOPE_DIM:(hd + 1) * QK_NOPE_DIM, :]
        r_k = lax.rsqrt(jnp.sum(kh * kh, axis=0, keepdims=True) * (1.0 / QK_NOPE_DIM) + RMS_EPS)
        k_ref[0, hd, 0, 0:QK_NOPE_DIM, :] = (kh * r_k * gkn_ref[0]).astype(jnp.bfloat16)
        k_ref[0, hd, 0, QK_NOPE_DIM:, :] = kr_rot_t


def _attn_kernel(q_ref, k_ref, v_ref, o_ref, s_scr, m_scr, acc_scr):
    assert tuple(s_scr.shape[1:]) == (q_ref.shape[2], q_ref.shape[2])
    i = pl.program_id(2)
    tq = q_ref.shape[2]
    tk = s_scr.shape[2]
    n_slabs = tk // LANES
    lane = lax.broadcasted_iota(jnp.int32, (1, LANES), 1)
    low_half = lane < V_HEAD_DIM

    n_buf = s_scr.shape[0]
    heads = HEADS_PER_STEP

    def scores(hh, j):
        s_scr[hh % n_buf] = jnp.dot(q_ref[0, hh], k_ref[0, hh, j],
                                    preferred_element_type=jnp.float32)

    def softmax_pv(hh, j, masked):
        rb = LANES
        if masked:
            tri = (lax.broadcasted_iota(jnp.int32, (rb, LANES), 1)
                   <= lax.broadcasted_iota(jnp.int32, (rb, LANES), 0))
        alphas, ps = [], []
        for rblk in range(tq // rb):
            r0 = rblk * rb
            blk = []
            for c in range(n_slabs if not masked else rblk + 1):
                sl = s_scr[hh % n_buf, r0:r0 + rb, c * LANES:(c + 1) * LANES]
                if masked and c == rblk:
                    sl = jnp.where(tri, sl, NEG_BIG)
                blk.append(sl)
            mx = blk[0]
            for sl in blk[1:]:
                mx = jnp.maximum(mx, sl)
            m_old = m_scr[hh, r0:r0 + rb]
            m_new = jnp.maximum(m_old, jnp.max(mx, axis=-1, keepdims=True))
            m_scr[hh, r0:r0 + rb] = m_new
            alphas.append(jnp.exp2(m_old - m_new))
            p_blk = [jnp.exp2(sl - m_new).astype(jnp.bfloat16) for sl in blk]
            p_blk += [jnp.zeros((rb, LANES), jnp.bfloat16)] * (n_slabs - len(blk))
            ps.append(jnp.concatenate(p_blk, axis=1))
        alpha = jnp.concatenate(alphas, axis=0)
        p = jnp.concatenate(ps, axis=0)
        start = pl.multiple_of(j * tk, tk)
        pair = hh // 2
        vj = v_ref[0, pl.ds(start, tk), pair * LANES:(pair + 1) * LANES]
        own = low_half if hh % 2 == 0 else jnp.logical_not(low_half)
        vj = jnp.where(own, vj, jnp.ones_like(vj))
        acc_scr[hh] = alpha * acc_scr[hh] + jnp.dot(p, vj, preferred_element_type=jnp.float32)

    m_scr[...] = jnp.full(m_scr.shape, NEG_BIG, jnp.float32)
    acc_scr[...] = jnp.zeros(acc_scr.shape, jnp.float32)
    ahead = SCORE_LOOKAHEAD
    for hh in range(ahead):
        scores(hh, 0)

    def chunk(j):
        for hh in range(heads):
            scores((hh + ahead) % heads, j + (hh + ahead) // heads)
            softmax_pv(hh, j, masked=False)

    def chunk_pair(jj, carry):
        chunk(2 * jj)
        chunk(2 * jj + 1)
        return carry

    lax.fori_loop(0, lax.shift_right_logical(i, 1), chunk_pair, 0)

    @pl.when(i % 2 == 1)
    def _():
        chunk(i - 1)

    tail_ahead = TAIL_LOOKAHEAD
    for hh in range(ahead, tail_ahead):
        scores(hh, i)
    def finalize(pair):
        acc_e = acc_scr[2 * pair]
        acc_o = acc_scr[2 * pair + 1]
        out_e = acc_e / acc_e[:, V_HEAD_DIM:V_HEAD_DIM + 1]
        out_o = acc_o / acc_o[:, 0:1]
        o_ref[0, :, pair * LANES:(pair + 1) * LANES] = jnp.where(
            low_half, out_e, out_o).astype(jnp.bfloat16)

    for hh in range(heads):
        if hh + tail_ahead < heads:
            scores(hh + tail_ahead, i)
        softmax_pv(hh, i, masked=True)
        if hh % 2 == 1:
            finalize(hh // 2)


def _out_ffn_kernel(h_ref, ypool_ref, ymla_ref, wo_ref, ffn_g_ref,
                    wg_ref, wu_ref, wd_ref, o_ref, acc_ref):
    h1 = (h_ref[...]
          + jnp.dot(ypool_ref[...], wo_ref[0, :POOL_WIDTH, :], preferred_element_type=jnp.float32)
          + jnp.dot(ymla_ref[...], wo_ref[0, POOL_WIDTH:, :], preferred_element_type=jnp.float32))
    g = (h1 * _rms_scale(h1, D_MODEL) * ffn_g_ref[0]).astype(jnp.bfloat16)
    acc_ref[...] = h1

    def chunk_slice(c):
        return pl.ds(pl.multiple_of(c * FF_CHUNK, FF_CHUNK), FF_CHUNK)

    def gate_up(c):
        cols = chunk_slice(c)
        gate = jnp.dot(g, wg_ref[0, :, cols], preferred_element_type=jnp.float32)
        up = jnp.dot(g, wu_ref[0, :, cols], preferred_element_type=jnp.float32)
        return (gate * (1.0 / (1.0 + jnp.exp(-gate))) * up).astype(jnp.bfloat16)

    def down(act, c):
        acc_ref[...] += jnp.dot(act, wd_ref[0, chunk_slice(c), :],
                                preferred_element_type=jnp.float32)

    n_chunks = wg_ref.shape[2] // FF_CHUNK

    def chunk_group(first, count):
        act = gate_up(first)
        for k in range(count):
            nxt = gate_up(first + k + 1) if k + 1 < count else None
            down(act, first + k)
            act = nxt

    def loop_step(step, carry):
        chunk_group(step * FF_UNROLL, FF_UNROLL)
        return carry

    n_steps = n_chunks // FF_UNROLL
    lax.fori_loop(0, n_steps, loop_step, 0)
    if n_chunks % FF_UNROLL:
        chunk_group(n_steps * FF_UNROLL, n_chunks % FF_UNROLL)
    o_ref[...] = acc_ref[...]


def _const_spec(shape):
    nd = len(shape)
    return pl.BlockSpec(shape, lambda *_: (0,) * nd)


def _layer_spec(stacked, layer):
    shape = (1,) + stacked.shape[1:]
    nd = len(shape)
    return pl.BlockSpec(shape, lambda *_: (layer,) + (0,) * (nd - 1),
                        pipeline_mode=pl.Buffered(1))


def _mix_in(h, consts, layer, tables, lp, tokens_meta=None):
    first = tokens_meta is not None
    batch = tokens_meta[0].shape[0] if first else h.shape[0]
    d = D_MODEL
    tm = SEQ_TILE
    n_tiles = lp // tm
    proj = lambda s: jnp.minimum(s, n_tiles - 1)
    done = lambda s: jnp.maximum(s - 1, 0)
    in_row_spec = lambda width: pl.BlockSpec((1, tm, width), lambda b, s: (b, proj(s), 0))
    row_spec = lambda width: pl.BlockSpec((1, tm, width), lambda b, s: (b, done(s), 0))
    q_spec = pl.BlockSpec((1, MLA_HEADS, tm, LANES), lambda b, s: (b, 0, done(s), 0))
    kt_spec = pl.BlockSpec((1, MLA_HEADS, 1, LANES, tm), lambda b, s: (b, 0, done(s), 0, 0))
    table_spec = pl.BlockSpec((tm, LANES), lambda b, s: (done(s), 0))
    table_t_spec = pl.BlockSpec((QK_ROPE_DIM // 2, tm), lambda b, s: (0, done(s)))
    out_specs = [row_spec(POOL_WIDTH), q_spec, kt_spec, row_spec(MLA_HEADS * V_HEAD_DIM)]
    out_shape = [
        jax.ShapeDtypeStruct((batch, lp, POOL_WIDTH), jnp.bfloat16),
        jax.ShapeDtypeStruct((batch, MLA_HEADS, lp, LANES), jnp.bfloat16),
        jax.ShapeDtypeStruct((batch, MLA_HEADS, n_tiles, LANES, tm), jnp.bfloat16),
        jax.ShapeDtypeStruct((batch, lp, MLA_HEADS * V_HEAD_DIM), jnp.bfloat16),
    ]
    scratch = [pltpu.VMEM((tm + POOL_HALO, POOL_WIDTH - drop * POOL_GROUP_DIM), jnp.float32)
               for drop in (0, 0, 1, 2)]
    scratch += [pltpu.VMEM((tm, IN_WIDTH_PADDED), jnp.float32)] * 2
    assert len(scratch) == N_MIX_SCRATCH
    if first:
        x, meta = tokens_meta
        seq = x.shape[1]
        body = functools.partial(_mix_in_first_kernel, seq=seq, n_tiles=n_tiles)
        x_spec = pl.BlockSpec(
            (pl.Element(1), pl.Element(tm), pl.Element(d)),
            lambda b, s: (b, pl.multiple_of(jnp.clip(proj(s) * tm - N_META, 0, seq - tm),
                                            SUBLANES), 0))
        lead_specs = [x_spec, _const_spec(meta.shape)]
        lead_args = [x, meta]
        out_specs.append(in_row_spec(d))
        out_shape.append(jax.ShapeDtypeStruct((batch, lp, d), jnp.float32))
    else:
        body = functools.partial(_mix_in_kernel, n_tiles=n_tiles)
        lead_specs = [in_row_spec(d)]
        lead_args = [h]
    return pl.pallas_call(
        body,
        grid=(batch, n_tiles + 1),
        in_specs=lead_specs + [_layer_spec(c, layer) for c in consts]
                 + [table_spec, table_spec, table_t_spec, table_t_spec],
        out_specs=out_specs,
        out_shape=out_shape,
        scratch_shapes=scratch,
        compiler_params=pltpu.CompilerParams(
            dimension_semantics=("arbitrary", "arbitrary"),
            vmem_limit_bytes=VMEM_LIMIT_BYTES),
        name="mix_in",
    )(*lead_args, *consts, *tables)


def _attention(q, k_t, v):
    batch, heads, lp, _ = q.shape
    tq = SEQ_TILE
    hps = HEADS_PER_STEP
    n_chunks = k_t.shape[2]
    assert k_t.shape[4] == tq
    return pl.pallas_call(
        _attn_kernel,
        grid=(batch, heads // hps, lp // tq),
        in_specs=[
            pl.BlockSpec((1, hps, tq, LANES), lambda b, hp, i: (b, hp, i, 0)),
            pl.BlockSpec((1, hps, n_chunks, LANES, tq), lambda b, hp, i: (b, hp, 0, 0, 0)),
            pl.BlockSpec((1, lp, hps * V_HEAD_DIM), lambda b, hp, i: (b, 0, hp)),
        ],
        out_specs=pl.BlockSpec((1, tq, hps * V_HEAD_DIM), lambda b, hp, i: (b, i, hp)),
        out_shape=jax.ShapeDtypeStruct((batch, lp, heads * V_HEAD_DIM), jnp.bfloat16),
        scratch_shapes=[
            pltpu.VMEM((SCORE_BUFFERS, tq, tq), jnp.float32),
            pltpu.VMEM((hps, tq, LANES), jnp.float32),
            pltpu.VMEM((hps, tq, LANES), jnp.float32),
        ],
        compiler_params=pltpu.CompilerParams(
            dimension_semantics=("arbitrary", "arbitrary", "arbitrary"),
            vmem_limit_bytes=VMEM_LIMIT_BYTES),
        name="mla_attn",
    )(q, k_t, v)


def _out_ffn(h2d, ypool2d, ymla2d, wo, ffn_g, wg, wu, wd, layer, real_rows=None):
    rows, d = h2d.shape
    per_call_rows = rows if real_rows is None else real_rows[3]
    tm = max(t for t in FFN_ROW_TILES if per_call_rows % t == 0)
    if real_rows is None:
        grid = (rows // tm,)
        in_row_spec = lambda width: pl.BlockSpec((tm, width), lambda i: (i, 0))
        out_spec = in_row_spec(d)
        out_rows = rows
    else:
        batch, lp, first, count = real_rows
        tiles = count // tm
        assert count % tm == 0 and first % BF16_ROWS == 0 and lp % BF16_ROWS == 0
        grid = (batch, tiles)
        in_row_spec = lambda width: pl.BlockSpec(
            (pl.Element(tm), pl.Element(width)),
            lambda b, j: (pl.multiple_of(b * lp + first + j * tm, BF16_ROWS), 0))
        out_spec = pl.BlockSpec((tm, d), lambda b, j: (b * tiles + j, 0))
        out_rows = batch * count
    return pl.pallas_call(
        _out_ffn_kernel,
        grid=grid,
        in_specs=[
            in_row_spec(d), in_row_spec(POOL_WIDTH), in_row_spec(MLA_HEADS * V_HEAD_DIM),
            _layer_spec(wo, layer), _layer_spec(ffn_g, layer),
            _layer_spec(wg, layer), _layer_spec(wu, layer), _layer_spec(wd, layer),
        ],
        out_specs=out_spec,
        out_shape=jax.ShapeDtypeStruct((out_rows, d), jnp.float32),
        scratch_shapes=[pltpu.VMEM((tm, d), jnp.float32)],
        compiler_params=pltpu.CompilerParams(
            dimension_semantics=("arbitrary",) * len(grid),
            vmem_limit_bytes=VMEM_LIMIT_BYTES),
        name="out_ffn",
    )(h2d, ypool2d, ymla2d, wo, ffn_g, wg, wu, wd)


def _rope_tables(lp):
    inv = 1.0 / (ROPE_THETA ** (jnp.arange(0, QK_ROPE_DIM, 2, dtype=jnp.float32) / QK_ROPE_DIM))
    ang = jnp.arange(lp, dtype=jnp.float32)[:, None] * inv[None, :]
    cos, sin = jnp.cos(ang), jnp.sin(ang)
    zeros = lambda n: jnp.zeros((lp, n), jnp.float32)
    tail = LANES - QK_HEAD_DIM
    cos_t = jnp.concatenate([jnp.ones((lp, QK_NOPE_DIM), jnp.float32), cos, cos, zeros(tail)], axis=1)
    sin_t = jnp.concatenate([zeros(QK_NOPE_DIM), -sin, sin, zeros(tail)], axis=1)
    return cos_t, sin_t, cos.T, sin.T


def kernel(x, meta_tokens, attn_norm_g, w_in, w_pool, pool_scale, q_a_norm_g, w_q_b,
           kv_a_norm_g, w_kv_b, q_norm_g, k_norm_g, w_out, ffn_norm_g, w_gate, w_up, w_down):
    batch, seq, d = x.shape
    depth = w_in.shape[0]
    length = N_META + seq
    lp = -(-length // SEQ_TILE) * SEQ_TILE
    assert (batch * lp) % min(FFN_ROW_TILES) == 0 and D_FF % FF_CHUNK == 0
    bf16 = jnp.bfloat16

    tables = _rope_tables(lp)
    h = None
    wo_all, wg_all, wu_all, wd_all = (w.astype(bf16) for w in (w_out, w_gate, w_up, w_down))
    ffn_g_all = ffn_norm_g[:, None, :]

    s3 = POOL_WIDTH + Q_LORA_RANK + KV_LORA_RANK
    rope_block = jnp.pad(w_in[:, :, s3:], ((0, 0), (0, 0), (QK_NOPE_DIM, LANES - QK_HEAD_DIM)))
    w_in_p = jnp.concatenate([w_in[:, :, :s3], rope_block], axis=2).astype(bf16)
    wq_t = w_q_b.transpose(0, 2, 1).astype(bf16)
    wkv = w_kv_b.reshape(depth, KV_LORA_RANK, MLA_HEADS, QK_NOPE_DIM + V_HEAD_DIM)
    wk_t = wkv[..., :QK_NOPE_DIM].reshape(depth, KV_LORA_RANK, MLA_HEADS * QK_NOPE_DIM)
    wk_t = wk_t.transpose(0, 2, 1).astype(bf16)
    wv = wkv[..., QK_NOPE_DIM:].reshape(depth, KV_LORA_RANK, MLA_HEADS * V_HEAD_DIM).astype(bf16)
    half = QK_ROPE_DIM // 2
    rows_t = lambda g: jnp.broadcast_to(g[:, :, None], g.shape + (SEQ_TILE,))
    gqn_t = rows_t(q_norm_g[:, :QK_NOPE_DIM])
    gq1_t = rows_t(q_norm_g[:, QK_NOPE_DIM:QK_NOPE_DIM + half])
    gq2_t = rows_t(q_norm_g[:, QK_NOPE_DIM + half:])
    gkn_t = jnp.broadcast_to(k_norm_g[:, :QK_NOPE_DIM, None], (depth, QK_NOPE_DIM, SEQ_TILE))
    gkr = jnp.pad(k_norm_g[:, None, QK_NOPE_DIM:],
                  ((0, 0), (0, 0), (QK_NOPE_DIM, LANES - QK_HEAD_DIM)))
    n_pairs = len(POOL_WINDOWS) // 2
    wp = w_pool.reshape(depth, n_pairs, 2, POOL_GROUP_DIM, POOL_GROUP_DIM)
    zero_blk = jnp.zeros_like(wp[:, :, 0])
    w_pool2 = jnp.concatenate([jnp.concatenate([wp[:, :, 0], zero_blk], axis=-1),
                               jnp.concatenate([zero_blk, wp[:, :, 1]], axis=-1)],
                              axis=-2).astype(bf16)
    consts = (attn_norm_g[:, None, :], w_in_p, w_pool2, pool_scale[:, None, :],
              q_a_norm_g[:, None, :], wq_t, kv_a_norm_g[:, None, :], wk_t, wv,
              gqn_t, gq1_t, gq2_t, gkn_t, gkr)

    for l in range(depth):
        if l == 0:
            ypool, q, k_t, v, h = _mix_in(None, consts, l, tables, lp,
                                          tokens_meta=(x, meta_tokens.astype(x.dtype)))
        else:
            ypool, q, k_t, v = _mix_in(h, consts, l, tables, lp)
        ymla = _attention(q, k_t, v)

        last = l == depth - 1
        h = _out_ffn(h.reshape(batch * lp, d), ypool.reshape(batch * lp, POOL_WIDTH),
                     ymla.reshape(batch * lp, MLA_HEADS * V_HEAD_DIM),
                     wo_all, ffn_g_all, wg_all, wu_all, wd_all, l,
                     real_rows=(batch, lp, N_META, seq) if last else None)
        h = h.reshape(batch, seq if last else lp, d)
    return h
```

```python
import functools

import jax
import jax.numpy as jnp
from jax import lax
from jax.experimental import pallas as pl
from jax.experimental.pallas import tpu as pltpu

D_MODEL = 1024
N_META = 16
POOL_WIDTH = 512
POOL_WINDOWS = (2, 4, 8, 16)
POOL_GROUP_DIM = 128
MLA_HEADS = 8
QK_NOPE_DIM = 64
QK_ROPE_DIM = 32
QK_HEAD_DIM = QK_NOPE_DIM + QK_ROPE_DIM
V_HEAD_DIM = 64
Q_LORA_RANK = 384
KV_LORA_RANK = 256
D_FF = 2816
ROPE_THETA = 10000.0
RMS_EPS = 1e-6

LANES = 128
SUBLANES = 8
BF16_ROWS = 2 * SUBLANES
assert POOL_WINDOWS == tuple(2 ** (k + 1) for k in range(len(POOL_WINDOWS)))
POOL_HALO = SUBLANES * len(POOL_WINDOWS)
assert POOL_HALO >= max(POOL_WINDOWS)

SEQ_TILE = 384
FFN_ROW_TILES = (512, 768, 1024)
FF_CHUNK = 256
FF_UNROLL = 5
HEADS_PER_STEP = 8
SCORE_LOOKAHEAD = 1
TAIL_LOOKAHEAD = 2
SCORE_BUFFERS = 4
assert SCORE_LOOKAHEAD <= TAIL_LOOKAHEAD < SCORE_BUFFERS and HEADS_PER_STEP % SCORE_BUFFERS == 0
IN_WIDTH_PADDED = 1280
VMEM_LIMIT_BYTES = 56 * 1024 * 1024

NEG_BIG = -1e30
LOG2_E = 1.4426950408889634


def _rms_scale(x, width):
    return lax.rsqrt(jnp.sum(x * x, axis=-1, keepdims=True) * (1.0 / width) + RMS_EPS)


N_MIX_SCRATCH = 6


def _mix_in_kernel(h_ref, *refs, n_tiles):
    _mix_in_steps(lambda: h_ref[0], refs, n_tiles)


def _mix_in_first_kernel(x_ref, meta_ref, *refs, seq, n_tiles):
    h_out_ref = refs[-N_MIX_SCRATCH - 1]
    refs = refs[:-N_MIX_SCRATCH - 1] + refs[-N_MIX_SCRATCH:]

    def load_x():
        i = jnp.minimum(pl.program_id(1), n_tiles - 1)
        blk = x_ref[0]
        tm = blk.shape[0]
        last_shift = (n_tiles - 1) * tm - N_META - (seq - tm)
        first_tile = jnp.concatenate([meta_ref[...], blk[:tm - N_META]], axis=0)
        last_tile = jnp.concatenate(
            [blk[last_shift:], jnp.zeros((last_shift, blk.shape[1]), blk.dtype)], axis=0)
        x = jnp.where(i == 0, first_tile, jnp.where(i == n_tiles - 1, last_tile, blk))
        h_out_ref[0] = x
        return x

    _mix_in_steps(load_x, refs, n_tiles)


def _mix_in_steps(load_x, refs, n_tiles):
    attn_g_ref, w_in_ref, *post_refs, za_ref, zb_ref = refs
    ue_ref = post_refs[-4]
    s = pl.program_id(1)
    even = s % 2 == 0

    @pl.when(s == 1)
    def _():
        ue_ref[0:POOL_HALO, :] = jnp.zeros((POOL_HALO, POOL_WIDTH), jnp.float32)

    def project(z_write):
        x = load_x()
        xn = (x * _rms_scale(x, D_MODEL) * attn_g_ref[0]).astype(jnp.bfloat16)
        z_write[...] = jnp.dot(xn, w_in_ref[0], preferred_element_type=jnp.float32)

    def finish(z_read):
        _mix_in_finish(z_read[...], s - 1, *post_refs)

    def both(z_write, z_read):
        project(z_write)
        finish(z_read)

    inner = jnp.logical_and(s > 0, s < n_tiles)
    pl.when(s == 0)(lambda: project(za_ref))
    pl.when(jnp.logical_and(inner, even))(lambda: both(za_ref, zb_ref))
    pl.when(jnp.logical_and(inner, jnp.logical_not(even)))(lambda: both(zb_ref, za_ref))
    pl.when(s == n_tiles)(lambda: finish(zb_ref if n_tiles % 2 == 0 else za_ref))


def _mix_in_finish(z, i, w_pool_ref, pool_scale_ref,
                   qa_g_ref, wqt_ref, kva_g_ref, wkt_ref, wv_ref,
                   gqn_ref, gq1_ref, gq2_ref, gkn_ref, gk1_ref, gk2_ref,
                   cosr_ref, sinr_ref,
                   ypool_ref, q_ref, k_ref, v_ref, ue_ref, s2_ref, s4_ref, s8_ref):
    tm = z.shape[0]
    halo = POOL_HALO

    u = z[:, :POOL_WIDTH]

    gd = POOL_GROUP_DIM
    rows = tm + halo
    ue_ref[halo:, :] = u
    r1, r2, r3, r4 = (k * SUBLANES for k in (1, 2, 3, 4))
    s2_ref[r1:, :] = ue_ref[r1:, :] + ue_ref[r1 - 1:rows - 1, :]
    s4_ref[r2:, :] = s2_ref[r2:, gd:] + s2_ref[r2 - 2:rows - 2, gd:]
    s8_ref[r3:, :] = s4_ref[r3:, gd:] + s4_ref[r3 - 4:rows - 4, gd:]
    s16 = s8_ref[r4:, gd:] + s8_ref[r4 - 8:rows - 8, gd:]
    window_sums = (s2_ref[halo:, 0:gd], s4_ref[halo:, 0:gd], s8_ref[halo:, 0:gd], s16)

    pos = i * tm + lax.broadcasted_iota(jnp.int32, (tm, 1), 0)
    pooled = []
    for g, w in enumerate(POOL_WINDOWS):
        cnt = jnp.minimum(pos + 1, w).astype(jnp.float32)
        pooled.append((window_sums[g] / cnt - u[:, g * gd:(g + 1) * gd]).astype(jnp.bfloat16))
    for pair in range(len(POOL_WINDOWS) // 2):
        sl = slice(2 * pair * gd, 2 * (pair + 1) * gd)
        p2 = jnp.concatenate(pooled[2 * pair:2 * pair + 2], axis=1)
        y2 = jnp.dot(p2, w_pool_ref[0, pair], preferred_element_type=jnp.float32)
        ypool_ref[0, :, sl] = (y2 * pool_scale_ref[0, :, sl]).astype(jnp.bfloat16)
    ue_ref[0:halo, :] = ue_ref[tm:tm + halo, :]

    scale = QK_HEAD_DIM ** -0.5 * LOG2_E

    c_q = z[:, POOL_WIDTH:POOL_WIDTH + Q_LORA_RANK]
    cqn = (c_q * _rms_scale(c_q, Q_LORA_RANK) * qa_g_ref[0]).astype(jnp.bfloat16)
    qf_t = lax.dot_general(wqt_ref[0], cqn, (((1,), (1,)), ((), ())),
                           preferred_element_type=jnp.float32)
    cos_r = cosr_ref[...]
    sin_r = sinr_ref[...]
    half = QK_ROPE_DIM // 2
    zero_rows = jnp.zeros((LANES - QK_HEAD_DIM, tm), jnp.float32)
    for hd in range(MLA_HEADS):
        r0 = hd * QK_HEAD_DIM
        xc = qf_t[r0:r0 + QK_NOPE_DIM, :]
        x1 = qf_t[r0 + QK_NOPE_DIM:r0 + QK_NOPE_DIM + half, :]
        x2 = qf_t[r0 + QK_NOPE_DIM + half:r0 + QK_HEAD_DIM, :]
        ss_n = jnp.sum(xc * xc, axis=0, keepdims=True)
        ss_r = jnp.sum(x1 * x1, axis=0, keepdims=True) + jnp.sum(x2 * x2, axis=0, keepdims=True)
        r_n = lax.rsqrt(ss_n * (1.0 / QK_NOPE_DIM) + RMS_EPS) * scale
        r_r = lax.rsqrt(ss_r * (1.0 / QK_ROPE_DIM) + RMS_EPS) * scale
        a = x1 * gq1_ref[0]
        b = x2 * gq2_ref[0]
        head_t = jnp.concatenate([xc * gqn_ref[0] * r_n,
                                  (a * cos_r - b * sin_r) * r_r,
                                  (a * sin_r + b * cos_r) * r_r,
                                  zero_rows], axis=0)
        q_ref[0, hd] = head_t.T.astype(jnp.bfloat16)

    kv0 = POOL_WIDTH + Q_LORA_RANK
    c_kv = z[:, kv0:kv0 + KV_LORA_RANK]
    ckvn = (c_kv * _rms_scale(c_kv, KV_LORA_RANK) * kva_g_ref[0]).astype(jnp.bfloat16)
    v_ref[0] = jnp.dot(ckvn, wv_ref[0],
                       preferred_element_type=jnp.float32).astype(jnp.bfloat16)

    kr_t = z[:, kv0 + KV_LORA_RANK:].T
    k1 = kr_t[QK_NOPE_DIM:QK_NOPE_DIM + half, :]
    k2 = kr_t[QK_NOPE_DIM + half:QK_HEAD_DIM, :]
    ss_k = jnp.sum(k1 * k1, axis=0, keepdims=True) + jnp.sum(k2 * k2, axis=0, keepdims=True)
    r_kr = lax.rsqrt(ss_k * (1.0 / QK_ROPE_DIM) + RMS_EPS)
    ka = k1 * gk1_ref[0]
    kb = k2 * gk2_ref[0]
    kr_rot_t = jnp.concatenate([(ka * cos_r - kb * sin_r) * r_kr,
                                (ka * sin_r + kb * cos_r) * r_kr,
                                zero_rows], axis=0).astype(jnp.bfloat16)

    kf_t = lax.dot_general(wkt_ref[0], ckvn, (((1,), (1,)), ((), ())),
                           preferred_element_type=jnp.float32)
    for hd in range(MLA_HEADS):
        kh = kf_t[hd * QK_NOPE_DIM:(hd + 1) * QK_NOPE_DIM, :]
        r_k = lax.rsqrt(jnp.sum(kh * kh, axis=0, keepdims=True) * (1.0 / QK_NOPE_DIM) + RMS_EPS)
        k_ref[0, hd, 0, 0:QK_NOPE_DIM, :] = (kh * r_k * gkn_ref[0]).astype(jnp.bfloat16)
        k_ref[0, hd, 0, QK_NOPE_DIM:, :] = kr_rot_t


def _attn_kernel(q_ref, k_ref, v_ref, o_ref, s_scr, m_scr, acc_scr):
    assert tuple(s_scr.shape[1:]) == (q_ref.shape[2], q_ref.shape[2])
    i = pl.program_id(2)
    tq = q_ref.shape[2]
    tk = s_scr.shape[2]
    n_slabs = tk // LANES
    lane = lax.broadcasted_iota(jnp.int32, (1, LANES), 1)
    low_half = lane < V_HEAD_DIM

    n_buf = s_scr.shape[0]
    heads = HEADS_PER_STEP

    def scores(hh, j):
        s_scr[hh % n_buf] = jnp.dot(q_ref[0, hh], k_ref[0, hh, j],
                                    preferred_element_type=jnp.float32)

    def softmax_pv(hh, j, masked):
        rb = LANES
        if masked:
            tri = (lax.broadcasted_iota(jnp.int32, (rb, LANES), 1)
                   <= lax.broadcasted_iota(jnp.int32, (rb, LANES), 0))
        alphas, ps = [], []
        for rblk in range(tq // rb):
            r0 = rblk * rb
            blk = []
            for c in range(n_slabs if not masked else rblk + 1):
                sl = s_scr[hh % n_buf, r0:r0 + rb, c * LANES:(c + 1) * LANES]
                if masked and c == rblk:
                    sl = jnp.where(tri, sl, NEG_BIG)
                blk.append(sl)
            mx = blk[0]
            for sl in blk[1:]:
                mx = jnp.maximum(mx, sl)
            m_old = m_scr[hh, r0:r0 + rb]
            m_new = jnp.maximum(m_old, jnp.max(mx, axis=-1, keepdims=True))
            m_scr[hh, r0:r0 + rb] = m_new
            alphas.append(jnp.exp2(m_old - m_new))
            p_blk = [jnp.exp2(sl - m_new).astype(jnp.bfloat16) for sl in blk]
            p_blk += [jnp.zeros((rb, LANES), jnp.bfloat16)] * (n_slabs - len(blk))
            ps.append(jnp.concatenate(p_blk, axis=1))
        alpha = jnp.concatenate(alphas, axis=0)
        p = jnp.concatenate(ps, axis=0)
        start = pl.multiple_of(j * tk, tk)
        pair = hh // 2
        vj = v_ref[0, pl.ds(start, tk), pair * LANES:(pair + 1) * LANES]
        own = low_half if hh % 2 == 0 else jnp.logical_not(low_half)
        vj = jnp.where(own, vj, jnp.ones_like(vj))
        acc_scr[hh] = alpha * acc_scr[hh] + jnp.dot(p, vj, preferred_element_type=jnp.float32)

    m_scr[...] = jnp.full(m_scr.shape, NEG_BIG, jnp.float32)
    acc_scr[...] = jnp.zeros(acc_scr.shape, jnp.float32)
    ahead = SCORE_LOOKAHEAD
    for hh in range(ahead):
        scores(hh, 0)

    def chunk(j):
        for hh in range(heads):
            scores((hh + ahead) % heads, j + (hh + ahead) // heads)
            softmax_pv(hh, j, masked=False)

    def chunk_pair(jj, carry):
        chunk(2 * jj)
        chunk(2 * jj + 1)
        return carry

    lax.fori_loop(0, lax.shift_right_logical(i, 1), chunk_pair, 0)

    @pl.when(i % 2 == 1)
    def _():
        chunk(i - 1)

    tail_ahead = TAIL_LOOKAHEAD
    for hh in range(ahead, tail_ahead):
        scores(hh, i)
    def finalize(pair):
        acc_e = acc_scr[2 * pair]
        acc_o = acc_scr[2 * pair + 1]
        out_e = acc_e / acc_e[:, V_HEAD_DIM:V_HEAD_DIM + 1]
        out_o = acc_o / acc_o[:, 0:1]
        o_ref[0, :, pair * LANES:(pair + 1) * LANES] = jnp.where(
            low_half, out_e, out_o).astype(jnp.bfloat16)

    for hh in range(heads):
        if hh + tail_ahead < heads:
            scores(hh + tail_ahead, i)
        softmax_pv(hh, i, masked=True)
        if hh % 2 == 1:
            finalize(hh // 2)


def _out_ffn_kernel(h_ref, ypool_ref, ymla_ref, wo_ref, ffn_g_ref,
                    wg_ref, wu_ref, wd_ref, o_ref, acc_ref):
    h1 = (h_ref[...]
          + jnp.dot(ypool_ref[...], wo_ref[0, :POOL_WIDTH, :], preferred_element_type=jnp.float32)
          + jnp.dot(ymla_ref[...], wo_ref[0, POOL_WIDTH:, :], preferred_element_type=jnp.float32))
    g = (h1 * _rms_scale(h1, D_MODEL) * ffn_g_ref[0]).astype(jnp.bfloat16)
    acc_ref[...] = h1

    def chunk_slice(c):
        return pl.ds(pl.multiple_of(c * FF_CHUNK, FF_CHUNK), FF_CHUNK)

    def gate_up(c):
        cols = chunk_slice(c)
        gate = jnp.dot(g, wg_ref[0, :, cols], preferred_element_type=jnp.float32)
        up = jnp.dot(g, wu_ref[0, :, cols], preferred_element_type=jnp.float32)
        return (gate * (1.0 / (1.0 + jnp.exp(-gate))) * up).astype(jnp.bfloat16)

    def down(act, c):
        acc_ref[...] += jnp.dot(act, wd_ref[0, chunk_slice(c), :],
                                preferred_element_type=jnp.float32)

    n_chunks = wg_ref.shape[2] // FF_CHUNK

    def chunk_group(first, count):
        act = gate_up(first)
        for k in range(count):
            nxt = gate_up(first + k + 1) if k + 1 < count else None
            down(act, first + k)
            act = nxt

    def loop_step(step, carry):
        chunk_group(step * FF_UNROLL, FF_UNROLL)
        return carry

    n_steps = n_chunks // FF_UNROLL
    lax.fori_loop(0, n_steps, loop_step, 0)
    if n_chunks % FF_UNROLL:
        chunk_group(n_steps * FF_UNROLL, n_chunks % FF_UNROLL)
    o_ref[...] = acc_ref[...]


def _const_spec(shape):
    nd = len(shape)
    return pl.BlockSpec(shape, lambda *_: (0,) * nd)


def _layer_spec(stacked, layer):
    shape = (1,) + stacked.shape[1:]
    nd = len(shape)
    return pl.BlockSpec(shape, lambda *_: (layer,) + (0,) * (nd - 1),
                        pipeline_mode=pl.Buffered(1))


def _mix_in(h, consts, layer, tables, lp, tokens_meta=None):
    first = tokens_meta is not None
    batch = tokens_meta[0].shape[0] if first else h.shape[0]
    d = D_MODEL
    tm = SEQ_TILE
    n_tiles = lp // tm
    proj = lambda s: jnp.minimum(s, n_tiles - 1)
    done = lambda s: jnp.maximum(s - 1, 0)
    in_row_spec = lambda width: pl.BlockSpec((1, tm, width), lambda b, s: (b, proj(s), 0))
    row_spec = lambda width: pl.BlockSpec((1, tm, width), lambda b, s: (b, done(s), 0))
    q_spec = pl.BlockSpec((1, MLA_HEADS, tm, LANES), lambda b, s: (b, 0, done(s), 0))
    kt_spec = pl.BlockSpec((1, MLA_HEADS, 1, LANES, tm), lambda b, s: (b, 0, done(s), 0, 0))
    table_t_spec = pl.BlockSpec((QK_ROPE_DIM // 2, tm), lambda b, s: (0, done(s)))
    out_specs = [row_spec(POOL_WIDTH), q_spec, kt_spec, row_spec(MLA_HEADS * V_HEAD_DIM)]
    out_shape = [
        jax.ShapeDtypeStruct((batch, lp, POOL_WIDTH), jnp.bfloat16),
        jax.ShapeDtypeStruct((batch, MLA_HEADS, lp, LANES), jnp.bfloat16),
        jax.ShapeDtypeStruct((batch, MLA_HEADS, n_tiles, LANES, tm), jnp.bfloat16),
        jax.ShapeDtypeStruct((batch, lp, MLA_HEADS * V_HEAD_DIM), jnp.bfloat16),
    ]
    scratch = [pltpu.VMEM((tm + POOL_HALO, POOL_WIDTH - drop * POOL_GROUP_DIM), jnp.float32)
               for drop in (0, 0, 1, 2)]
    scratch += [pltpu.VMEM((tm, IN_WIDTH_PADDED), jnp.float32)] * 2
    assert len(scratch) == N_MIX_SCRATCH
    if first:
        x, meta = tokens_meta
        seq = x.shape[1]
        body = functools.partial(_mix_in_first_kernel, seq=seq, n_tiles=n_tiles)
        x_spec = pl.BlockSpec(
            (pl.Element(1), pl.Element(tm), pl.Element(d)),
            lambda b, s: (b, pl.multiple_of(jnp.clip(proj(s) * tm - N_META, 0, seq - tm),
                                            SUBLANES), 0))
        lead_specs = [x_spec, _const_spec(meta.shape)]
        lead_args = [x, meta]
        out_specs.append(in_row_spec(d))
        out_shape.append(jax.ShapeDtypeStruct((batch, lp, d), jnp.float32))
    else:
        body = functools.partial(_mix_in_kernel, n_tiles=n_tiles)
        lead_specs = [in_row_spec(d)]
        lead_args = [h]
    return pl.pallas_call(
        body,
        grid=(batch, n_tiles + 1),
        in_specs=lead_specs + [_layer_spec(c, layer) for c in consts]
                 + [table_t_spec, table_t_spec],
        out_specs=out_specs,
        out_shape=out_shape,
        scratch_shapes=scratch,
        compiler_params=pltpu.CompilerParams(
            dimension_semantics=("arbitrary", "arbitrary"),
            vmem_limit_bytes=VMEM_LIMIT_BYTES),
        name="mix_in",
    )(*lead_args, *consts, *tables)


def _attention(q, k_t, v):
    batch, heads, lp, _ = q.shape
    tq = SEQ_TILE
    hps = HEADS_PER_STEP
    n_chunks = k_t.shape[2]
    assert k_t.shape[4] == tq
    return pl.pallas_call(
        _attn_kernel,
        grid=(batch, heads // hps, lp // tq),
        in_specs=[
            pl.BlockSpec((1, hps, tq, LANES), lambda b, hp, i: (b, hp, i, 0)),
            pl.BlockSpec((1, hps, n_chunks, LANES, tq), lambda b, hp, i: (b, hp, 0, 0, 0)),
            pl.BlockSpec((1, lp, hps * V_HEAD_DIM), lambda b, hp, i: (b, 0, hp)),
        ],
        out_specs=pl.BlockSpec((1, tq, hps * V_HEAD_DIM), lambda b, hp, i: (b, i, hp)),
        out_shape=jax.ShapeDtypeStruct((batch, lp, heads * V_HEAD_DIM), jnp.bfloat16),
        scratch_shapes=[
            pltpu.VMEM((SCORE_BUFFERS, tq, tq), jnp.float32),
            pltpu.VMEM((hps, tq, LANES), jnp.float32),
            pltpu.VMEM((hps, tq, LANES), jnp.float32),
        ],
        compiler_params=pltpu.CompilerParams(
            dimension_semantics=("arbitrary", "arbitrary", "arbitrary"),
            vmem_limit_bytes=VMEM_LIMIT_BYTES),
        name="mla_attn",
    )(q, k_t, v)


def _out_ffn(h2d, ypool2d, ymla2d, wo, ffn_g, wg, wu, wd, layer, real_rows=None):
    rows, d = h2d.shape
    per_call_rows = rows if real_rows is None else real_rows[3]
    tm = max(t for t in FFN_ROW_TILES if per_call_rows % t == 0)
    if real_rows is None:
        grid = (rows // tm,)
        in_row_spec = lambda width: pl.BlockSpec((tm, width), lambda i: (i, 0))
        out_spec = in_row_spec(d)
        out_rows = rows
    else:
        batch, lp, first, count = real_rows
        tiles = count // tm
        assert count % tm == 0 and first % BF16_ROWS == 0 and lp % BF16_ROWS == 0
        grid = (batch, tiles)
        in_row_spec = lambda width: pl.BlockSpec(
            (pl.Element(tm), pl.Element(width)),
            lambda b, j: (pl.multiple_of(b * lp + first + j * tm, BF16_ROWS), 0))
        out_spec = pl.BlockSpec((tm, d), lambda b, j: (b * tiles + j, 0))
        out_rows = batch * count
    return pl.pallas_call(
        _out_ffn_kernel,
        grid=grid,
        in_specs=[
            in_row_spec(d), in_row_spec(POOL_WIDTH), in_row_spec(MLA_HEADS * V_HEAD_DIM),
            _layer_spec(wo, layer), _layer_spec(ffn_g, layer),
            _layer_spec(wg, layer), _layer_spec(wu, layer), _layer_spec(wd, layer),
        ],
        out_specs=out_spec,
        out_shape=jax.ShapeDtypeStruct((out_rows, d), jnp.float32),
        scratch_shapes=[pltpu.VMEM((tm, d), jnp.float32)],
        compiler_params=pltpu.CompilerParams(
            dimension_semantics=("arbitrary",) * len(grid),
            vmem_limit_bytes=VMEM_LIMIT_BYTES),
        name="out_ffn",
    )(h2d, ypool2d, ymla2d, wo, ffn_g, wg, wu, wd)


def _rope_tables(lp):
    inv = 1.0 / (ROPE_THETA ** (jnp.arange(0, QK_ROPE_DIM, 2, dtype=jnp.float32) / QK_ROPE_DIM))
    ang = inv[:, None] * jnp.arange(lp, dtype=jnp.float32)[None, :]
    return jnp.cos(ang), jnp.sin(ang)


def kernel(x, meta_tokens, attn_norm_g, w_in, w_pool, pool_scale, q_a_norm_g, w_q_b,
           kv_a_norm_g, w_kv_b, q_norm_g, k_norm_g, w_out, ffn_norm_g, w_gate, w_up, w_down):
    batch, seq, d = x.shape
    depth = w_in.shape[0]
    length = N_META + seq
    lp = -(-length // SEQ_TILE) * SEQ_TILE
    assert (batch * lp) % min(FFN_ROW_TILES) == 0 and D_FF % FF_CHUNK == 0
    bf16 = jnp.bfloat16

    tables = _rope_tables(lp)
    h = None
    wo_all, wg_all, wu_all, wd_all = (w.astype(bf16) for w in (w_out, w_gate, w_up, w_down))
    ffn_g_all = ffn_norm_g[:, None, :]

    s3 = POOL_WIDTH + Q_LORA_RANK + KV_LORA_RANK
    rope_block = jnp.pad(w_in[:, :, s3:], ((0, 0), (0, 0), (QK_NOPE_DIM, LANES - QK_HEAD_DIM)))
    w_in_p = jnp.concatenate([w_in[:, :, :s3], rope_block], axis=2).astype(bf16)
    wq_t = w_q_b.transpose(0, 2, 1).astype(bf16)
    wkv = w_kv_b.reshape(depth, KV_LORA_RANK, MLA_HEADS, QK_NOPE_DIM + V_HEAD_DIM)
    wk_t = wkv[..., :QK_NOPE_DIM].reshape(depth, KV_LORA_RANK, MLA_HEADS * QK_NOPE_DIM)
    wk_t = wk_t.transpose(0, 2, 1).astype(bf16)
    wv = wkv[..., QK_NOPE_DIM:].reshape(depth, KV_LORA_RANK, MLA_HEADS * V_HEAD_DIM).astype(bf16)
    half = QK_ROPE_DIM // 2
    rows_t = lambda g: jnp.broadcast_to(g[:, :, None], g.shape + (SEQ_TILE,))
    gqn_t = rows_t(q_norm_g[:, :QK_NOPE_DIM])
    gq1_t = rows_t(q_norm_g[:, QK_NOPE_DIM:QK_NOPE_DIM + half])
    gq2_t = rows_t(q_norm_g[:, QK_NOPE_DIM + half:])
    gkn_t = jnp.broadcast_to(k_norm_g[:, :QK_NOPE_DIM, None], (depth, QK_NOPE_DIM, SEQ_TILE))
    gk1_t = rows_t(k_norm_g[:, QK_NOPE_DIM:QK_NOPE_DIM + half])
    gk2_t = rows_t(k_norm_g[:, QK_NOPE_DIM + half:])
    n_pairs = len(POOL_WINDOWS) // 2
    wp = w_pool.reshape(depth, n_pairs, 2, POOL_GROUP_DIM, POOL_GROUP_DIM)
    zero_blk = jnp.zeros_like(wp[:, :, 0])
    w_pool2 = jnp.concatenate([jnp.concatenate([wp[:, :, 0], zero_blk], axis=-1),
                               jnp.concatenate([zero_blk, wp[:, :, 1]], axis=-1)],
                              axis=-2).astype(bf16)
    consts = (attn_norm_g[:, None, :], w_in_p, w_pool2, pool_scale[:, None, :],
              q_a_norm_g[:, None, :], wq_t, kv_a_norm_g[:, None, :], wk_t, wv,
              gqn_t, gq1_t, gq2_t, gkn_t, gk1_t, gk2_t)

    for l in range(depth):
        if l == 0:
            ypool, q, k_t, v, h = _mix_in(None, consts, l, tables, lp,
                                          tokens_meta=(x, meta_tokens.astype(x.dtype)))
        else:
            ypool, q, k_t, v = _mix_in(h, consts, l, tables, lp)
        ymla = _attention(q, k_t, v)

        last = l == depth - 1
        h = _out_ffn(h.reshape(batch * lp, d), ypool.reshape(batch * lp, POOL_WIDTH),
                     ymla.reshape(batch * lp, MLA_HEADS * V_HEAD_DIM),
                     wo_all, ffn_g_all, wg_all, wu_all, wd_all, l,
                     real_rows=(batch, lp, N_META, seq) if last else None)
        h = h.reshape(batch, seq if last else lp, d)
    return h
```

```python
import functools

import jax
import jax.numpy as jnp
from jax import lax
from jax.experimental import pallas as pl
from jax.experimental.pallas import tpu as pltpu

D_MODEL = 1024
N_META = 16
POOL_WIDTH = 512
POOL_WINDOWS = (2, 4, 8, 16)
POOL_GROUP_DIM = 128
MLA_HEADS = 8
QK_NOPE_DIM = 64
QK_ROPE_DIM = 32
QK_HEAD_DIM = QK_NOPE_DIM + QK_ROPE_DIM
V_HEAD_DIM = 64
Q_LORA_RANK = 384
KV_LORA_RANK = 256
D_FF = 2816
ROPE_THETA = 10000.0
RMS_EPS = 1e-6

LANES = 128
SUBLANES = 8
BF16_ROWS = 2 * SUBLANES
assert POOL_WINDOWS == tuple(2 ** (k + 1) for k in range(len(POOL_WINDOWS)))
POOL_HALO = SUBLANES * len(POOL_WINDOWS)
assert POOL_HALO >= max(POOL_WINDOWS)

SEQ_TILE = 384
FFN_ROW_TILES = (512, 768, 1024)
FF_CHUNK = 256
FF_UNROLL = 5
HEADS_PER_STEP = 8
SCORE_LOOKAHEAD = 1
TAIL_LOOKAHEAD = 2
SCORE_BUFFERS = 4
assert SCORE_LOOKAHEAD <= TAIL_LOOKAHEAD < SCORE_BUFFERS and HEADS_PER_STEP % SCORE_BUFFERS == 0
IN_WIDTH_PADDED = 1280
VMEM_LIMIT_BYTES = 56 * 1024 * 1024

NEG_BIG = -1e30
LOG2_E = 1.4426950408889634


def _rms_scale(x, width):
    return lax.rsqrt(jnp.sum(x * x, axis=-1, keepdims=True) * (1.0 / width) + RMS_EPS)


N_MIX_SCRATCH = 6


def _mix_in_kernel(h_ref, *refs, n_tiles):
    _mix_in_steps(lambda: h_ref[0], refs, n_tiles)


def _mix_in_first_kernel(x_ref, meta_ref, *refs, seq, n_tiles):
    h_out_ref = refs[-N_MIX_SCRATCH - 1]
    refs = refs[:-N_MIX_SCRATCH - 1] + refs[-N_MIX_SCRATCH:]

    def load_x():
        i = jnp.minimum(pl.program_id(1), n_tiles - 1)
        blk = x_ref[0]
        tm = blk.shape[0]
        last_shift = (n_tiles - 1) * tm - N_META - (seq - tm)
        first_tile = jnp.concatenate([meta_ref[...], blk[:tm - N_META]], axis=0)
        last_tile = jnp.concatenate(
            [blk[last_shift:], jnp.zeros((last_shift, blk.shape[1]), blk.dtype)], axis=0)
        x = jnp.where(i == 0, first_tile, jnp.where(i == n_tiles - 1, last_tile, blk))
        h_out_ref[0] = x
        return x

    _mix_in_steps(load_x, refs, n_tiles)


def _mix_in_steps(load_x, refs, n_tiles):
    attn_g_ref, w_in_ref, *post_refs, za_ref, zb_ref = refs
    ue_ref = post_refs[-4]
    s = pl.program_id(1)
    even = s % 2 == 0

    @pl.when(s == 1)
    def _():
        ue_ref[0:POOL_HALO, :] = jnp.zeros((POOL_HALO, POOL_WIDTH), jnp.float32)

    def project(z_write):
        x = load_x()
        xn = (x * _rms_scale(x, D_MODEL) * attn_g_ref[0]).astype(jnp.bfloat16)
        z_write[...] = jnp.dot(xn, w_in_ref[0], preferred_element_type=jnp.float32)

    def finish(z_read):
        _mix_in_finish(z_read[...], s - 1, *post_refs)

    def both(z_write, z_read):
        project(z_write)
        finish(z_read)

    inner = jnp.logical_and(s > 0, s < n_tiles)
    pl.when(s == 0)(lambda: project(za_ref))
    pl.when(jnp.logical_and(inner, even))(lambda: both(za_ref, zb_ref))
    pl.when(jnp.logical_and(inner, jnp.logical_not(even)))(lambda: both(zb_ref, za_ref))
    pl.when(s == n_tiles)(lambda: finish(zb_ref if n_tiles % 2 == 0 else za_ref))


def _mix_in_finish(z, i, w_pool_ref, pool_scale_ref,
                   qa_g_ref, wqt_ref, kva_g_ref, wkt_ref, wv_ref,
                   gqn_ref, gq1_ref, gq2_ref, gkn_ref, gk1_ref, gk2_ref,
                   cosr_ref, sinr_ref,
                   ypool_ref, q_ref, k_ref, v_ref, ue_ref, s2_ref, s4_ref, s8_ref):
    tm = z.shape[0]
    halo = POOL_HALO

    u = z[:, :POOL_WIDTH]

    gd = POOL_GROUP_DIM
    rows = tm + halo
    ue_ref[halo:, :] = u
    r1, r2, r3, r4 = (k * SUBLANES for k in (1, 2, 3, 4))
    s2_ref[r1:, :] = ue_ref[r1:, :] + ue_ref[r1 - 1:rows - 1, :]
    s4_ref[r2:, :] = s2_ref[r2:, gd:] + s2_ref[r2 - 2:rows - 2, gd:]
    s8_ref[r3:, :] = s4_ref[r3:, gd:] + s4_ref[r3 - 4:rows - 4, gd:]
    s16 = s8_ref[r4:, gd:] + s8_ref[r4 - 8:rows - 8, gd:]
    window_sums = (s2_ref[halo:, 0:gd], s4_ref[halo:, 0:gd], s8_ref[halo:, 0:gd], s16)

    pos = i * tm + lax.broadcasted_iota(jnp.int32, (tm, 1), 0)
    pooled = []
    for g, w in enumerate(POOL_WINDOWS):
        cnt = jnp.minimum(pos + 1, w).astype(jnp.float32)
        pooled.append((window_sums[g] / cnt - u[:, g * gd:(g + 1) * gd]).astype(jnp.bfloat16))
    for pair in range(len(POOL_WINDOWS) // 2):
        sl = slice(2 * pair * gd, 2 * (pair + 1) * gd)
        p2 = jnp.concatenate(pooled[2 * pair:2 * pair + 2], axis=1)
        y2 = jnp.dot(p2, w_pool_ref[0, pair], preferred_element_type=jnp.float32)
        ypool_ref[0, :, sl] = (y2 * pool_scale_ref[0, :, sl]).astype(jnp.bfloat16)
    ue_ref[0:halo, :] = ue_ref[tm:tm + halo, :]

    scale = QK_HEAD_DIM ** -0.5 * LOG2_E

    c_q = z[:, POOL_WIDTH:POOL_WIDTH + Q_LORA_RANK]
    cqn = (c_q * _rms_scale(c_q, Q_LORA_RANK) * qa_g_ref[0]).astype(jnp.bfloat16)
    qf_t = lax.dot_general(wqt_ref[0], cqn, (((1,), (1,)), ((), ())),
                           preferred_element_type=jnp.float32)
    cos_r = cosr_ref[...]
    sin_r = sinr_ref[...]
    half = QK_ROPE_DIM // 2
    zero_rows = jnp.zeros((LANES - QK_HEAD_DIM, tm), jnp.float32)
    for hd in range(MLA_HEADS):
        r0 = hd * QK_HEAD_DIM
        xc = qf_t[r0:r0 + QK_NOPE_DIM, :]
        x1 = qf_t[r0 + QK_NOPE_DIM:r0 + QK_NOPE_DIM + half, :]
        x2 = qf_t[r0 + QK_NOPE_DIM + half:r0 + QK_HEAD_DIM, :]
        ss_n = jnp.sum(xc * xc, axis=0, keepdims=True)
        ss_r = jnp.sum(x1 * x1, axis=0, keepdims=True) + jnp.sum(x2 * x2, axis=0, keepdims=True)
        r_n = lax.rsqrt(ss_n * (1.0 / QK_NOPE_DIM) + RMS_EPS) * scale
        r_r = lax.rsqrt(ss_r * (1.0 / QK_ROPE_DIM) + RMS_EPS) * scale
        a = x1 * gq1_ref[0]
        b = x2 * gq2_ref[0]
        head_t = jnp.concatenate([xc * gqn_ref[0] * r_n,
                                  (a * cos_r - b * sin_r) * r_r,
                                  (a * sin_r + b * cos_r) * r_r,
                                  zero_rows], axis=0)
        q_ref[0, hd] = head_t.T.astype(jnp.bfloat16)

    kv0 = POOL_WIDTH + Q_LORA_RANK
    c_kv = z[:, kv0:kv0 + KV_LORA_RANK]
    ckvn = (c_kv * _rms_scale(c_kv, KV_LORA_RANK) * kva_g_ref[0]).astype(jnp.bfloat16)
    v_ref[0] = jnp.dot(ckvn, wv_ref[0],
                       preferred_element_type=jnp.float32).astype(jnp.bfloat16)

    kr_t = z[:, kv0 + KV_LORA_RANK:].T
    k1 = kr_t[QK_NOPE_DIM:QK_NOPE_DIM + half, :]
    k2 = kr_t[QK_NOPE_DIM + half:QK_HEAD_DIM, :]
    ss_k = jnp.sum(k1 * k1, axis=0, keepdims=True) + jnp.sum(k2 * k2, axis=0, keepdims=True)
    r_kr = lax.rsqrt(ss_k * (1.0 / QK_ROPE_DIM) + RMS_EPS)
    ka = k1 * gk1_ref[0]
    kb = k2 * gk2_ref[0]
    kr_rot_t = jnp.concatenate([(ka * cos_r - kb * sin_r) * r_kr,
                                (ka * sin_r + kb * cos_r) * r_kr,
                                zero_rows], axis=0).astype(jnp.bfloat16)

    kf_t = lax.dot_general(wkt_ref[0], ckvn, (((1,), (1,)), ((), ())),
                           preferred_element_type=jnp.float32)
    for hd in range(MLA_HEADS):
        kh = kf_t[hd * QK_NOPE_DIM:(hd + 1) * QK_NOPE_DIM, :]
        r_k = lax.rsqrt(jnp.sum(kh * kh, axis=0, keepdims=True) * (1.0 / QK_NOPE_DIM) + RMS_EPS)
        k_ref[0, hd, 0, 0:QK_NOPE_DIM, :] = (kh * r_k * gkn_ref[0]).astype(jnp.bfloat16)
        k_ref[0, hd, 0, QK_NOPE_DIM:, :] = kr_rot_t


def _attn_kernel(q_ref, k_ref, v_ref, o_ref, s_scr, m_scr, acc_scr):
    assert tuple(s_scr.shape[1:]) == (q_ref.shape[2], q_ref.shape[2])
    i = pl.program_id(2)
    tq = q_ref.shape[2]
    tk = s_scr.shape[2]
    n_slabs = tk // LANES
    lane = lax.broadcasted_iota(jnp.int32, (1, LANES), 1)
    low_half = lane < V_HEAD_DIM

    n_buf = s_scr.shape[0]
    heads = HEADS_PER_STEP

    def scores(hh, j):
        s_scr[hh % n_buf] = jnp.dot(q_ref[0, hh], k_ref[0, hh, j],
                                    preferred_element_type=jnp.float32)

    def softmax_pv(hh, j, masked):
        rb = LANES
        if masked:
            tri = (lax.broadcasted_iota(jnp.int32, (rb, LANES), 1)
                   <= lax.broadcasted_iota(jnp.int32, (rb, LANES), 0))
        alphas, ps = [], []
        for rblk in range(tq // rb):
            r0 = rblk * rb
            blk = []
            for c in range(n_slabs if not masked else rblk + 1):
                sl = s_scr[hh % n_buf, r0:r0 + rb, c * LANES:(c + 1) * LANES]
                if masked and c == rblk:
                    sl = jnp.where(tri, sl, NEG_BIG)
                blk.append(sl)
            mx = blk[0]
            for sl in blk[1:]:
                mx = jnp.maximum(mx, sl)
            m_old = m_scr[hh, r0:r0 + rb]
            m_new = jnp.maximum(m_old, jnp.max(mx, axis=-1, keepdims=True))
            m_scr[hh, r0:r0 + rb] = m_new
            alphas.append(jnp.exp2(m_old - m_new))
            p_blk = [jnp.exp2(sl - m_new).astype(jnp.bfloat16) for sl in blk]
            p_blk += [jnp.zeros((rb, LANES), jnp.bfloat16)] * (n_slabs - len(blk))
            ps.append(jnp.concatenate(p_blk, axis=1))
        alpha = jnp.concatenate(alphas, axis=0)
        p = jnp.concatenate(ps, axis=0)
        start = pl.multiple_of(j * tk, tk)
        pair = hh // 2
        vj = v_ref[0, pl.ds(start, tk), pair * LANES:(pair + 1) * LANES]
        own = low_half if hh % 2 == 0 else jnp.logical_not(low_half)
        vj = jnp.where(own, vj, jnp.ones_like(vj))
        acc_scr[hh] = alpha * acc_scr[hh] + jnp.dot(p, vj, preferred_element_type=jnp.float32)

    m_scr[...] = jnp.full(m_scr.shape, NEG_BIG, jnp.float32)
    acc_scr[...] = jnp.zeros(acc_scr.shape, jnp.float32)
    ahead = SCORE_LOOKAHEAD
    for hh in range(ahead):
        scores(hh, 0)

    def chunk(j):
        for hh in range(heads):
            scores((hh + ahead) % heads, j + (hh + ahead) // heads)
            softmax_pv(hh, j, masked=False)

    def chunk_pair(jj, carry):
        chunk(2 * jj)
        chunk(2 * jj + 1)
        return carry

    lax.fori_loop(0, lax.shift_right_logical(i, 1), chunk_pair, 0)

    @pl.when(i % 2 == 1)
    def _():
        chunk(i - 1)

    tail_ahead = TAIL_LOOKAHEAD
    for hh in range(ahead, tail_ahead):
        scores(hh, i)
    def finalize(pair):
        acc_e = acc_scr[2 * pair]
        acc_o = acc_scr[2 * pair + 1]
        out_e = acc_e / acc_e[:, V_HEAD_DIM:V_HEAD_DIM + 1]
        out_o = acc_o / acc_o[:, 0:1]
        o_ref[0, :, pair * LANES:(pair + 1) * LANES] = jnp.where(
            low_half, out_e, out_o).astype(jnp.bfloat16)

    for hh in range(heads):
        if hh + tail_ahead < heads:
            scores(hh + tail_ahead, i)
        softmax_pv(hh, i, masked=True)
        if hh % 2 == 1:
            finalize(hh // 2)


def _out_ffn_kernel(h_ref, ypool_ref, ymla_ref, wo_ref, ffn_g_ref,
                    wg_ref, wu_ref, wd_ref, o_ref, acc_ref):
    y = jnp.concatenate([ypool_ref[...], ymla_ref[...]], axis=1)
    h1 = h_ref[...] + jnp.dot(y, wo_ref[0], preferred_element_type=jnp.float32)
    g = (h1 * _rms_scale(h1, D_MODEL) * ffn_g_ref[0]).astype(jnp.bfloat16)
    acc_ref[...] = h1

    def chunk_slice(c):
        return pl.ds(pl.multiple_of(c * FF_CHUNK, FF_CHUNK), FF_CHUNK)

    def gate_up(c):
        cols = chunk_slice(c)
        gate = jnp.dot(g, wg_ref[0, :, cols], preferred_element_type=jnp.float32)
        up = jnp.dot(g, wu_ref[0, :, cols], preferred_element_type=jnp.float32)
        return (gate * (1.0 / (1.0 + jnp.exp(-gate))) * up).astype(jnp.bfloat16)

    def down(act, c):
        acc_ref[...] += jnp.dot(act, wd_ref[0, chunk_slice(c), :],
                                preferred_element_type=jnp.float32)

    n_chunks = wg_ref.shape[2] // FF_CHUNK

    def chunk_group(first, count):
        act = gate_up(first)
        for k in range(count):
            nxt = gate_up(first + k + 1) if k + 1 < count else None
            down(act, first + k)
            act = nxt

    def loop_step(step, carry):
        chunk_group(step * FF_UNROLL, FF_UNROLL)
        return carry

    n_steps = n_chunks // FF_UNROLL
    lax.fori_loop(0, n_steps, loop_step, 0)
    if n_chunks % FF_UNROLL:
        chunk_group(n_steps * FF_UNROLL, n_chunks % FF_UNROLL)
    o_ref[...] = acc_ref[...]


def _const_spec(shape):
    nd = len(shape)
    return pl.BlockSpec(shape, lambda *_: (0,) * nd)


def _layer_spec(stacked, layer):
    shape = (1,) + stacked.shape[1:]
    nd = len(shape)
    return pl.BlockSpec(shape, lambda *_: (layer,) + (0,) * (nd - 1),
                        pipeline_mode=pl.Buffered(1))


def _mix_in(h, consts, layer, tables, lp, tokens_meta=None):
    first = tokens_meta is not None
    batch = tokens_meta[0].shape[0] if first else h.shape[0]
    d = D_MODEL
    tm = SEQ_TILE
    n_tiles = lp // tm
    proj = lambda s: jnp.minimum(s, n_tiles - 1)
    done = lambda s: jnp.maximum(s - 1, 0)
    in_row_spec = lambda width: pl.BlockSpec((1, tm, width), lambda b, s: (b, proj(s), 0))
    row_spec = lambda width: pl.BlockSpec((1, tm, width), lambda b, s: (b, done(s), 0))
    q_spec = pl.BlockSpec((1, MLA_HEADS, tm, LANES), lambda b, s: (b, 0, done(s), 0))
    kt_spec = pl.BlockSpec((1, MLA_HEADS, 1, LANES, tm), lambda b, s: (b, 0, done(s), 0, 0))
    table_t_spec = pl.BlockSpec((QK_ROPE_DIM // 2, tm), lambda b, s: (0, done(s)))
    out_specs = [row_spec(POOL_WIDTH), q_spec, kt_spec, row_spec(MLA_HEADS * V_HEAD_DIM)]
    out_shape = [
        jax.ShapeDtypeStruct((batch, lp, POOL_WIDTH), jnp.bfloat16),
        jax.ShapeDtypeStruct((batch, MLA_HEADS, lp, LANES), jnp.bfloat16),
        jax.ShapeDtypeStruct((batch, MLA_HEADS, n_tiles, LANES, tm), jnp.bfloat16),
        jax.ShapeDtypeStruct((batch, lp, MLA_HEADS * V_HEAD_DIM), jnp.bfloat16),
    ]
    scratch = [pltpu.VMEM((tm + POOL_HALO, POOL_WIDTH - drop * POOL_GROUP_DIM), jnp.float32)
               for drop in (0, 0, 1, 2)]
    scratch += [pltpu.VMEM((tm, IN_WIDTH_PADDED), jnp.float32)] * 2
    assert len(scratch) == N_MIX_SCRATCH
    if first:
        x, meta = tokens_meta
        seq = x.shape[1]
        body = functools.partial(_mix_in_first_kernel, seq=seq, n_tiles=n_tiles)
        x_spec = pl.BlockSpec(
            (pl.Element(1), pl.Element(tm), pl.Element(d)),
            lambda b, s: (b, pl.multiple_of(jnp.clip(proj(s) * tm - N_META, 0, seq - tm),
                                            SUBLANES), 0))
        lead_specs = [x_spec, _const_spec(meta.shape)]
        lead_args = [x, meta]
        out_specs.append(in_row_spec(d))
        out_shape.append(jax.ShapeDtypeStruct((batch, lp, d), jnp.float32))
    else:
        body = functools.partial(_mix_in_kernel, n_tiles=n_tiles)
        lead_specs = [in_row_spec(d)]
        lead_args = [h]
    return pl.pallas_call(
        body,
        grid=(batch, n_tiles + 1),
        in_specs=lead_specs + [_layer_spec(c, layer) for c in consts]
                 + [table_t_spec, table_t_spec],
        out_specs=out_specs,
        out_shape=out_shape,
        scratch_shapes=scratch,
        compiler_params=pltpu.CompilerParams(
            dimension_semantics=("arbitrary", "arbitrary"),
            vmem_limit_bytes=VMEM_LIMIT_BYTES),
        name="mix_in",
    )(*lead_args, *consts, *tables)


def _attention(q, k_t, v):
    batch, heads, lp, _ = q.shape
    tq = SEQ_TILE
    hps = HEADS_PER_STEP
    n_chunks = k_t.shape[2]
    assert k_t.shape[4] == tq
    return pl.pallas_call(
        _attn_kernel,
        grid=(batch, heads // hps, lp // tq),
        in_specs=[
            pl.BlockSpec((1, hps, tq, LANES), lambda b, hp, i: (b, hp, i, 0)),
            pl.BlockSpec((1, hps, n_chunks, LANES, tq), lambda b, hp, i: (b, hp, 0, 0, 0)),
            pl.BlockSpec((1, lp, hps * V_HEAD_DIM), lambda b, hp, i: (b, 0, hp)),
        ],
        out_specs=pl.BlockSpec((1, tq, hps * V_HEAD_DIM), lambda b, hp, i: (b, i, hp)),
        out_shape=jax.ShapeDtypeStruct((batch, lp, heads * V_HEAD_DIM), jnp.bfloat16),
        scratch_shapes=[
            pltpu.VMEM((SCORE_BUFFERS, tq, tq), jnp.float32),
            pltpu.VMEM((hps, tq, LANES), jnp.float32),
            pltpu.VMEM((hps, tq, LANES), jnp.float32),
        ],
        compiler_params=pltpu.CompilerParams(
            dimension_semantics=("arbitrary", "arbitrary", "arbitrary"),
            vmem_limit_bytes=VMEM_LIMIT_BYTES),
        name="mla_attn",
    )(q, k_t, v)


def _out_ffn(h2d, ypool2d, ymla2d, wo, ffn_g, wg, wu, wd, layer, real_rows=None):
    rows, d = h2d.shape
    per_call_rows = rows if real_rows is None else real_rows[3]
    tm = max(t for t in FFN_ROW_TILES if per_call_rows % t == 0)
    if real_rows is None:
        grid = (rows // tm,)
        in_row_spec = lambda width: pl.BlockSpec((tm, width), lambda i: (i, 0))
        out_spec = in_row_spec(d)
        out_rows = rows
    else:
        batch, lp, first, count = real_rows
        tiles = count // tm
        assert count % tm == 0 and first % BF16_ROWS == 0 and lp % BF16_ROWS == 0
        grid = (batch, tiles)
        in_row_spec = lambda width: pl.BlockSpec(
            (pl.Element(tm), pl.Element(width)),
            lambda b, j: (pl.multiple_of(b * lp + first + j * tm, BF16_ROWS), 0))
        out_spec = pl.BlockSpec((tm, d), lambda b, j: (b * tiles + j, 0))
        out_rows = batch * count
    return pl.pallas_call(
        _out_ffn_kernel,
        grid=grid,
        in_specs=[
            in_row_spec(d), in_row_spec(POOL_WIDTH), in_row_spec(MLA_HEADS * V_HEAD_DIM),
            _layer_spec(wo, layer), _layer_spec(ffn_g, layer),
            _layer_spec(wg, layer), _layer_spec(wu, layer), _layer_spec(wd, layer),
        ],
        out_specs=out_spec,
        out_shape=jax.ShapeDtypeStruct((out_rows, d), jnp.float32),
        scratch_shapes=[pltpu.VMEM((tm, d), jnp.float32)],
        compiler_params=pltpu.CompilerParams(
            dimension_semantics=("arbitrary",) * len(grid),
            vmem_limit_bytes=VMEM_LIMIT_BYTES),
        name="out_ffn",
    )(h2d, ypool2d, ymla2d, wo, ffn_g, wg, wu, wd)


def _rope_tables(lp):
    inv = 1.0 / (ROPE_THETA ** (jnp.arange(0, QK_ROPE_DIM, 2, dtype=jnp.float32) / QK_ROPE_DIM))
    ang = inv[:, None] * jnp.arange(lp, dtype=jnp.float32)[None, :]
    return jnp.cos(ang), jnp.sin(ang)


def kernel(x, meta_tokens, attn_norm_g, w_in, w_pool, pool_scale, q_a_norm_g, w_q_b,
           kv_a_norm_g, w_kv_b, q_norm_g, k_norm_g, w_out, ffn_norm_g, w_gate, w_up, w_down):
    batch, seq, d = x.shape
    depth = w_in.shape[0]
    length = N_META + seq
    lp = -(-length // SEQ_TILE) * SEQ_TILE
    assert (batch * lp) % min(FFN_ROW_TILES) == 0 and D_FF % FF_CHUNK == 0
    bf16 = jnp.bfloat16

    tables = _rope_tables(lp)
    h = None
    wo_all, wg_all, wu_all, wd_all = (w.astype(bf16) for w in (w_out, w_gate, w_up, w_down))
    ffn_g_all = ffn_norm_g[:, None, :]

    s3 = POOL_WIDTH + Q_LORA_RANK + KV_LORA_RANK
    rope_block = jnp.pad(w_in[:, :, s3:], ((0, 0), (0, 0), (QK_NOPE_DIM, LANES - QK_HEAD_DIM)))
    w_in_p = jnp.concatenate([w_in[:, :, :s3], rope_block], axis=2).astype(bf16)
    wq_t = w_q_b.transpose(0, 2, 1).astype(bf16)
    wkv = w_kv_b.reshape(depth, KV_LORA_RANK, MLA_HEADS, QK_NOPE_DIM + V_HEAD_DIM)
    wk_t = wkv[..., :QK_NOPE_DIM].reshape(depth, KV_LORA_RANK, MLA_HEADS * QK_NOPE_DIM)
    wk_t = wk_t.transpose(0, 2, 1).astype(bf16)
    wv = wkv[..., QK_NOPE_DIM:].reshape(depth, KV_LORA_RANK, MLA_HEADS * V_HEAD_DIM).astype(bf16)
    half = QK_ROPE_DIM // 2
    rows_t = lambda g: jnp.broadcast_to(g[:, :, None], g.shape + (SEQ_TILE,))
    gqn_t = rows_t(q_norm_g[:, :QK_NOPE_DIM])
    gq1_t = rows_t(q_norm_g[:, QK_NOPE_DIM:QK_NOPE_DIM + half])
    gq2_t = rows_t(q_norm_g[:, QK_NOPE_DIM + half:])
    gkn_t = jnp.broadcast_to(k_norm_g[:, :QK_NOPE_DIM, None], (depth, QK_NOPE_DIM, SEQ_TILE))
    gk1_t = rows_t(k_norm_g[:, QK_NOPE_DIM:QK_NOPE_DIM + half])
    gk2_t = rows_t(k_norm_g[:, QK_NOPE_DIM + half:])
    n_pairs = len(POOL_WINDOWS) // 2
    wp = w_pool.reshape(depth, n_pairs, 2, POOL_GROUP_DIM, POOL_GROUP_DIM)
    zero_blk = jnp.zeros_like(wp[:, :, 0])
    w_pool2 = jnp.concatenate([jnp.concatenate([wp[:, :, 0], zero_blk], axis=-1),
                               jnp.concatenate([zero_blk, wp[:, :, 1]], axis=-1)],
                              axis=-2).astype(bf16)
    consts = (attn_norm_g[:, None, :], w_in_p, w_pool2, pool_scale[:, None, :],
              q_a_norm_g[:, None, :], wq_t, kv_a_norm_g[:, None, :], wk_t, wv,
              gqn_t, gq1_t, gq2_t, gkn_t, gk1_t, gk2_t)

    for l in range(depth):
        if l == 0:
            ypool, q, k_t, v, h = _mix_in(None, consts, l, tables, lp,
                                          tokens_meta=(x, meta_tokens.astype(x.dtype)))
        else:
            ypool, q, k_t, v = _mix_in(h, consts, l, tables, lp)
        ymla = _attention(q, k_t, v)

        last = l == depth - 1
        h = _out_ffn(h.reshape(batch * lp, d), ypool.reshape(batch * lp, POOL_WIDTH),
                     ymla.reshape(batch * lp, MLA_HEADS * V_HEAD_DIM),
                     wo_all, ffn_g_all, wg_all, wu_all, wd_all, l,
                     real_rows=(batch, lp, N_META, seq) if last else None)
        h = h.reshape(batch, seq if last else lp, d)
    return h
```
